```python
import math
import jax
import jax.numpy as jnp
from jax import lax
import numpy as np

D_MODEL = 1024
BATCH = 2
SEQ = 8192
DEPTH = 4
DEC_BATCH = 128
DEC_SEQ = 8
PAST_LEN = 8192
PAGE_SIZE = 128

N_META = 16
N_EVEN = (DEPTH + 1) // 2
N_ODD = DEPTH // 2
D_SSM = D_MODEL // 2
SSM_GROUP = 16
N_SSM_GROUPS = D_SSM // SSM_GROUP
SSM_STATE = 64
HEAD_DIM = 64
N_Q_HEADS = (D_MODEL // 2) // HEAD_DIM
N_KV_HEADS = 2
GQ = N_Q_HEADS // N_KV_HEADS
D_Q = N_Q_HEADS * HEAD_DIM
D_KV = N_KV_HEADS * HEAD_DIM
D_IN_MIX = D_SSM + D_Q + 2 * D_KV
D_OUT_MIX = D_SSM + D_Q
WINDOW = 128
BLOCK = 128
N_BUCKETS = 32
MAX_DISTANCE = 128
CONV_DIM = D_MODEL
CONV_WIDTH = 31
D_FF = 2816
FFN_CONV_WIDTH = 3
EPS = 1e-6
NEG_INF = -1e30

kernel_name = 'hybrid_s5_swa_conformer_decode_step'


def _window_rows():
    return min(WINDOW, PAST_LEN)


def rmsnorm(x, g):
    xf = x.astype(jnp.float32)
    y = xf * lax.rsqrt(jnp.mean(xf * xf, axis=-1, keepdims=True) + EPS) * g.astype(jnp.float32)
    return y.astype(x.dtype)


def causal_dwconv(x_ext, w):
    return lax.conv_general_dilated(x_ext, w.astype(x_ext.dtype)[:, None, :], (1,), 'VALID',
                                    dimension_numbers=('NWC', 'WIO', 'NWC'),
                                    feature_group_count=x_ext.shape[-1])


def t5_bucket(dist):
    n = jnp.maximum(dist, 0)
    max_exact = N_BUCKETS // 2
    nf = jnp.maximum(n, max_exact).astype(jnp.float32)
    large = max_exact + (jnp.log(nf / max_exact) / math.log(MAX_DISTANCE / max_exact)
                         * (N_BUCKETS - max_exact)).astype(jnp.int32)
    large = jnp.minimum(large, N_BUCKETS - 1)
    return jnp.where(n < max_exact, n, large)


def rel_bias_for(dist, rel_bias):
    b = rel_bias.astype(jnp.float32)[t5_bucket(dist)]
    return jnp.moveaxis(b, -1, 0).reshape(N_KV_HEADS, GQ, dist.shape[0], dist.shape[1])


def sink_softmax(s, sink):
    m = jnp.maximum(jnp.max(s, axis=-1, keepdims=True), sink)
    p = jnp.exp(s - m)
    return p / (jnp.sum(p, axis=-1, keepdims=True) + jnp.exp(sink - m))


def _complex_affine_combine(e1, e2):
    a1r, a1i, b1r, b1i = e1
    a2r, a2i, b2r, b2i = e2
    return (a2r * a1r - a2i * a1i, a2r * a1i + a2i * a1r,
            a2r * b1r - a2i * b1i + b2r, a2r * b1i + a2i * b1r + b2i)


def s5_mixer(u, h0_re, h0_im, lam_re, lam_im, log_step, b_re, b_im, c_re, c_im, d_skip, w_glu, b_glu):
    f32 = jnp.float32
    n, t, _ = u.shape
    uf = u.astype(f32)
    ug = uf.reshape(n, t, N_SSM_GROUPS, SSM_GROUP)
    lr, li = lam_re.astype(f32), lam_im.astype(f32)
    dt = jnp.exp(log_step.astype(f32))[:, None]
    decay = jnp.exp(lr * dt)
    a_re = decay * jnp.cos(li * dt)
    a_im = decay * jnp.sin(li * dt)
    den = lr * lr + li * li
    num_re = a_re - 1.0
    coef_re = (num_re * lr + a_im * li) / den
    coef_im = (a_im * lr - num_re * li) / den
    br, bi = b_re.astype(f32), b_im.astype(f32)
    bbar_re = coef_re[..., None] * br - coef_im[..., None] * bi
    bbar_im = coef_re[..., None] * bi + coef_im[..., None] * br
    bu_re = jnp.einsum('ntgc,gpc->ntgp', ug, bbar_re)
    bu_im = jnp.einsum('ntgc,gpc->ntgp', ug, bbar_im)
    h0r, h0i = h0_re.astype(f32), h0_im.astype(f32)
    bu_re = bu_re.at[:, 0].add(a_re * h0r - a_im * h0i)
    bu_im = bu_im.at[:, 0].add(a_re * h0i + a_im * h0r)
    a_full_re = jnp.broadcast_to(a_re, bu_re.shape)
    a_full_im = jnp.broadcast_to(a_im, bu_im.shape)
    _, _, hr, hi = lax.associative_scan(_complex_affine_combine, (a_full_re, a_full_im, bu_re, bu_im), axis=1)
    y = (jnp.einsum('ntgp,gcp->ntgc', hr, c_re.astype(f32))
         - jnp.einsum('ntgp,gcp->ntgc', hi, c_im.astype(f32)))
    y = y.reshape(n, t, D_SSM) + d_skip.astype(f32) * uf
    g = jax.nn.gelu(y)
    out = g * jax.nn.sigmoid(g @ w_glu.astype(f32) + b_glu.astype(f32))
    return out.astype(u.dtype), hr[:, -1], hi[:, -1]


def swa_prompt(q, k, v, rel_bias, sink, w_rows):
    n, t, _ = q.shape
    pad = BLOCK - N_META
    length = t + pad
    nb = length // BLOCK
    padt = lambda z: jnp.pad(z, ((0, 0), (pad, 0), (0, 0)))
    qb = padt(q).reshape(n, nb, BLOCK, N_KV_HEADS, GQ, HEAD_DIM)
    kb = padt(k).reshape(n, nb, BLOCK, N_KV_HEADS, HEAD_DIM)
    vb = padt(v).reshape(n, nb, BLOCK, N_KV_HEADS, HEAD_DIM)
    band = lambda z: jnp.concatenate(
        [jnp.concatenate([jnp.zeros_like(z[:, :1]), z[:, :-1]], axis=1), z], axis=2)
    kband, vband = band(kb), band(vb)
    i = jnp.arange(BLOCK)
    j = jnp.arange(2 * BLOCK)
    dist = BLOCK + i[:, None] - j[None, :]
    kpos = (jnp.arange(nb)[:, None] - 1) * BLOCK + j[None, :] - pad
    mask = (kpos >= 0)[:, None, :] & ((dist >= 0) & (dist < WINDOW))[None]
    s = jnp.einsum('bnqkgd,bnskd->bnkgqs', qb, kband, preferred_element_type=jnp.float32) * (HEAD_DIM ** -0.5)
    s = s + rel_bias_for(dist, rel_bias)
    s = jnp.where(mask[None, :, None, None], s, NEG_INF)
    p = sink_softmax(s, sink.astype(jnp.float32).reshape(N_KV_HEADS, GQ, 1, 1))
    o = jnp.einsum('bnkgqs,bnskd->bnqkgd', p.astype(vband.dtype), vband)
    o = o.reshape(n, length, D_Q)[:, pad:]
    k4 = k.reshape(n, t, N_KV_HEADS, HEAD_DIM)
    v4 = v.reshape(n, t, N_KV_HEADS, HEAD_DIM)
    return o, k4[:, -w_rows:], v4[:, -w_rows:]


def swa_sample(q, k, v, k_buf, v_buf, rel_bias, sink):
    n, t, _ = q.shape
    w = k_buf.shape[1]
    kc = jnp.concatenate([k_buf.astype(k.dtype), k.reshape(n, t, N_KV_HEADS, HEAD_DIM)], axis=1)
    vc = jnp.concatenate([v_buf.astype(v.dtype), v.reshape(n, t, N_KV_HEADS, HEAD_DIM)], axis=1)
    dist = (jnp.arange(t)[:, None] + w) - jnp.arange(w + t)[None, :]
    mask = (dist >= 0) & (dist < WINDOW)
    qh = q.reshape(n, t, N_KV_HEADS, GQ, HEAD_DIM)
    s = jnp.einsum('bqkgd,bskd->bkgqs', qh, kc, preferred_element_type=jnp.float32) * (HEAD_DIM ** -0.5)
    s = s + rel_bias_for(dist, rel_bias)
    s = jnp.where(mask, s, NEG_INF)
    p = sink_softmax(s, sink.astype(jnp.float32).reshape(N_KV_HEADS, GQ, 1, 1))
    o = jnp.einsum('bkgqs,bskd->bqkgd', p.astype(vc.dtype), vc).reshape(n, t, D_Q)
    return o, kc[:, -w:], vc[:, -w:]


def conformer_conv(h, past, w_pw1, w_dw, b_dw, ln_g, ln_b, w_pw2):
    z = h @ w_pw1
    a, g = jnp.split(z, 2, axis=-1)
    gl = a * jax.nn.sigmoid(g)
    ext = jnp.concatenate([past.astype(gl.dtype), gl], axis=1)
    y = causal_dwconv(ext, w_dw).astype(jnp.float32) + b_dw.astype(jnp.float32)
    mu = jnp.mean(y, axis=-1, keepdims=True)
    yc = y - mu
    var = jnp.mean(yc * yc, axis=-1, keepdims=True)
    y = yc * lax.rsqrt(var + EPS) * ln_g.astype(jnp.float32) + ln_b.astype(jnp.float32)
    y = jax.nn.silu(y).astype(h.dtype)
    return y @ w_pw2, ext[:, -(CONV_WIDTH - 1):]


def conv_ffn(h, past, w_up, w_conv, b_conv, w_down):
    z = h @ w_up
    g, u = jnp.split(z, 2, axis=-1)
    ext = jnp.concatenate([past.astype(g.dtype), g], axis=1)
    gc = causal_dwconv(ext, w_conv) + b_conv.astype(g.dtype)
    y = jax.nn.gelu(gc) * u
    return y @ w_down, ext[:, -(FFN_CONV_WIDTH - 1):]


def trunk(x, ssm_re, ssm_im, swa_k, swa_v, conv_st, ffn_st, p):
    n = x.shape[0]
    prompt = swa_k is None
    w_rows = _window_rows()
    new_sr, new_si, new_k, new_v, new_c, new_f = [], [], [], [], [], []
    for layer in range(DEPTH):
        idx = layer // 2
        h = rmsnorm(x, p['g_mix'][layer])
        if layer % 2 == 0:
            z = h @ p['w_in_mix'][idx]
            u = z[..., :D_SSM]
            q = z[..., D_SSM:D_SSM + D_Q]
            k = z[..., D_SSM + D_Q:D_SSM + D_Q + D_KV]
            v = z[..., D_SSM + D_Q + D_KV:]
            if prompt:
                h0r = jnp.zeros((n, N_SSM_GROUPS, SSM_STATE), jnp.float32)
                h0i = jnp.zeros((n, N_SSM_GROUPS, SSM_STATE), jnp.float32)
            else:
                h0r, h0i = ssm_re[idx], ssm_im[idx]
            ya, sr, si = s5_mixer(u, h0r, h0i, p['ssm_lambda_re'][idx], p['ssm_lambda_im'][idx],
                                  p['ssm_log_step'][idx], p['ssm_b_re'][idx], p['ssm_b_im'][idx],
                                  p['ssm_c_re'][idx], p['ssm_c_im'][idx], p['ssm_d'][idx],
                                  p['ssm_w_glu'][idx], p['ssm_b_glu'][idx])
            if prompt:
                yb, nk, nv = swa_prompt(q, k, v, p['rel_bias'], p['attn_sinks'][idx], w_rows)
            else:
                yb, nk, nv = swa_sample(q, k, v, swa_k[idx], swa_v[idx], p['rel_bias'], p['attn_sinks'][idx])
            x = x + jnp.concatenate([ya, yb.astype(ya.dtype)], axis=-1) @ p['w_out_mix'][idx]
            new_sr.append(sr)
            new_si.append(si)
            new_k.append(nk)
            new_v.append(nv)
        else:
            past = jnp.zeros((n, CONV_WIDTH - 1, CONV_DIM), h.dtype) if prompt else conv_st[idx]
            yc, nc = conformer_conv(h, past, p['conv_w_pw1'][idx], p['conv_w_dw'][idx], p['conv_b_dw'][idx],
                                    p['conv_ln_g'][idx], p['conv_ln_b'][idx], p['conv_w_pw2'][idx])
            x = x + yc
            new_c.append(nc)
        h = rmsnorm(x, p['g_ffn'][layer])
        past = jnp.zeros((n, FFN_CONV_WIDTH - 1, D_FF), h.dtype) if prompt else ffn_st[layer]
        yf, nf = conv_ffn(h, past, p['ffn_w_up'][layer], p['ffn_w_conv'][layer], p['ffn_b_conv'][layer],
                          p['ffn_w_down'][layer])
        x = x + yf
        new_f.append(nf)
    states = (jnp.stack(new_sr), jnp.stack(new_si), jnp.stack(new_k), jnp.stack(new_v),
              jnp.stack(new_c), jnp.stack(new_f))
    return rmsnorm(x, p['g_final']), states


def setup_inputs(seed: int = 0) -> dict:
    key = jax.random.key(seed)
    ks = jax.random.split(key, 40)
    f32 = jnp.float32
    nrm = lambda k, shape, scale: jax.random.normal(k, shape, f32) * scale
    w_rows = _window_rows()
    inp = {}
    inp['x_prompt'] = nrm(ks[0], (BATCH, SEQ, D_MODEL), 1.0)
    inp['x_sample'] = nrm(ks[1], (DEC_BATCH, DEC_SEQ, D_MODEL), 1.0)
    inp['state_ssm_re'] = nrm(ks[2], (N_EVEN, DEC_BATCH, N_SSM_GROUPS, SSM_STATE), 0.1)
    inp['state_ssm_im'] = nrm(ks[3], (N_EVEN, DEC_BATCH, N_SSM_GROUPS, SSM_STATE), 0.1)
    inp['cache_swa_k'] = nrm(ks[4], (N_EVEN, DEC_BATCH, w_rows, N_KV_HEADS, HEAD_DIM), 1.0)
    inp['cache_swa_v'] = nrm(ks[5], (N_EVEN, DEC_BATCH, w_rows, N_KV_HEADS, HEAD_DIM), 1.0)
    inp['state_conv'] = nrm(ks[6], (N_ODD, DEC_BATCH, CONV_WIDTH - 1, CONV_DIM), 0.5)
    inp['state_ffn'] = nrm(ks[7], (DEPTH, DEC_BATCH, FFN_CONV_WIDTH - 1, D_FF), 1.0)
    inp['meta_tokens'] = nrm(ks[8], (N_META, D_MODEL), 1.0)
    inp['g_mix'] = 1.0 + nrm(ks[9], (DEPTH, D_MODEL), 0.01)
    inp['g_ffn'] = 1.0 + nrm(ks[10], (DEPTH, D_MODEL), 0.01)
    inp['g_final'] = 1.0 + nrm(ks[11], (D_MODEL,), 0.01)
    inp['w_in_mix'] = nrm(ks[12], (N_EVEN, D_MODEL, D_IN_MIX), D_MODEL ** -0.5)
    inp['ssm_lambda_re'] = -0.5 + nrm(ks[13], (N_EVEN, N_SSM_GROUPS, SSM_STATE), 0.01)
    inp['ssm_lambda_im'] = (jnp.pi * jnp.arange(SSM_STATE, dtype=f32)
                            + nrm(ks[14], (N_EVEN, N_SSM_GROUPS, SSM_STATE), 0.01))
    inp['ssm_log_step'] = jax.random.uniform(ks[15], (N_EVEN, N_SSM_GROUPS), f32,
                                             math.log(1e-3), math.log(1e-1))
    inp['ssm_b_re'] = nrm(ks[16], (N_EVEN, N_SSM_GROUPS, SSM_STATE, SSM_GROUP), (2 * SSM_GROUP) ** -0.5)
    inp['ssm_b_im'] = nrm(ks[17], (N_EVEN, N_SSM_GROUPS, SSM_STATE, SSM_GROUP), (2 * SSM_GROUP) ** -0.5)
    inp['ssm_c_re'] = nrm(ks[18], (N_EVEN, N_SSM_GROUPS, SSM_GROUP, SSM_STATE), 0.5)
    inp['ssm_c_im'] = nrm(ks[19], (N_EVEN, N_SSM_GROUPS, SSM_GROUP, SSM_STATE), 0.5)
    inp['ssm_d'] = nrm(ks[20], (N_EVEN, D_SSM), 1.0)
    inp['ssm_w_glu'] = nrm(ks[21], (N_EVEN, D_SSM, D_SSM), D_SSM ** -0.5)
    inp['ssm_b_glu'] = nrm(ks[22], (N_EVEN, D_SSM), 0.01)
    inp['rel_bias'] = nrm(ks[23], (N_BUCKETS, N_Q_HEADS), 0.5)
    inp['attn_sinks'] = nrm(ks[24], (N_EVEN, N_Q_HEADS), 0.5)
    inp['w_out_mix'] = nrm(ks[25], (N_EVEN, D_OUT_MIX, D_MODEL), D_OUT_MIX ** -0.5)
    inp['conv_w_pw1'] = nrm(ks[26], (N_ODD, D_MODEL, 2 * CONV_DIM), D_MODEL ** -0.5)
    inp['conv_w_dw'] = nrm(ks[27], (N_ODD, CONV_WIDTH, CONV_DIM), CONV_WIDTH ** -0.5)
    inp['conv_b_dw'] = nrm(ks[28], (N_ODD, CONV_DIM), 0.01)
    inp['conv_ln_g'] = 1.0 + nrm(ks[29], (N_ODD, CONV_DIM), 0.01)
    inp['conv_ln_b'] = nrm(ks[30], (N_ODD, CONV_DIM), 0.01)
    inp['conv_w_pw2'] = nrm(ks[31], (N_ODD, CONV_DIM, D_MODEL), CONV_DIM ** -0.5)
    inp['ffn_w_up'] = nrm(ks[32], (DEPTH, D_MODEL, 2 * D_FF), D_MODEL ** -0.5)
    inp['ffn_w_conv'] = nrm(ks[33], (DEPTH, FFN_CONV_WIDTH, D_FF), FFN_CONV_WIDTH ** -0.5)
    inp['ffn_b_conv'] = nrm(ks[34], (DEPTH, D_FF), 0.01)
    inp['ffn_w_down'] = nrm(ks[35], (DEPTH, D_FF, D_MODEL), D_FF ** -0.5)
    return inp


def reference(x_prompt, x_sample, state_ssm_re, state_ssm_im, cache_swa_k, cache_swa_v, state_conv, state_ffn,
              meta_tokens, g_mix, g_ffn, g_final, w_in_mix, ssm_lambda_re, ssm_lambda_im, ssm_log_step,
              ssm_b_re, ssm_b_im, ssm_c_re, ssm_c_im, ssm_d, ssm_w_glu, ssm_b_glu, rel_bias, attn_sinks,
              w_out_mix, conv_w_pw1, conv_w_dw, conv_b_dw, conv_ln_g, conv_ln_b, conv_w_pw2,
              ffn_w_up, ffn_w_conv, ffn_b_conv, ffn_w_down):
    p = {'g_mix': g_mix, 'g_ffn': g_ffn, 'g_final': g_final, 'w_in_mix': w_in_mix,
         'ssm_lambda_re': ssm_lambda_re, 'ssm_lambda_im': ssm_lambda_im, 'ssm_log_step': ssm_log_step,
         'ssm_b_re': ssm_b_re, 'ssm_b_im': ssm_b_im, 'ssm_c_re': ssm_c_re, 'ssm_c_im': ssm_c_im,
         'ssm_d': ssm_d, 'ssm_w_glu': ssm_w_glu, 'ssm_b_glu': ssm_b_glu, 'rel_bias': rel_bias,
         'attn_sinks': attn_sinks, 'w_out_mix': w_out_mix, 'conv_w_pw1': conv_w_pw1, 'conv_w_dw': conv_w_dw,
         'conv_b_dw': conv_b_dw, 'conv_ln_g': conv_ln_g, 'conv_ln_b': conv_ln_b, 'conv_w_pw2': conv_w_pw2,
         'ffn_w_up': ffn_w_up, 'ffn_w_conv': ffn_w_conv, 'ffn_b_conv': ffn_b_conv, 'ffn_w_down': ffn_w_down}
    n_b = x_prompt.shape[0]
    meta = jnp.broadcast_to(meta_tokens.astype(x_prompt.dtype)[None], (n_b, N_META, D_MODEL))
    xp = jnp.concatenate([meta, x_prompt], axis=1)
    yp, (sr_p, si_p, k_p, v_p, c_p, f_p) = trunk(xp, None, None, None, None, None, None, p)
    ys, (sr_s, si_s, k_s, v_s, c_s, f_s) = trunk(x_sample, state_ssm_re, state_ssm_im, cache_swa_k, cache_swa_v,
                                                 state_conv, state_ffn, p)
    return (yp[:, N_META:], ys, sr_p, si_p, k_p, v_p, c_p, f_p, sr_s, si_s, k_s, v_s, c_s, f_s)
```

```python
import functools
import math

import numpy as np
import jax
import jax.numpy as jnp
from jax import lax
from jax.experimental import pallas as pl
from jax.experimental.pallas import tpu as pltpu

F32 = jnp.float32
BF16 = jnp.bfloat16

EPS = 1e-6
NEG_INF = -1e30
WINDOW = 128
HEAD_DIM = 64
N_KV_HEADS = 2
N_BUCKETS = 32
MAX_DISTANCE = 128
SSM_GROUP = 16
SSM_STATE = 64
SSM_BLK_GROUPS = 8
SUBLANES = 8
CONV_HALO = 32
FFN_HALO = 8
CONV_ROWS = 16
PROMPT_TILE_TARGET = 768
SAMPLE_SEQS = 32
VMEM_LIMIT = 56 * 1024 * 1024


def _cparams(sem):
    return pltpu.CompilerParams(dimension_semantics=sem, vmem_limit_bytes=VMEM_LIMIT)


def _const_spec(shape):
    nd = len(shape)
    return pl.BlockSpec(shape, lambda *_: (0,) * nd, pipeline_mode=pl.Buffered(1))


def _rmsnorm(x, g):
    return x * lax.rsqrt(jnp.mean(x * x, axis=-1, keepdims=True) + EPS) * g


def _dot(a, b):
    return jnp.dot(a, b, preferred_element_type=F32)


def _ssm_prep_kernel(lr_ref, li_ref, ls_ref, bre_ref, bim_ref, cre_ref, cim_ref,
                     tre_ref, tim_ref, bbre_ref, bbim_ref, ccre_ref, ccim_ref):
    lr = lr_ref[0]
    li = li_ref[0]
    dt = jnp.exp(ls_ref[0])
    decay = jnp.exp(lr * dt)
    a_re = decay * jnp.cos(li * dt)
    a_im = decay * jnp.sin(li * dt)
    den = lr * lr + li * li
    num_re = a_re - 1.0
    coef_re = (num_re * lr + a_im * li) / den
    coef_im = (a_im * lr - num_re * li) / den

    pw = [(a_re, a_im)]
    for _ in range(SUBLANES - 1):
        pr, pi = pw[-1]
        pw.append((pr * a_re - pi * a_im, pr * a_im + pi * a_re))
    n = lr.shape[-1]
    row = lax.broadcasted_iota(jnp.int32, (SUBLANES, n), 0)
    zero = jnp.zeros((SUBLANES, n), F32)
    for t, k in enumerate((1, 2, 4)):
        tre_ref[0, t] = jnp.where(row >= k, jnp.broadcast_to(pw[k - 1][0], (SUBLANES, n)), zero)
        tim_ref[0, t] = jnp.where(row >= k, jnp.broadcast_to(pw[k - 1][1], (SUBLANES, n)), zero)
    ap_re, ap_im = zero, zero
    for j in range(SUBLANES):
        ap_re = jnp.where(row == j, jnp.broadcast_to(pw[j][0], (SUBLANES, n)), ap_re)
        ap_im = jnp.where(row == j, jnp.broadcast_to(pw[j][1], (SUBLANES, n)), ap_im)
    tre_ref[0, 3] = ap_re
    tim_ref[0, 3] = ap_im

    bre = bre_ref[0]
    bim = bim_ref[0]
    bbar_re = coef_re * bre - coef_im * bim
    bbar_im = coef_re * bim + coef_im * bre
    rows = SSM_BLK_GROUPS * SSM_GROUP
    r_i = lax.broadcasted_iota(jnp.int32, (rows, n), 0)
    c_i = lax.broadcasted_iota(jnp.int32, (rows, n), 1)
    sh = lambda v, d: lax.shift_right_logical(v, jnp.full(v.shape, int(math.log2(d)), jnp.int32))
    diag = sh(r_i, SSM_GROUP) == (sh(c_i, SSM_STATE) & (SSM_BLK_GROUPS - 1))
    zer = jnp.zeros((rows, n), F32)
    tile = lambda v: jnp.concatenate([v] * SSM_BLK_GROUPS, axis=0)
    bbre_ref[0] = jnp.where(diag, tile(bbar_re), zer)
    bbim_ref[0] = jnp.where(diag, tile(bbar_im), zer)
    ccre_ref[0] = jnp.where(diag, tile(cre_ref[0]), zer)
    ccim_ref[0] = jnp.where(diag, tile(-cim_ref[0]), zer)


def _ssm_prep(lam_re, lam_im, log_step, b_re, b_im, c_re, c_im):
    ne, g, p = lam_re.shape
    n = g * p
    assert g % SSM_BLK_GROUPS == 0 and p == SSM_STATE and b_re.shape[-1] == SSM_GROUP
    nblk = g // SSM_BLK_GROUPS
    rows = SSM_BLK_GROUPS * SSM_GROUP
    flat = lambda a: a.reshape(ne, 1, n)
    ls = jnp.broadcast_to(log_step[:, :, None], (ne, g, p))
    bt = lambda a: jnp.transpose(a, (0, 3, 1, 2)).reshape(ne, SSM_GROUP, n)
    ct = lambda a: jnp.transpose(a, (0, 2, 1, 3)).reshape(ne, SSM_GROUP, n)
    vec = pl.BlockSpec((1, 1, n), lambda i: (i, 0, 0))
    mat = pl.BlockSpec((1, SSM_GROUP, n), lambda i: (i, 0, 0))
    tab = pl.BlockSpec((1, 4, SUBLANES, n), lambda i: (i, 0, 0, 0))
    big = pl.BlockSpec((1, rows, n), lambda i: (i, 0, 0))
    tre, tim, bbre, bbim, ccre, ccim = pl.pallas_call(
        _ssm_prep_kernel,
        grid=(ne,),
        in_specs=[vec, vec, vec, mat, mat, mat, mat],
        out_specs=[tab, tab, big, big, big, big],
        out_shape=[jax.ShapeDtypeStruct((ne, 4, SUBLANES, n), F32)] * 2
        + [jax.ShapeDtypeStruct((ne, rows, n), F32)] * 4,
        name="ssm_prep",
    )(flat(lam_re), flat(lam_im), flat(ls), bt(b_re), bt(b_im), ct(c_re), ct(c_im))
    w = SSM_BLK_GROUPS * SSM_STATE
    split = lambda a: jnp.transpose(a.reshape(ne, rows, nblk, w), (0, 2, 1, 3))
    bcat = jnp.concatenate([split(bbre), split(bbim)], axis=-1).astype(BF16)
    ccat = jnp.transpose(jnp.concatenate([split(ccre), split(ccim)], axis=-1), (0, 1, 3, 2)).astype(BF16)
    return tre, tim, bcat, ccat


def _bucket_table():
    i = np.arange(WINDOW)[:, None]
    j = np.arange(2 * WINDOW)[None, :]
    dist = WINDOW + i - j
    nn = np.maximum(dist, 0)
    max_exact = N_BUCKETS // 2
    nf = np.maximum(nn, max_exact).astype(np.float32)
    large = max_exact + (np.log(nf / np.float32(max_exact)) / np.float32(math.log(MAX_DISTANCE / max_exact))
                         * np.float32(N_BUCKETS - max_exact)).astype(np.int32)
    large = np.minimum(large, N_BUCKETS - 1)
    bucket = np.where(nn < max_exact, nn, large)
    valid = (dist >= 0) & (dist < WINDOW)
    return np.where(valid, bucket, -1).astype(np.int32)


def _bias_kernel(rb_ref, bucket_ref, out_ref):
    bucket = bucket_ref[...]
    nh = out_ref.shape[0]
    for h in range(nh):
        acc = jnp.full(bucket.shape, NEG_INF, F32)
        for b in range(N_BUCKETS):
            acc = jnp.where(bucket == b, rb_ref[b, h], acc)
        out_ref[h] = acc


def _bias_table(rel_bias):
    nh = rel_bias.shape[1]
    bucket = jnp.asarray(_bucket_table())
    return pl.pallas_call(
        _bias_kernel,
        in_specs=[pl.BlockSpec(memory_space=pltpu.SMEM),
                  pl.BlockSpec(bucket.shape, lambda: (0, 0))],
        out_specs=pl.BlockSpec((nh,) + bucket.shape, lambda: (0, 0, 0)),
        out_shape=jax.ShapeDtypeStruct((nh,) + bucket.shape, F32),
        name="rel_bias_table",
    )(rel_bias.astype(F32), bucket)


def _s5_scan(hbuf, tre_ref, tim_ref, lane0, w, nblocks, carry_in, carry_out):
    lanes = pl.ds(lane0, w)

    def body(b, carry):
        r0 = pl.multiple_of(b * SUBLANES, SUBLANES)
        xr = hbuf[pl.ds(r0, SUBLANES), 0:w]
        xi = hbuf[pl.ds(r0, SUBLANES), w:2 * w]
        for t, k in enumerate((1, 2, 4)):
            cr = tre_ref[0, t, :, lanes]
            ci = tim_ref[0, t, :, lanes]
            rr = pltpu.roll(xr, k, 0)
            ri = pltpu.roll(xi, k, 0)
            xr, xi = xr + cr * rr - ci * ri, xi + cr * ri + ci * rr
        if carry_in is not None:
            hr, hi = carry_in(b)
        else:
            hr, hi = carry
        pr = tre_ref[0, 3, :, lanes]
        pi = tim_ref[0, 3, :, lanes]
        xr, xi = xr + pr * hr - pi * hi, xi + pr * hi + pi * hr
        hbuf[pl.ds(r0, SUBLANES), 0:w] = xr
        hbuf[pl.ds(r0, SUBLANES), w:2 * w] = xi
        lr = xr[SUBLANES - 1:SUBLANES, :]
        li = xi[SUBLANES - 1:SUBLANES, :]
        if carry_out is not None:
            carry_out(b, lr, li)
        return (jnp.broadcast_to(lr, (SUBLANES, w)), jnp.broadcast_to(li, (SUBLANES, w)))

    return body


def _s5_mix(u, hbuf, bcat_ref, ccat_ref, tre_ref, tim_ref, d, wglu_ref, bglu, run_scan):
    nblk = bcat_ref.shape[1]
    cw = bcat_ref.shape[2]
    w = bcat_ref.shape[3] // 2
    ys = []
    for blk in range(nblk):
        ub = u[:, blk * cw:(blk + 1) * cw].astype(BF16)
        hbuf[...] = _dot(ub, bcat_ref[0, blk])
        run_scan(blk, blk * w, w)
        ys.append(_dot(hbuf[...].astype(BF16), ccat_ref[0, blk]))
    y = jnp.concatenate(ys, axis=1) + d * u
    g = jax.nn.gelu(y)
    return g * jax.nn.sigmoid(_dot(g.astype(BF16), wglu_ref[0]) + bglu)


def _mixer_prompt_kernel(x_ref, g_ref, win_ref, bcat_ref, ccat_ref, tre_ref, tim_ref, d_ref, wglu_ref,
                         bglu_ref, bias_ref, sink_ref, wout_ref,
                         o_ref, sre_ref, sim_ref, ko_ref, vo_ref,
                         hbuf, hst, kbuf, vbuf, mix, *, pad_rows):
    t = pl.program_id(1)
    tm = x_ref.shape[1]
    d_ssm = d_ref.shape[-1]
    d_q = bias_ref.shape[0] * (bias_ref.shape[1] // WINDOW) * HEAD_DIM
    d_kv = N_KV_HEADS * HEAD_DIM
    gq = bias_ref.shape[1] // WINDOW

    @pl.when(t == 0)
    def _():
        hst[...] = jnp.zeros_like(hst)
        kbuf[0:WINDOW, :] = jnp.zeros((WINDOW, d_kv), F32)
        vbuf[0:WINDOW, :] = jnp.zeros((WINDOW, d_kv), F32)

    x = x_ref[0]
    h = _rmsnorm(x, g_ref[0]).astype(BF16)
    z = _dot(h, win_ref[0])
    u = z[:, :d_ssm]
    q = z[:, d_ssm:d_ssm + d_q] * (HEAD_DIM ** -0.5)
    kbuf[WINDOW:, :] = z[:, d_ssm + d_q:d_ssm + d_q + d_kv]
    vbuf[WINDOW:, :] = z[:, d_ssm + d_q + d_kv:]

    n_state = hst.shape[1] // 2

    def run_scan(blk, lane0, w):
        body = _s5_scan(hbuf, tre_ref, tim_ref, lane0, w, tm // SUBLANES, None, None)
        init = (hst[:, lane0:lane0 + w], hst[:, n_state + lane0:n_state + lane0 + w])
        hr, hi = lax.fori_loop(0, tm // SUBLANES, body, init)
        hst[:, lane0:lane0 + w] = hr
        hst[:, n_state + lane0:n_state + lane0 + w] = hi

    ya = _s5_mix(u, hbuf, bcat_ref, ccat_ref, tre_ref, tim_ref, d_ref[0], wglu_ref, bglu_ref[0], run_scan)
    mix[:, 0:d_ssm] = ya.astype(BF16)
    sre_ref[0] = hst[0:1, 0:n_state]
    sim_ref[0] = hst[0:1, n_state:]

    col = lax.broadcasted_iota(jnp.int32, (1, 2 * WINDOW), 1)
    for qb in range(tm // WINDOW):
        r0 = qb * WINDOW
        thresh = pad_rows + WINDOW - (t * tm + r0)
        kmask = col >= thresh
        pieces = []
        for j in range(N_KV_HEADS):
            kb = kbuf[r0:r0 + 2 * WINDOW, j * HEAD_DIM:(j + 1) * HEAD_DIM].astype(BF16)
            vb = vbuf[r0:r0 + 2 * WINDOW, j * HEAD_DIM:(j + 1) * HEAD_DIM].astype(BF16)
            qs = jnp.concatenate(
                [q[r0:r0 + WINDOW, (j * gq + g) * HEAD_DIM:(j * gq + g + 1) * HEAD_DIM] for g in range(gq)],
                axis=0).astype(BF16)
            s = lax.dot_general(qs, kb, (((1,), (1,)), ((), ())), preferred_element_type=F32)
            s = jnp.where(kmask, s + bias_ref[j], NEG_INF)
            sink = sink_ref[0, j]
            m = jnp.maximum(jnp.max(s, axis=-1, keepdims=True), sink)
            p = jnp.exp(s - m)
            l = jnp.sum(p, axis=-1, keepdims=True) + jnp.exp(sink - m)
            o = _dot(p.astype(BF16), vb) / l
            pieces += [o[g * WINDOW:(g + 1) * WINDOW] for g in range(gq)]
        mix[r0:r0 + WINDOW, d_ssm:] = jnp.concatenate(pieces, axis=1).astype(BF16)

    kbuf[0:WINDOW, :] = kbuf[tm:tm + WINDOW, :]
    vbuf[0:WINDOW, :] = vbuf[tm:tm + WINDOW, :]
    ko_ref[0] = kbuf[0:WINDOW, :]
    vo_ref[0] = vbuf[0:WINDOW, :]
    o_ref[0] = x + _dot(mix[...], wout_ref[0])


def _mixer_prompt(x, idx, layer, wts, tm, pad_rows):
    nb, tp, dm = x.shape
    nt = tp // tm
    n_state = wts["tre"].shape[-1]
    d_ssm = wts["ssm_d"].shape[-1]
    d_kv = N_KV_HEADS * HEAD_DIM
    w = wts["bcat"].shape[-1] // 2
    layer_spec = lambda a, i: pl.BlockSpec((1,) + a.shape[1:], lambda b, t: (i,) + (0,) * (a.ndim - 1),
                                           pipeline_mode=pl.Buffered(1))
    ins = [
        (x, pl.BlockSpec((1, tm, dm), lambda b, t: (b, t, 0))),
        (wts["g_mix"], layer_spec(wts["g_mix"], layer)),
        (wts["w_in"], layer_spec(wts["w_in"], idx)),
        (wts["bcat"], layer_spec(wts["bcat"], idx)),
        (wts["ccat"], layer_spec(wts["ccat"], idx)),
        (wts["tre"], layer_spec(wts["tre"], idx)),
        (wts["tim"], layer_spec(wts["tim"], idx)),
        (wts["ssm_d"], layer_spec(wts["ssm_d"], idx)),
        (wts["w_glu"], layer_spec(wts["w_glu"], idx)),
        (wts["b_glu"], layer_spec(wts["b_glu"], idx)),
        (wts["bias_p"], _const_spec(wts["bias_p"].shape)),
        (wts["sink_p"], layer_spec(wts["sink_p"], idx)),
        (wts["w_out"], layer_spec(wts["w_out"], idx)),
    ]
    out_shape = [
        jax.ShapeDtypeStruct((nb, tp, dm), F32),
        jax.ShapeDtypeStruct((nb, 1, n_state), F32),
        jax.ShapeDtypeStruct((nb, 1, n_state), F32),
        jax.ShapeDtypeStruct((nb, WINDOW, d_kv), F32),
        jax.ShapeDtypeStruct((nb, WINDOW, d_kv), F32),
    ]
    out_specs = [
        pl.BlockSpec((1, tm, dm), lambda b, t: (b, t, 0)),
        pl.BlockSpec((1, 1, n_state), lambda b, t: (b, 0, 0)),
        pl.BlockSpec((1, 1, n_state), lambda b, t: (b, 0, 0)),
        pl.BlockSpec((1, WINDOW, d_kv), lambda b, t: (b, 0, 0)),
        pl.BlockSpec((1, WINDOW, d_kv), lambda b, t: (b, 0, 0)),
    ]
    scratch = [
        pltpu.VMEM((tm, 2 * w), F32),
        pltpu.VMEM((SUBLANES, 2 * n_state), F32),
        pltpu.VMEM((tm + WINDOW, d_kv), F32),
        pltpu.VMEM((tm + WINDOW, d_kv), F32),
        pltpu.VMEM((tm, dm), BF16),
    ]
    return pl.pallas_call(
        functools.partial(_mixer_prompt_kernel, pad_rows=pad_rows),
        grid=(nb, nt),
        in_specs=[s for _, s in ins],
        out_specs=out_specs,
        out_shape=out_shape,
        scratch_shapes=scratch,
        compiler_params=_cparams(("arbitrary", "arbitrary")),
        name=f"mixer_prompt_l{layer}",
    )(*[a for a, _ in ins])


def _mixer_sample_kernel(x_ref, h0r_ref, h0i_ref, kc_ref, vc_ref, g_ref, win_ref, bcat_ref, ccat_ref,
                         tre_ref, tim_ref, d_ref, wglu_ref, bglu_ref, bias_ref, sink_ref, wout_ref,
                         o_ref, sre_ref, sim_ref, ko_ref, vo_ref, hbuf):
    rows = x_ref.shape[0]
    ns = kc_ref.shape[0]
    tq = rows // ns
    assert tq == SUBLANES
    w_rows = kc_ref.shape[1]
    d_ssm = d_ref.shape[-1]
    gq = bias_ref.shape[1] // tq
    d_q = N_KV_HEADS * gq * HEAD_DIM
    d_kv = N_KV_HEADS * HEAD_DIM

    x = x_ref[...]
    h = _rmsnorm(x, g_ref[0]).astype(BF16)
    z = _dot(h, win_ref[0])
    u = z[:, :d_ssm]
    q = (z[:, d_ssm:d_ssm + d_q] * (HEAD_DIM ** -0.5)).reshape(ns, tq, d_q)
    kn = z[:, d_ssm + d_q:d_ssm + d_q + d_kv].reshape(ns, tq, d_kv)
    vn = z[:, d_ssm + d_q + d_kv:].reshape(ns, tq, d_kv)

    def run_scan(blk, lane0, w):
        def carry_in(b):
            hr = h0r_ref[pl.ds(b, 1), lane0:lane0 + w]
            hi = h0i_ref[pl.ds(b, 1), lane0:lane0 + w]
            return jnp.broadcast_to(hr, (SUBLANES, w)), jnp.broadcast_to(hi, (SUBLANES, w))

        def carry_out(b, lr, li):
            sre_ref[pl.ds(b, 1), lane0:lane0 + w] = lr
            sim_ref[pl.ds(b, 1), lane0:lane0 + w] = li

        body = _s5_scan(hbuf, tre_ref, tim_ref, lane0, w, ns, carry_in, carry_out)
        zero = jnp.zeros((SUBLANES, w), F32)
        lax.fori_loop(0, ns, body, (zero, zero))

    ya = _s5_mix(u, hbuf, bcat_ref, ccat_ref, tre_ref, tim_ref, d_ref[0], wglu_ref, bglu_ref[0], run_scan)

    kc = jnp.concatenate([kc_ref[...], kn], axis=1)
    vc = jnp.concatenate([vc_ref[...], vn], axis=1)
    ko_ref[...] = kc[:, tq:, :]
    vo_ref[...] = vc[:, tq:, :]
    pieces = []
    for j in range(N_KV_HEADS):
        kb = kc[:, :, j * HEAD_DIM:(j + 1) * HEAD_DIM].astype(BF16)
        vb = vc[:, :, j * HEAD_DIM:(j + 1) * HEAD_DIM].astype(BF16)
        qs = jnp.concatenate(
            [q[:, :, (j * gq + g) * HEAD_DIM:(j * gq + g + 1) * HEAD_DIM] for g in range(gq)],
            axis=1).astype(BF16)
        s = jnp.einsum("nqd,nkd->nqk", qs, kb, preferred_element_type=F32) + bias_ref[j]
        sink = sink_ref[0, j]
        m = jnp.maximum(jnp.max(s, axis=-1, keepdims=True), sink)
        p = jnp.exp(s - m)
        l = jnp.sum(p, axis=-1, keepdims=True) + jnp.exp(sink - m)
        o = jnp.einsum("nqk,nkd->nqd", p.astype(BF16), vb, preferred_element_type=F32) / l
        pieces += [o[:, g * tq:(g + 1) * tq, :] for g in range(gq)]
    yb = jnp.concatenate(pieces, axis=2).reshape(rows, d_q)
    mixed = jnp.concatenate([ya, yb], axis=1).astype(BF16)
    o_ref[...] = x + _dot(mixed, wout_ref[0])


def _mixer_sample(x2, h0r, h0i, kc, vc, idx, layer, wts, ns):
    rows_all, dm = x2.shape
    n_seq, w_rows, d_kv = kc.shape
    tq = rows_all // n_seq
    n_state = h0r.shape[-1]
    w = wts["bcat"].shape[-1] // 2
    layer_spec = lambda a, i: pl.BlockSpec((1,) + a.shape[1:], lambda s: (i,) + (0,) * (a.ndim - 1),
                                           pipeline_mode=pl.Buffered(1))
    row_spec = pl.BlockSpec((ns * tq, dm), lambda s: (s, 0))
    st_spec = pl.BlockSpec((ns, n_state), lambda s: (s, 0))
    kv_spec = pl.BlockSpec((ns, w_rows, d_kv), lambda s: (s, 0, 0))
    ins = [
        (x2, row_spec), (h0r, st_spec), (h0i, st_spec), (kc, kv_spec), (vc, kv_spec),
        (wts["g_mix"], layer_spec(wts["g_mix"], layer)),
        (wts["w_in"], layer_spec(wts["w_in"], idx)),
        (wts["bcat"], layer_spec(wts["bcat"], idx)),
        (wts["ccat"], layer_spec(wts["ccat"], idx)),
        (wts["tre"], layer_spec(wts["tre"], idx)),
        (wts["tim"], layer_spec(wts["tim"], idx)),
        (wts["ssm_d"], layer_spec(wts["ssm_d"], idx)),
        (wts["w_glu"], layer_spec(wts["w_glu"], idx)),
        (wts["b_glu"], layer_spec(wts["b_glu"], idx)),
        (wts["bias_s"], _const_spec(wts["bias_s"].shape)),
        (wts["sink_s"], layer_spec(wts["sink_s"], idx)),
        (wts["w_out"], layer_spec(wts["w_out"], idx)),
    ]
    return pl.pallas_call(
        _mixer_sample_kernel,
        grid=(n_seq // ns,),
        in_specs=[s for _, s in ins],
        out_specs=[row_spec, st_spec, st_spec, kv_spec, kv_spec],
        out_shape=[jax.ShapeDtypeStruct(x2.shape, F32), jax.ShapeDtypeStruct(h0r.shape, F32),
                   jax.ShapeDtypeStruct(h0i.shape, F32), jax.ShapeDtypeStruct(kc.shape, F32),
                   jax.ShapeDtypeStruct(vc.shape, F32)],
        scratch_shapes=[pltpu.VMEM((ns * tq, 2 * w), F32)],
        compiler_params=_cparams(("arbitrary",)),
        name=f"mixer_sample_l{layer}",
    )(*[a for a, _ in ins])


def _conformer_kernel(*refs, carry, pad_rows, width):
    if carry:
        (x_ref, g_ref, w1_ref, wdw_ref, bdw_ref, lng_ref, lnb_ref, w2_ref, o_ref, st_ref, ext, ybuf,
         stage) = refs
        past_ref = None
        t = pl.program_id(1)
        x = x_ref[0]
    else:
        (x_ref, past_ref, g_ref, w1_ref, wdw_ref, bdw_ref, lng_ref, lnb_ref, w2_ref, o_ref, st_ref,
         ext, ybuf) = refs
        x = x_ref[...]
    ns, le, c = ext.shape
    l = le - CONV_HALO
    rows = ns * l
    off = CONV_HALO - (width - 1)

    if carry:
        @pl.when(t == 0)
        def _():
            ext[:, 0:CONV_HALO, :] = jnp.zeros((ns, CONV_HALO, c), F32)
    else:
        ext[:, 0:CONV_HALO, :] = past_ref[...]

    h = _rmsnorm(x, g_ref[0]).astype(BF16)
    z = _dot(h, w1_ref[0])
    gl = z[:, :c] * jax.nn.sigmoid(z[:, c:])
    ext[:, CONV_HALO:, :] = gl.reshape(ns, l, c)
    st_ref[...] = ext[:, l + off:, :].reshape(st_ref.shape)

    if carry:
        def chunk(i, _):
            for half in range(2):
                r0 = pl.multiple_of((2 * i + half) * CONV_ROWS, CONV_ROWS)
                stage[half] = ext[0, pl.ds(r0, CONV_ROWS + CONV_HALO), :]
                acc = jnp.zeros((CONV_ROWS, c), F32)
                for k in range(width):
                    acc = acc + wdw_ref[0, k:k + 1, :] * stage[half, off + k:off + k + CONV_ROWS, :]
                ybuf[pl.ds(r0, CONV_ROWS), :] = acc
            return 0
        assert l % (2 * CONV_ROWS) == 0
        lax.fori_loop(0, l // (2 * CONV_ROWS), chunk, 0)
    else:
        def seq(n, _):
            acc = jnp.zeros((l, c), F32)
            for k in range(width):
                acc = acc + wdw_ref[0, k:k + 1, :] * ext[n, off + k:off + k + l, :]
            ybuf[pl.ds(pl.multiple_of(n * l, l), l), :] = acc
            return 0
        lax.fori_loop(0, ns, seq, 0)

    y = ybuf[...] + bdw_ref[0]
    mu = jnp.mean(y, axis=-1, keepdims=True)
    yc = y - mu
    var = jnp.mean(yc * yc, axis=-1, keepdims=True)
    y = yc * lax.rsqrt(var + EPS) * lng_ref[0] + lnb_ref[0]
    y = jax.nn.silu(y).astype(BF16)
    out = _dot(y, w2_ref[0])
    if carry:
        rid = t * rows + lax.broadcasted_iota(jnp.int32, (rows, 1), 0)
        out = jnp.where(rid >= pad_rows, out, 0.0)
        o_ref[0] = x + out
        ext[:, 0:CONV_HALO, :] = ext[:, l:l + CONV_HALO, :]
    else:
        o_ref[...] = x + out


def _conformer(x, past, idx, layer, wts, tile, pad_rows):
    carry = past is None
    width = wts["w_dw"].shape[1]
    c = wts["w_dw"].shape[2]
    if carry:
        nb, tp, dm = x.shape
        grid = (nb, tp // tile)
        lspec = lambda a, i: pl.BlockSpec((1,) + a.shape[1:], lambda b, t: (i,) + (0,) * (a.ndim - 1),
                                          pipeline_mode=pl.Buffered(1))
        xspec = pl.BlockSpec((1, tile, dm), lambda b, t: (b, t, 0))
        ins = [(x, xspec)]
        st_shape = (nb, width - 1, c)
        st_spec = pl.BlockSpec((1, width - 1, c), lambda b, t: (b, 0, 0))
        ns, l = 1, tile
        sem = ("arbitrary", "arbitrary")
    else:
        rows_all, dm = x.shape
        n_seq = past.shape[0]
        l = rows_all // n_seq
        ns = tile
        grid = (n_seq // ns,)
        lspec = lambda a, i: pl.BlockSpec((1,) + a.shape[1:], lambda s: (i,) + (0,) * (a.ndim - 1),
                                          pipeline_mode=pl.Buffered(1))
        xspec = pl.BlockSpec((ns * l, dm), lambda s: (s, 0))
        ins = [(x, xspec), (past, pl.BlockSpec((ns, CONV_HALO, c), lambda s: (s, 0, 0)))]
        st_shape = (n_seq, width - 1, c)
        st_spec = pl.BlockSpec((ns, width - 1, c), lambda s: (s, 0, 0))
        sem = ("arbitrary",)
    ins += [
        (wts["g_mix"], lspec(wts["g_mix"], layer)),
        (wts["w_pw1"], lspec(wts["w_pw1"], idx)),
        (wts["w_dw"], lspec(wts["w_dw"], idx)),
        (wts["b_dw"], lspec(wts["b_dw"], idx)),
        (wts["ln_g"], lspec(wts["ln_g"], idx)),
        (wts["ln_b"], lspec(wts["ln_b"], idx)),
        (wts["w_pw2"], lspec(wts["w_pw2"], idx)),
    ]
    return pl.pallas_call(
        functools.partial(_conformer_kernel, carry=carry, pad_rows=pad_rows, width=width),
        grid=grid,
        in_specs=[s for _, s in ins],
        out_specs=[xspec, st_spec],
        out_shape=[jax.ShapeDtypeStruct(x.shape, F32), jax.ShapeDtypeStruct(st_shape, F32)],
        scratch_shapes=[pltpu.VMEM((ns, l + CONV_HALO, c), F32), pltpu.VMEM((ns * l, c), F32)]
        + ([pltpu.VMEM((2, CONV_ROWS + CONV_HALO, c), F32)] if carry else []),
        compiler_params=_cparams(sem),
        name=f"conformer_{'prompt' if carry else 'sample'}_l{layer}",
    )(*[a for a, _ in ins])


def _ffn_kernel(*refs, carry, final_norm, n_chunks):
    if carry:
        (x_ref, g_ref, wup_ref, wcv_ref, bcv_ref, wdn_ref, gf_ref, o_ref, st_ref, ext) = refs
        past_ref = None
        t = pl.program_id(1)
        x = x_ref[0]
    else:
        (x_ref, past_ref, g_ref, wup_ref, wcv_ref, bcv_ref, wdn_ref, gf_ref, o_ref, st_ref, ext) = refs
        x = x_ref[...]
    ns, le, dff = ext.shape
    l = le - FFN_HALO
    rows = ns * l
    kw = wcv_ref.shape[1]
    ch = dff // n_chunks

    if carry:
        @pl.when(t == 0)
        def _():
            ext[:, 0:FFN_HALO, :] = jnp.zeros((ns, FFN_HALO, dff), F32)
    else:
        ext[:, 0:FFN_HALO, :] = past_ref[...]

    h = _rmsnorm(x, g_ref[0]).astype(BF16)
    acc = x
    for cix in range(n_chunks):
        c0 = cix * ch
        gate = _dot(h, wup_ref[0, :, c0:c0 + ch])
        up = _dot(h, wup_ref[0, :, dff + c0:dff + c0 + ch])
        ext[:, FFN_HALO:, c0:c0 + ch] = gate.reshape(ns, l, ch)
        gc = bcv_ref[0, :, c0:c0 + ch]
        for k in range(kw):
            o = FFN_HALO - (kw - 1) + k
            gc = gc + wcv_ref[0, k:k + 1, c0:c0 + ch] * ext[:, o:o + l, c0:c0 + ch]
        y = (jax.nn.gelu(gc).reshape(rows, ch) * up).astype(BF16)
        acc = acc + _dot(y, wdn_ref[0, c0:c0 + ch, :])
    st_ref[...] = ext[:, le - (kw - 1):, :].reshape(st_ref.shape)
    if final_norm:
        acc = _rmsnorm(acc, gf_ref[...])
    if carry:
        o_ref[0] = acc
        ext[:, 0:FFN_HALO, :] = ext[:, l:l + FFN_HALO, :]
    else:
        o_ref[...] = acc


def _ffn(x, past, layer, wts, tile, final_norm):
    carry = past is None
    dff = wts["w_dn"].shape[1]
    kw = wts["w_cv"].shape[1]
    if carry:
        nb, tp, dm = x.shape
        grid = (nb, tp // tile)
        lspec = lambda a, i: pl.BlockSpec((1,) + a.shape[1:], lambda b, t: (i,) + (0,) * (a.ndim - 1),
                                          pipeline_mode=pl.Buffered(1))
        xspec = pl.BlockSpec((1, tile, dm), lambda b, t: (b, t, 0))
        ins = [(x, xspec)]
        st_shape = (nb, kw - 1, dff)
        st_spec = pl.BlockSpec((1, kw - 1, dff), lambda b, t: (b, 0, 0))
        ns, l = 1, tile
        sem = ("arbitrary", "arbitrary")
    else:
        rows_all, dm = x.shape
        n_seq = past.shape[0]
        l = rows_all // n_seq
        ns = tile
        grid = (n_seq // ns,)
        lspec = lambda a, i: pl.BlockSpec((1,) + a.shape[1:], lambda s: (i,) + (0,) * (a.ndim - 1),
                                          pipeline_mode=pl.Buffered(1))
        xspec = pl.BlockSpec((ns * l, dm), lambda s: (s, 0))
        ins = [(x, xspec), (past, pl.BlockSpec((ns, FFN_HALO, dff), lambda s: (s, 0, 0)))]
        st_shape = (n_seq, kw - 1, dff)
        st_spec = pl.BlockSpec((ns, kw - 1, dff), lambda s: (s, 0, 0))
        sem = ("arbitrary",)
    ins += [
        (wts["g_ffn"], lspec(wts["g_ffn"], layer)),
        (wts["w_up"], lspec(wts["w_up"], layer)),
        (wts["w_cv"], lspec(wts["w_cv"], layer)),
        (wts["b_cv"], lspec(wts["b_cv"], layer)),
        (wts["w_dn"], lspec(wts["w_dn"], layer)),
        (wts["g_final"], _const_spec(wts["g_final"].shape)),
    ]
    n_chunks = 2 if dff % 256 == 0 else 1
    return pl.pallas_call(
        functools.partial(_ffn_kernel, carry=carry, final_norm=final_norm, n_chunks=n_chunks),
        grid=grid,
        in_specs=[s for _, s in ins],
        out_specs=[xspec, st_spec],
        out_shape=[jax.ShapeDtypeStruct(x.shape, F32), jax.ShapeDtypeStruct(st_shape, F32)],
        scratch_shapes=[pltpu.VMEM((ns, l + FFN_HALO, dff), F32)],
        compiler_params=_cparams(sem),
        name=f"ffn_{'prompt' if carry else 'sample'}_l{layer}",
    )(*[a for a, _ in ins])


def _pick_tile(total, target):
    best = WINDOW
    for m in range(1, total // WINDOW + 1):
        if total % (m * WINDOW) == 0 and m * WINDOW <= target:
            best = m * WINDOW
    return best


def kernel(x_prompt, x_sample, state_ssm_re, state_ssm_im, cache_swa_k, cache_swa_v, state_conv, state_ffn,
           meta_tokens, g_mix, g_ffn, g_final, w_in_mix, ssm_lambda_re, ssm_lambda_im, ssm_log_step,
           ssm_b_re, ssm_b_im, ssm_c_re, ssm_c_im, ssm_d, ssm_w_glu, ssm_b_glu, rel_bias, attn_sinks,
           w_out_mix, conv_w_pw1, conv_w_dw, conv_b_dw, conv_ln_g, conv_ln_b, conv_w_pw2,
           ffn_w_up, ffn_w_conv, ffn_b_conv, ffn_w_down):
    nb, seq, dm = x_prompt.shape
    n_seq, tq, _ = x_sample.shape
    depth = g_mix.shape[0]
    n_meta = meta_tokens.shape[0]
    n_even, g_ssm, p_ssm = ssm_lambda_re.shape
    n_heads = rel_bias.shape[1]
    gq = n_heads // N_KV_HEADS
    w_rows = cache_swa_k.shape[2]
    d_kv = N_KV_HEADS * HEAD_DIM
    conv_w = conv_w_dw.shape[1]
    ffn_w = ffn_w_conv.shape[1]
    assert tq == SUBLANES and w_rows == WINDOW and seq % WINDOW == 0 and n_meta <= WINDOW
    assert conv_w - 1 <= CONV_HALO and ffn_w - 1 <= FFN_HALO

    tre, tim, bcat, ccat = _ssm_prep(ssm_lambda_re, ssm_lambda_im, ssm_log_step,
                                     ssm_b_re, ssm_b_im, ssm_c_re, ssm_c_im)
    bias = _bias_table(rel_bias)
    row3 = lambda a: a.reshape(a.shape[0], 1, a.shape[-1]).astype(F32)
    sinks = attn_sinks.astype(F32).reshape(n_even, N_KV_HEADS, gq, 1)
    wts = {
        "g_mix": row3(g_mix), "g_ffn": row3(g_ffn), "g_final": g_final.reshape(1, dm).astype(F32),
        "w_in": w_in_mix.astype(BF16), "bcat": bcat, "ccat": ccat, "tre": tre, "tim": tim,
        "ssm_d": row3(ssm_d), "w_glu": ssm_w_glu.astype(BF16), "b_glu": row3(ssm_b_glu),
        "bias_p": bias.reshape(N_KV_HEADS, gq * WINDOW, 2 * WINDOW),
        "bias_s": bias[:, :tq, :w_rows + tq].reshape(N_KV_HEADS, gq * tq, w_rows + tq),
        "sink_p": jnp.broadcast_to(sinks[:, :, :, None, :], (n_even, N_KV_HEADS, gq, WINDOW, 1)
                                   ).reshape(n_even, N_KV_HEADS, gq * WINDOW, 1),
        "sink_s": jnp.broadcast_to(sinks[:, :, :, None, :], (n_even, N_KV_HEADS, gq, tq, 1)
                                   ).reshape(n_even, N_KV_HEADS, gq * tq, 1),
        "w_out": w_out_mix.astype(BF16),
        "w_pw1": conv_w_pw1.astype(BF16), "w_dw": conv_w_dw.astype(F32), "b_dw": row3(conv_b_dw),
        "ln_g": row3(conv_ln_g), "ln_b": row3(conv_ln_b), "w_pw2": conv_w_pw2.astype(BF16),
        "w_up": ffn_w_up.astype(BF16), "w_cv": ffn_w_conv.astype(F32), "b_cv": row3(ffn_b_conv),
        "w_dn": ffn_w_down.astype(BF16),
    }

    pad_rows = WINDOW - n_meta
    tp = pad_rows + n_meta + seq
    tm = _pick_tile(tp, PROMPT_TILE_TARGET)
    xp = jnp.concatenate([jnp.zeros((nb, pad_rows, dm), F32),
                          jnp.broadcast_to(meta_tokens.astype(F32)[None], (nb, n_meta, dm)),
                          x_prompt.astype(F32)], axis=1)
    xs = x_sample.astype(F32).reshape(n_seq * tq, dm)
    ns = min(SAMPLE_SEQS, n_seq)
    assert n_seq % ns == 0

    sr_p, si_p, k_p, v_p, c_p, f_p = [], [], [], [], [], []
    sr_s, si_s, k_s, v_s, c_s, f_s = [], [], [], [], [], []
    for layer in range(depth):
        idx = layer // 2
        if layer % 2 == 0:
            xp, sre, sim, ko, vo = _mixer_prompt(xp, idx, layer, wts, tm, pad_rows)
            sr_p.append(sre.reshape(nb, g_ssm, p_ssm))
            si_p.append(sim.reshape(nb, g_ssm, p_ssm))
            k_p.append(ko.reshape(nb, WINDOW, N_KV_HEADS, HEAD_DIM))
            v_p.append(vo.reshape(nb, WINDOW, N_KV_HEADS, HEAD_DIM))
            xs, sre, sim, ko, vo = _mixer_sample(
                xs, state_ssm_re[idx].astype(F32).reshape(n_seq, g_ssm * p_ssm),
                state_ssm_im[idx].astype(F32).reshape(n_seq, g_ssm * p_ssm),
                cache_swa_k[idx].astype(F32).reshape(n_seq, w_rows, d_kv),
                cache_swa_v[idx].astype(F32).reshape(n_seq, w_rows, d_kv), idx, layer, wts, ns)
            sr_s.append(sre.reshape(n_seq, g_ssm, p_ssm))
            si_s.append(sim.reshape(n_seq, g_ssm, p_ssm))
            k_s.append(ko.reshape(n_seq, w_rows, N_KV_HEADS, HEAD_DIM))
            v_s.append(vo.reshape(n_seq, w_rows, N_KV_HEADS, HEAD_DIM))
        else:
            xp, st = _conformer(xp, None, idx, layer, wts, tm, pad_rows)
            c_p.append(st)
            past = jnp.pad(state_conv[idx].astype(F32), ((0, 0), (CONV_HALO - (conv_w - 1), 0), (0, 0)))
            xs, st = _conformer(xs, past, idx, layer, wts, ns, 0)
            c_s.append(st)
        last = layer == depth - 1
        xp, st = _ffn(xp, None, layer, wts, tm, last)
        f_p.append(st)
        past = jnp.pad(state_ffn[layer].astype(F32), ((0, 0), (FFN_HALO - (ffn_w - 1), 0), (0, 0)))
        xs, st = _ffn(xs, past, layer, wts, ns, last)
        f_s.append(st)

    yp = xp[:, pad_rows + n_meta:]
    ys = xs.reshape(n_seq, tq, dm)
    st = jnp.stack
    return (yp, ys, st(sr_p), st(si_p), st(k_p), st(v_p), st(c_p), st(f_p),
            st(sr_s), st(si_s), st(k_s), st(v_s), st(c_s), st(f_s))
```

```python
import functools
import math

import numpy as np
import jax
import jax.numpy as jnp
from jax import lax
from jax.experimental import pallas as pl
from jax.experimental.pallas import tpu as pltpu

F32 = jnp.float32
BF16 = jnp.bfloat16

EPS = 1e-6
NEG_INF = -1e30
WINDOW = 128
HEAD_DIM = 64
N_KV_HEADS = 2
N_BUCKETS = 32
MAX_DISTANCE = 128
SSM_GROUP = 16
SSM_STATE = 64
SSM_BLK_GROUPS = 8
SUBLANES = 8
CONV_HALO = 32
FFN_HALO = 8
CONV_ROWS = 32
PROMPT_TILE_TARGET = 768
SAMPLE_SEQS = 32
VMEM_LIMIT = 56 * 1024 * 1024


def _cparams(sem):
    return pltpu.CompilerParams(dimension_semantics=sem, vmem_limit_bytes=VMEM_LIMIT)


def _const_spec(shape):
    nd = len(shape)
    return pl.BlockSpec(shape, lambda *_: (0,) * nd, pipeline_mode=pl.Buffered(1))


def _rmsnorm(x, g):
    return x * lax.rsqrt(jnp.mean(x * x, axis=-1, keepdims=True) + EPS) * g


def _dot(a, b):
    return jnp.dot(a, b, preferred_element_type=F32)


def _ssm_prep_kernel(lr_ref, li_ref, ls_ref, bre_ref, bim_ref, cre_ref, cim_ref,
                     tre_ref, tim_ref, bbre_ref, bbim_ref, ccre_ref, ccim_ref):
    lr = lr_ref[0]
    li = li_ref[0]
    dt = jnp.exp(ls_ref[0])
    decay = jnp.exp(lr * dt)
    a_re = decay * jnp.cos(li * dt)
    a_im = decay * jnp.sin(li * dt)
    den = lr * lr + li * li
    num_re = a_re - 1.0
    coef_re = (num_re * lr + a_im * li) / den
    coef_im = (a_im * lr - num_re * li) / den

    pw = [(a_re, a_im)]
    for _ in range(SUBLANES - 1):
        pr, pi = pw[-1]
        pw.append((pr * a_re - pi * a_im, pr * a_im + pi * a_re))
    n = lr.shape[-1]
    row = lax.broadcasted_iota(jnp.int32, (SUBLANES, n), 0)
    zero = jnp.zeros((SUBLANES, n), F32)
    for t, k in enumerate((1, 2, 4)):
        tre_ref[0, t] = jnp.where(row >= k, jnp.broadcast_to(pw[k - 1][0], (SUBLANES, n)), zero)
        tim_ref[0, t] = jnp.where(row >= k, jnp.broadcast_to(pw[k - 1][1], (SUBLANES, n)), zero)
    ap_re, ap_im = zero, zero
    for j in range(SUBLANES):
        ap_re = jnp.where(row == j, jnp.broadcast_to(pw[j][0], (SUBLANES, n)), ap_re)
        ap_im = jnp.where(row == j, jnp.broadcast_to(pw[j][1], (SUBLANES, n)), ap_im)
    tre_ref[0, 3] = ap_re
    tim_ref[0, 3] = ap_im

    bre = bre_ref[0]
    bim = bim_ref[0]
    bbar_re = coef_re * bre - coef_im * bim
    bbar_im = coef_re * bim + coef_im * bre
    rows = SSM_BLK_GROUPS * SSM_GROUP
    r_i = lax.broadcasted_iota(jnp.int32, (rows, n), 0)
    c_i = lax.broadcasted_iota(jnp.int32, (rows, n), 1)
    sh = lambda v, d: lax.shift_right_logical(v, jnp.full(v.shape, int(math.log2(d)), jnp.int32))
    diag = sh(r_i, SSM_GROUP) == (sh(c_i, SSM_STATE) & (SSM_BLK_GROUPS - 1))
    zer = jnp.zeros((rows, n), F32)
    tile = lambda v: jnp.concatenate([v] * SSM_BLK_GROUPS, axis=0)
    bbre_ref[0] = jnp.where(diag, tile(bbar_re), zer)
    bbim_ref[0] = jnp.where(diag, tile(bbar_im), zer)
    ccre_ref[0] = jnp.where(diag, tile(cre_ref[0]), zer)
    ccim_ref[0] = jnp.where(diag, tile(-cim_ref[0]), zer)


def _ssm_prep(lam_re, lam_im, log_step, b_re, b_im, c_re, c_im):
    ne, g, p = lam_re.shape
    n = g * p
    assert g % SSM_BLK_GROUPS == 0 and p == SSM_STATE and b_re.shape[-1] == SSM_GROUP
    nblk = g // SSM_BLK_GROUPS
    rows = SSM_BLK_GROUPS * SSM_GROUP
    flat = lambda a: a.reshape(ne, 1, n)
    ls = jnp.broadcast_to(log_step[:, :, None], (ne, g, p))
    bt = lambda a: jnp.transpose(a, (0, 3, 1, 2)).reshape(ne, SSM_GROUP, n)
    ct = lambda a: jnp.transpose(a, (0, 2, 1, 3)).reshape(ne, SSM_GROUP, n)
    vec = pl.BlockSpec((1, 1, n), lambda i: (i, 0, 0))
    mat = pl.BlockSpec((1, SSM_GROUP, n), lambda i: (i, 0, 0))
    tab = pl.BlockSpec((1, 4, SUBLANES, n), lambda i: (i, 0, 0, 0))
    big = pl.BlockSpec((1, rows, n), lambda i: (i, 0, 0))
    tre, tim, bbre, bbim, ccre, ccim = pl.pallas_call(
        _ssm_prep_kernel,
        grid=(ne,),
        in_specs=[vec, vec, vec, mat, mat, mat, mat],
        out_specs=[tab, tab, big, big, big, big],
        out_shape=[jax.ShapeDtypeStruct((ne, 4, SUBLANES, n), F32)] * 2
        + [jax.ShapeDtypeStruct((ne, rows, n), F32)] * 4,
        name="ssm_prep",
    )(flat(lam_re), flat(lam_im), flat(ls), bt(b_re), bt(b_im), ct(c_re), ct(c_im))
    w = SSM_BLK_GROUPS * SSM_STATE
    split = lambda a: jnp.transpose(a.reshape(ne, rows, nblk, w), (0, 2, 1, 3))
    bcat = jnp.concatenate([split(bbre), split(bbim)], axis=-1).astype(BF16)
    ccat = jnp.transpose(jnp.concatenate([split(ccre), split(ccim)], axis=-1), (0, 1, 3, 2)).astype(BF16)
    return tre, tim, bcat, ccat


def _bucket_table():
    i = np.arange(WINDOW)[:, None]
    j = np.arange(2 * WINDOW)[None, :]
    dist = WINDOW + i - j
    nn = np.maximum(dist, 0)
    max_exact = N_BUCKETS // 2
    nf = np.maximum(nn, max_exact).astype(np.float32)
    large = max_exact + (np.log(nf / np.float32(max_exact)) / np.float32(math.log(MAX_DISTANCE / max_exact))
                         * np.float32(N_BUCKETS - max_exact)).astype(np.int32)
    large = np.minimum(large, N_BUCKETS - 1)
    bucket = np.where(nn < max_exact, nn, large)
    valid = (dist >= 0) & (dist < WINDOW)
    return np.where(valid, bucket, -1).astype(np.int32)


def _bias_kernel(rb_ref, bucket_ref, out_ref):
    bucket = bucket_ref[...]
    nh = out_ref.shape[0]
    for h in range(nh):
        acc = jnp.full(bucket.shape, NEG_INF, F32)
        for b in range(N_BUCKETS):
            acc = jnp.where(bucket == b, rb_ref[b, h], acc)
        out_ref[h] = acc


def _bias_table(rel_bias):
    nh = rel_bias.shape[1]
    bucket = jnp.asarray(_bucket_table())
    return pl.pallas_call(
        _bias_kernel,
        in_specs=[pl.BlockSpec(memory_space=pltpu.SMEM),
                  pl.BlockSpec(bucket.shape, lambda: (0, 0))],
        out_specs=pl.BlockSpec((nh,) + bucket.shape, lambda: (0, 0, 0)),
        out_shape=jax.ShapeDtypeStruct((nh,) + bucket.shape, F32),
        name="rel_bias_table",
    )(rel_bias.astype(F32), bucket)


def _s5_scan(hbuf, tre_ref, tim_ref, lane0, w, nblocks, carry_in, carry_out):
    lanes = pl.ds(lane0, w)

    def body(b, carry):
        r0 = pl.multiple_of(b * SUBLANES, SUBLANES)
        xr = hbuf[pl.ds(r0, SUBLANES), 0:w]
        xi = hbuf[pl.ds(r0, SUBLANES), w:2 * w]
        for t, k in enumerate((1, 2, 4)):
            cr = tre_ref[0, t, :, lanes]
            ci = tim_ref[0, t, :, lanes]
            rr = pltpu.roll(xr, k, 0)
            ri = pltpu.roll(xi, k, 0)
            xr, xi = xr + cr * rr - ci * ri, xi + cr * ri + ci * rr
        if carry_in is not None:
            hr, hi = carry_in(b)
        else:
            hr, hi = carry
        pr = tre_ref[0, 3, :, lanes]
        pi = tim_ref[0, 3, :, lanes]
        xr, xi = xr + pr * hr - pi * hi, xi + pr * hi + pi * hr
        hbuf[pl.ds(r0, SUBLANES), 0:w] = xr
        hbuf[pl.ds(r0, SUBLANES), w:2 * w] = xi
        lr = xr[SUBLANES - 1:SUBLANES, :]
        li = xi[SUBLANES - 1:SUBLANES, :]
        if carry_out is not None:
            carry_out(b, lr, li)
        return (jnp.broadcast_to(lr, (SUBLANES, w)), jnp.broadcast_to(li, (SUBLANES, w)))

    return body


def _s5_mix(u, hbuf, bcat_ref, ccat_ref, tre_ref, tim_ref, d, wglu_ref, bglu, run_scan):
    nblk = bcat_ref.shape[1]
    cw = bcat_ref.shape[2]
    w = bcat_ref.shape[3] // 2
    ys = []
    for blk in range(nblk):
        ub = u[:, blk * cw:(blk + 1) * cw].astype(BF16)
        hbuf[...] = _dot(ub, bcat_ref[0, blk])
        run_scan(blk, blk * w, w)
        ys.append(_dot(hbuf[...].astype(BF16), ccat_ref[0, blk]))
    y = jnp.concatenate(ys, axis=1) + d * u
    g = jax.nn.gelu(y)
    return g * jax.nn.sigmoid(_dot(g.astype(BF16), wglu_ref[0]) + bglu)


def _mixer_prompt_kernel(x_ref, g_ref, win_ref, bcat_ref, ccat_ref, tre_ref, tim_ref, d_ref, wglu_ref,
                         bglu_ref, bias_ref, sink_ref, wout_ref,
                         o_ref, sre_ref, sim_ref, ko_ref, vo_ref,
                         hbuf, hst, kbuf, vbuf, mix, *, pad_rows):
    t = pl.program_id(1)
    tm = x_ref.shape[1]
    d_ssm = d_ref.shape[-1]
    d_q = bias_ref.shape[0] * (bias_ref.shape[1] // WINDOW) * HEAD_DIM
    d_kv = N_KV_HEADS * HEAD_DIM
    gq = bias_ref.shape[1] // WINDOW

    @pl.when(t == 0)
    def _():
        hst[...] = jnp.zeros_like(hst)
        kbuf[0:WINDOW, :] = jnp.zeros((WINDOW, d_kv), F32)
        vbuf[0:WINDOW, :] = jnp.zeros((WINDOW, d_kv), F32)

    x = x_ref[0]
    h = _rmsnorm(x, g_ref[0]).astype(BF16)
    z = _dot(h, win_ref[0])
    u = z[:, :d_ssm]
    q = z[:, d_ssm:d_ssm + d_q] * (HEAD_DIM ** -0.5)
    kbuf[WINDOW:, :] = z[:, d_ssm + d_q:d_ssm + d_q + d_kv]
    vbuf[WINDOW:, :] = z[:, d_ssm + d_q + d_kv:]

    n_state = hst.shape[1] // 2

    def run_scan(blk, lane0, w):
        body = _s5_scan(hbuf, tre_ref, tim_ref, lane0, w, tm // SUBLANES, None, None)
        init = (hst[:, lane0:lane0 + w], hst[:, n_state + lane0:n_state + lane0 + w])
        hr, hi = lax.fori_loop(0, tm // SUBLANES, body, init)
        hst[:, lane0:lane0 + w] = hr
        hst[:, n_state + lane0:n_state + lane0 + w] = hi

    ya = _s5_mix(u, hbuf, bcat_ref, ccat_ref, tre_ref, tim_ref, d_ref[0], wglu_ref, bglu_ref[0], run_scan)
    mix[:, 0:d_ssm] = ya.astype(BF16)
    sre_ref[0] = hst[0:1, 0:n_state]
    sim_ref[0] = hst[0:1, n_state:]

    col = lax.broadcasted_iota(jnp.int32, (1, 2 * WINDOW), 1)
    for qb in range(tm // WINDOW):
        r0 = qb * WINDOW
        thresh = pad_rows + WINDOW - (t * tm + r0)
        kmask = col >= thresh
        pieces = []
        for j in range(N_KV_HEADS):
            kb = kbuf[r0:r0 + 2 * WINDOW, j * HEAD_DIM:(j + 1) * HEAD_DIM].astype(BF16)
            vb = vbuf[r0:r0 + 2 * WINDOW, j * HEAD_DIM:(j + 1) * HEAD_DIM].astype(BF16)
            qs = jnp.concatenate(
                [q[r0:r0 + WINDOW, (j * gq + g) * HEAD_DIM:(j * gq + g + 1) * HEAD_DIM] for g in range(gq)],
                axis=0).astype(BF16)
            s = lax.dot_general(qs, kb, (((1,), (1,)), ((), ())), preferred_element_type=F32)
            s = jnp.where(kmask, s + bias_ref[j], NEG_INF)
            sink = sink_ref[0, j]
            m = jnp.maximum(jnp.max(s, axis=-1, keepdims=True), sink)
            p = jnp.exp(s - m)
            l = jnp.sum(p, axis=-1, keepdims=True) + jnp.exp(sink - m)
            o = _dot(p.astype(BF16), vb) / l
            pieces += [o[g * WINDOW:(g + 1) * WINDOW] for g in range(gq)]
        mix[r0:r0 + WINDOW, d_ssm:] = jnp.concatenate(pieces, axis=1).astype(BF16)

    kbuf[0:WINDOW, :] = kbuf[tm:tm + WINDOW, :]
    vbuf[0:WINDOW, :] = vbuf[tm:tm + WINDOW, :]
    ko_ref[0] = kbuf[0:WINDOW, :]
    vo_ref[0] = vbuf[0:WINDOW, :]
    o_ref[0] = x + _dot(mix[...], wout_ref[0])


def _mixer_prompt(x, idx, layer, wts, tm, pad_rows):
    nb, tp, dm = x.shape
    nt = tp // tm
    n_state = wts["tre"].shape[-1]
    d_ssm = wts["ssm_d"].shape[-1]
    d_kv = N_KV_HEADS * HEAD_DIM
    w = wts["bcat"].shape[-1] // 2
    layer_spec = lambda a, i: pl.BlockSpec((1,) + a.shape[1:], lambda b, t: (i,) + (0,) * (a.ndim - 1),
                                           pipeline_mode=pl.Buffered(1))
    ins = [
        (x, pl.BlockSpec((1, tm, dm), lambda b, t: (b, t, 0))),
        (wts["g_mix"], layer_spec(wts["g_mix"], layer)),
        (wts["w_in"], layer_spec(wts["w_in"], idx)),
        (wts["bcat"], layer_spec(wts["bcat"], idx)),
        (wts["ccat"], layer_spec(wts["ccat"], idx)),
        (wts["tre"], layer_spec(wts["tre"], idx)),
        (wts["tim"], layer_spec(wts["tim"], idx)),
        (wts["ssm_d"], layer_spec(wts["ssm_d"], idx)),
        (wts["w_glu"], layer_spec(wts["w_glu"], idx)),
        (wts["b_glu"], layer_spec(wts["b_glu"], idx)),
        (wts["bias_p"], _const_spec(wts["bias_p"].shape)),
        (wts["sink_p"], layer_spec(wts["sink_p"], idx)),
        (wts["w_out"], layer_spec(wts["w_out"], idx)),
    ]
    out_shape = [
        jax.ShapeDtypeStruct((nb, tp, dm), F32),
        jax.ShapeDtypeStruct((nb, 1, n_state), F32),
        jax.ShapeDtypeStruct((nb, 1, n_state), F32),
        jax.ShapeDtypeStruct((nb, WINDOW, d_kv), F32),
        jax.ShapeDtypeStruct((nb, WINDOW, d_kv), F32),
    ]
    out_specs = [
        pl.BlockSpec((1, tm, dm), lambda b, t: (b, t, 0)),
        pl.BlockSpec((1, 1, n_state), lambda b, t: (b, 0, 0)),
        pl.BlockSpec((1, 1, n_state), lambda b, t: (b, 0, 0)),
        pl.BlockSpec((1, WINDOW, d_kv), lambda b, t: (b, 0, 0)),
        pl.BlockSpec((1, WINDOW, d_kv), lambda b, t: (b, 0, 0)),
    ]
    scratch = [
        pltpu.VMEM((tm, 2 * w), F32),
        pltpu.VMEM((SUBLANES, 2 * n_state), F32),
        pltpu.VMEM((tm + WINDOW, d_kv), F32),
        pltpu.VMEM((tm + WINDOW, d_kv), F32),
        pltpu.VMEM((tm, dm), BF16),
    ]
    return pl.pallas_call(
        functools.partial(_mixer_prompt_kernel, pad_rows=pad_rows),
        grid=(nb, nt),
        in_specs=[s for _, s in ins],
        out_specs=out_specs,
        out_shape=out_shape,
        scratch_shapes=scratch,
        compiler_params=_cparams(("arbitrary", "arbitrary")),
        name=f"mixer_prompt_l{layer}",
    )(*[a for a, _ in ins])


def _mixer_sample_kernel(x_ref, h0r_ref, h0i_ref, kc_ref, vc_ref, g_ref, win_ref, bcat_ref, ccat_ref,
                         tre_ref, tim_ref, d_ref, wglu_ref, bglu_ref, bias_ref, sink_ref, wout_ref,
                         o_ref, sre_ref, sim_ref, ko_ref, vo_ref, hbuf):
    rows = x_ref.shape[0]
    ns = kc_ref.shape[0]
    tq = rows // ns
    assert tq == SUBLANES
    w_rows = kc_ref.shape[1]
    d_ssm = d_ref.shape[-1]
    gq = bias_ref.shape[1] // tq
    d_q = N_KV_HEADS * gq * HEAD_DIM
    d_kv = N_KV_HEADS * HEAD_DIM

    x = x_ref[...]
    h = _rmsnorm(x, g_ref[0]).astype(BF16)
    z = _dot(h, win_ref[0])
    u = z[:, :d_ssm]
    q = (z[:, d_ssm:d_ssm + d_q] * (HEAD_DIM ** -0.5)).reshape(ns, tq, d_q)
    kn = z[:, d_ssm + d_q:d_ssm + d_q + d_kv].reshape(ns, tq, d_kv)
    vn = z[:, d_ssm + d_q + d_kv:].reshape(ns, tq, d_kv)

    def run_scan(blk, lane0, w):
        def carry_in(b):
            hr = h0r_ref[pl.ds(b, 1), lane0:lane0 + w]
            hi = h0i_ref[pl.ds(b, 1), lane0:lane0 + w]
            return jnp.broadcast_to(hr, (SUBLANES, w)), jnp.broadcast_to(hi, (SUBLANES, w))

        def carry_out(b, lr, li):
            sre_ref[pl.ds(b, 1), lane0:lane0 + w] = lr
            sim_ref[pl.ds(b, 1), lane0:lane0 + w] = li

        body = _s5_scan(hbuf, tre_ref, tim_ref, lane0, w, ns, carry_in, carry_out)
        zero = jnp.zeros((SUBLANES, w), F32)
        lax.fori_loop(0, ns, body, (zero, zero))

    ya = _s5_mix(u, hbuf, bcat_ref, ccat_ref, tre_ref, tim_ref, d_ref[0], wglu_ref, bglu_ref[0], run_scan)

    kc = jnp.concatenate([kc_ref[...], kn], axis=1)
    vc = jnp.concatenate([vc_ref[...], vn], axis=1)
    ko_ref[...] = kc[:, tq:, :]
    vo_ref[...] = vc[:, tq:, :]
    pieces = []
    for j in range(N_KV_HEADS):
        kb = kc[:, :, j * HEAD_DIM:(j + 1) * HEAD_DIM].astype(BF16)
        vb = vc[:, :, j * HEAD_DIM:(j + 1) * HEAD_DIM].astype(BF16)
        qs = jnp.concatenate(
            [q[:, :, (j * gq + g) * HEAD_DIM:(j * gq + g + 1) * HEAD_DIM] for g in range(gq)],
            axis=1).astype(BF16)
        s = jnp.einsum("nqd,nkd->nqk", qs, kb, preferred_element_type=F32) + bias_ref[j]
        sink = sink_ref[0, j]
        m = jnp.maximum(jnp.max(s, axis=-1, keepdims=True), sink)
        p = jnp.exp(s - m)
        l = jnp.sum(p, axis=-1, keepdims=True) + jnp.exp(sink - m)
        o = jnp.einsum("nqk,nkd->nqd", p.astype(BF16), vb, preferred_element_type=F32) / l
        pieces += [o[:, g * tq:(g + 1) * tq, :] for g in range(gq)]
    yb = jnp.concatenate(pieces, axis=2).reshape(rows, d_q)
    mixed = jnp.concatenate([ya, yb], axis=1).astype(BF16)
    o_ref[...] = x + _dot(mixed, wout_ref[0])


def _mixer_sample(x2, h0r, h0i, kc, vc, idx, layer, wts, ns):
    rows_all, dm = x2.shape
    n_seq, w_rows, d_kv = kc.shape
    tq = rows_all // n_seq
    n_state = h0r.shape[-1]
    w = wts["bcat"].shape[-1] // 2
    layer_spec = lambda a, i: pl.BlockSpec((1,) + a.shape[1:], lambda s: (i,) + (0,) * (a.ndim - 1),
                                           pipeline_mode=pl.Buffered(1))
    row_spec = pl.BlockSpec((ns * tq, dm), lambda s: (s, 0))
    st_spec = pl.BlockSpec((ns, n_state), lambda s: (s, 0))
    kv_spec = pl.BlockSpec((ns, w_rows, d_kv), lambda s: (s, 0, 0))
    ins = [
        (x2, row_spec), (h0r, st_spec), (h0i, st_spec), (kc, kv_spec), (vc, kv_spec),
        (wts["g_mix"], layer_spec(wts["g_mix"], layer)),
        (wts["w_in"], layer_spec(wts["w_in"], idx)),
        (wts["bcat"], layer_spec(wts["bcat"], idx)),
        (wts["ccat"], layer_spec(wts["ccat"], idx)),
        (wts["tre"], layer_spec(wts["tre"], idx)),
        (wts["tim"], layer_spec(wts["tim"], idx)),
        (wts["ssm_d"], layer_spec(wts["ssm_d"], idx)),
        (wts["w_glu"], layer_spec(wts["w_glu"], idx)),
        (wts["b_glu"], layer_spec(wts["b_glu"], idx)),
        (wts["bias_s"], _const_spec(wts["bias_s"].shape)),
        (wts["sink_s"], layer_spec(wts["sink_s"], idx)),
        (wts["w_out"], layer_spec(wts["w_out"], idx)),
    ]
    return pl.pallas_call(
        _mixer_sample_kernel,
        grid=(n_seq // ns,),
        in_specs=[s for _, s in ins],
        out_specs=[row_spec, st_spec, st_spec, kv_spec, kv_spec],
        out_shape=[jax.ShapeDtypeStruct(x2.shape, F32), jax.ShapeDtypeStruct(h0r.shape, F32),
                   jax.ShapeDtypeStruct(h0i.shape, F32), jax.ShapeDtypeStruct(kc.shape, F32),
                   jax.ShapeDtypeStruct(vc.shape, F32)],
        scratch_shapes=[pltpu.VMEM((ns * tq, 2 * w), F32)],
        compiler_params=_cparams(("arbitrary",)),
        name=f"mixer_sample_l{layer}",
    )(*[a for a, _ in ins])


def _conformer_kernel(*refs, carry, pad_rows, width):
    if carry:
        (x_ref, g_ref, w1_ref, wdw_ref, bdw_ref, lng_ref, lnb_ref, w2_ref, o_ref, st_ref, ext, ybuf,
         stage) = refs
        past_ref = None
        t = pl.program_id(1)
        x = x_ref[0]
    else:
        (x_ref, past_ref, g_ref, w1_ref, wdw_ref, bdw_ref, lng_ref, lnb_ref, w2_ref, o_ref, st_ref,
         ext, ybuf, stage) = refs
        x = x_ref[...]
    ns, le, c = ext.shape
    l = le - CONV_HALO
    rows = ns * l
    off = CONV_HALO - (width - 1)

    if carry:
        @pl.when(t == 0)
        def _():
            ext[:, 0:CONV_HALO, :] = jnp.zeros((ns, CONV_HALO, c), F32)
    else:
        ext[:, 0:CONV_HALO, :] = past_ref[...]

    h = _rmsnorm(x, g_ref[0]).astype(BF16)
    z = _dot(h, w1_ref[0])
    gl = z[:, :c] * jax.nn.sigmoid(z[:, c:])
    ext[:, CONV_HALO:, :] = gl.reshape(ns, l, c)
    st_ref[...] = ext[:, l + off:, :].reshape(st_ref.shape)

    def conv_unit(n, r0, rc, stg):
        win = ext[n, pl.ds(r0, rc + CONV_HALO), :]
        for s in range(1, SUBLANES):
            stg[s - 1] = win[s:s + rc + CONV_HALO - SUBLANES]
        acc = jnp.zeros((rc // SUBLANES, SUBLANES, c), F32)
        for k in range(width):
            a, s = divmod(off + k, SUBLANES)
            if s == 0:
                xk = ext[n, pl.ds(r0 + a * SUBLANES, rc), :]
            else:
                xk = stg[s - 1, a * SUBLANES:a * SUBLANES + rc, :]
            acc = acc + wdw_ref[0, k][None] * xk.reshape(rc // SUBLANES, SUBLANES, c)
        return acc.reshape(rc, c)

    if carry:
        def chunk(i, _):
            for half in range(2):
                r0 = pl.multiple_of((2 * i + half) * CONV_ROWS, CONV_ROWS)
                ybuf[pl.ds(r0, CONV_ROWS), :] = conv_unit(0, r0, CONV_ROWS, stage.at[half])
            return 0
        assert l % (2 * CONV_ROWS) == 0
        lax.fori_loop(0, l // (2 * CONV_ROWS), chunk, 0)
    else:
        def seq(i, _):
            for half in range(2):
                n = 2 * i + half
                ybuf[pl.ds(pl.multiple_of(n * l, l), l), :] = conv_unit(n, 0, l, stage.at[half])
            return 0
        assert ns % 2 == 0 and l % SUBLANES == 0
        lax.fori_loop(0, ns // 2, seq, 0)

    y = ybuf[...] + bdw_ref[0]
    mu = jnp.mean(y, axis=-1, keepdims=True)
    yc = y - mu
    var = jnp.mean(yc * yc, axis=-1, keepdims=True)
    y = yc * lax.rsqrt(var + EPS) * lng_ref[0] + lnb_ref[0]
    y = jax.nn.silu(y).astype(BF16)
    out = _dot(y, w2_ref[0])
    if carry:
        rid = t * rows + lax.broadcasted_iota(jnp.int32, (rows, 1), 0)
        out = jnp.where(rid >= pad_rows, out, 0.0)
        o_ref[0] = x + out
        ext[:, 0:CONV_HALO, :] = ext[:, l:l + CONV_HALO, :]
    else:
        o_ref[...] = x + out


def _conformer(x, past, idx, layer, wts, tile, pad_rows):
    carry = past is None
    width = wts["w_dw"].shape[1]
    c = wts["w_dw"].shape[3]
    if carry:
        nb, tp, dm = x.shape
        grid = (nb, tp // tile)
        lspec = lambda a, i: pl.BlockSpec((1,) + a.shape[1:], lambda b, t: (i,) + (0,) * (a.ndim - 1),
                                          pipeline_mode=pl.Buffered(1))
        xspec = pl.BlockSpec((1, tile, dm), lambda b, t: (b, t, 0))
        ins = [(x, xspec)]
        st_shape = (nb, width - 1, c)
        st_spec = pl.BlockSpec((1, width - 1, c), lambda b, t: (b, 0, 0))
        ns, l = 1, tile
        sem = ("arbitrary", "arbitrary")
    else:
        rows_all, dm = x.shape
        n_seq = past.shape[0]
        l = rows_all // n_seq
        ns = tile
        grid = (n_seq // ns,)
        lspec = lambda a, i: pl.BlockSpec((1,) + a.shape[1:], lambda s: (i,) + (0,) * (a.ndim - 1),
                                          pipeline_mode=pl.Buffered(1))
        xspec = pl.BlockSpec((ns * l, dm), lambda s: (s, 0))
        ins = [(x, xspec), (past, pl.BlockSpec((ns, CONV_HALO, c), lambda s: (s, 0, 0)))]
        st_shape = (n_seq, width - 1, c)
        st_spec = pl.BlockSpec((ns, width - 1, c), lambda s: (s, 0, 0))
        sem = ("arbitrary",)
    ins += [
        (wts["g_mix"], lspec(wts["g_mix"], layer)),
        (wts["w_pw1"], lspec(wts["w_pw1"], idx)),
        (wts["w_dw"], lspec(wts["w_dw"], idx)),
        (wts["b_dw"], lspec(wts["b_dw"], idx)),
        (wts["ln_g"], lspec(wts["ln_g"], idx)),
        (wts["ln_b"], lspec(wts["ln_b"], idx)),
        (wts["w_pw2"], lspec(wts["w_pw2"], idx)),
    ]
    return pl.pallas_call(
        functools.partial(_conformer_kernel, carry=carry, pad_rows=pad_rows, width=width),
        grid=grid,
        in_specs=[s for _, s in ins],
        out_specs=[xspec, st_spec],
        out_shape=[jax.ShapeDtypeStruct(x.shape, F32), jax.ShapeDtypeStruct(st_shape, F32)],
        scratch_shapes=[pltpu.VMEM((ns, l + CONV_HALO, c), F32), pltpu.VMEM((ns * l, c), F32)]
        + [pltpu.VMEM((2, SUBLANES - 1, (CONV_ROWS if carry else l) + CONV_HALO - SUBLANES, c), F32)],
        compiler_params=_cparams(sem),
        name=f"conformer_{'prompt' if carry else 'sample'}_l{layer}",
    )(*[a for a, _ in ins])


def _ffn_kernel(*refs, carry, final_norm, n_chunks):
    if carry:
        (x_ref, g_ref, wup_ref, wcv_ref, bcv_ref, wdn_ref, gf_ref, o_ref, st_ref, ext) = refs
        past_ref = None
        t = pl.program_id(1)
        x = x_ref[0]
    else:
        (x_ref, past_ref, g_ref, wup_ref, wcv_ref, bcv_ref, wdn_ref, gf_ref, o_ref, st_ref, ext) = refs
        x = x_ref[...]
    ns, le, dff = ext.shape
    l = le - FFN_HALO
    rows = ns * l
    kw = wcv_ref.shape[1]
    ch = dff // n_chunks

    if carry:
        @pl.when(t == 0)
        def _():
            ext[:, 0:FFN_HALO, :] = jnp.zeros((ns, FFN_HALO, dff), F32)
    else:
        ext[:, 0:FFN_HALO, :] = past_ref[...]

    h = _rmsnorm(x, g_ref[0]).astype(BF16)
    acc = x
    for cix in range(n_chunks):
        c0 = cix * ch
        gate = _dot(h, wup_ref[0, :, c0:c0 + ch])
        up = _dot(h, wup_ref[0, :, dff + c0:dff + c0 + ch])
        ext[:, FFN_HALO:, c0:c0 + ch] = gate.reshape(ns, l, ch)
        gc = bcv_ref[0, :, c0:c0 + ch]
        for k in range(kw):
            o = FFN_HALO - (kw - 1) + k
            gc = gc + wcv_ref[0, k:k + 1, c0:c0 + ch] * ext[:, o:o + l, c0:c0 + ch]
        y = (jax.nn.gelu(gc).reshape(rows, ch) * up).astype(BF16)
        acc = acc + _dot(y, wdn_ref[0, c0:c0 + ch, :])
    st_ref[...] = ext[:, le - (kw - 1):, :].reshape(st_ref.shape)
    if final_norm:
        acc = _rmsnorm(acc, gf_ref[...])
    if carry:
        o_ref[0] = acc
        ext[:, 0:FFN_HALO, :] = ext[:, l:l + FFN_HALO, :]
    else:
        o_ref[...] = acc


def _ffn(x, past, layer, wts, tile, final_norm):
    carry = past is None
    dff = wts["w_dn"].shape[1]
    kw = wts["w_cv"].shape[1]
    if carry:
        nb, tp, dm = x.shape
        grid = (nb, tp // tile)
        lspec = lambda a, i: pl.BlockSpec((1,) + a.shape[1:], lambda b, t: (i,) + (0,) * (a.ndim - 1),
                                          pipeline_mode=pl.Buffered(1))
        xspec = pl.BlockSpec((1, tile, dm), lambda b, t: (b, t, 0))
        ins = [(x, xspec)]
        st_shape = (nb, kw - 1, dff)
        st_spec = pl.BlockSpec((1, kw - 1, dff), lambda b, t: (b, 0, 0))
        ns, l = 1, tile
        sem = ("arbitrary", "arbitrary")
    else:
        rows_all, dm = x.shape
        n_seq = past.shape[0]
        l = rows_all // n_seq
        ns = tile
        grid = (n_seq // ns,)
        lspec = lambda a, i: pl.BlockSpec((1,) + a.shape[1:], lambda s: (i,) + (0,) * (a.ndim - 1),
                                          pipeline_mode=pl.Buffered(1))
        xspec = pl.BlockSpec((ns * l, dm), lambda s: (s, 0))
        ins = [(x, xspec), (past, pl.BlockSpec((ns, FFN_HALO, dff), lambda s: (s, 0, 0)))]
        st_shape = (n_seq, kw - 1, dff)
        st_spec = pl.BlockSpec((ns, kw - 1, dff), lambda s: (s, 0, 0))
        sem = ("arbitrary",)
    ins += [
        (wts["g_ffn"], lspec(wts["g_ffn"], layer)),
        (wts["w_up"], lspec(wts["w_up"], layer)),
        (wts["w_cv"], lspec(wts["w_cv"], layer)),
        (wts["b_cv"], lspec(wts["b_cv"], layer)),
        (wts["w_dn"], lspec(wts["w_dn"], layer)),
        (wts["g_final"], _const_spec(wts["g_final"].shape)),
    ]
    n_chunks = 2 if dff % 256 == 0 else 1
    return pl.pallas_call(
        functools.partial(_ffn_kernel, carry=carry, final_norm=final_norm, n_chunks=n_chunks),
        grid=grid,
        in_specs=[s for _, s in ins],
        out_specs=[xspec, st_spec],
        out_shape=[jax.ShapeDtypeStruct(x.shape, F32), jax.ShapeDtypeStruct(st_shape, F32)],
        scratch_shapes=[pltpu.VMEM((ns, l + FFN_HALO, dff), F32)],
        compiler_params=_cparams(sem),
        name=f"ffn_{'prompt' if carry else 'sample'}_l{layer}",
    )(*[a for a, _ in ins])


def _pick_tile(total, target):
    best = WINDOW
    for m in range(1, total // WINDOW + 1):
        if total % (m * WINDOW) == 0 and m * WINDOW <= target:
            best = m * WINDOW
    return best


def kernel(x_prompt, x_sample, state_ssm_re, state_ssm_im, cache_swa_k, cache_swa_v, state_conv, state_ffn,
           meta_tokens, g_mix, g_ffn, g_final, w_in_mix, ssm_lambda_re, ssm_lambda_im, ssm_log_step,
           ssm_b_re, ssm_b_im, ssm_c_re, ssm_c_im, ssm_d, ssm_w_glu, ssm_b_glu, rel_bias, attn_sinks,
           w_out_mix, conv_w_pw1, conv_w_dw, conv_b_dw, conv_ln_g, conv_ln_b, conv_w_pw2,
           ffn_w_up, ffn_w_conv, ffn_b_conv, ffn_w_down):
    nb, seq, dm = x_prompt.shape
    n_seq, tq, _ = x_sample.shape
    depth = g_mix.shape[0]
    n_meta = meta_tokens.shape[0]
    n_even, g_ssm, p_ssm = ssm_lambda_re.shape
    n_heads = rel_bias.shape[1]
    gq = n_heads // N_KV_HEADS
    w_rows = cache_swa_k.shape[2]
    d_kv = N_KV_HEADS * HEAD_DIM
    conv_w = conv_w_dw.shape[1]
    ffn_w = ffn_w_conv.shape[1]
    assert tq == SUBLANES and w_rows == WINDOW and seq % WINDOW == 0 and n_meta <= WINDOW
    assert conv_w - 1 <= CONV_HALO and ffn_w - 1 <= FFN_HALO

    tre, tim, bcat, ccat = _ssm_prep(ssm_lambda_re, ssm_lambda_im, ssm_log_step,
                                     ssm_b_re, ssm_b_im, ssm_c_re, ssm_c_im)
    bias = _bias_table(rel_bias)
    row3 = lambda a: a.reshape(a.shape[0], 1, a.shape[-1]).astype(F32)
    sinks = attn_sinks.astype(F32).reshape(n_even, N_KV_HEADS, gq, 1)
    wts = {
        "g_mix": row3(g_mix), "g_ffn": row3(g_ffn), "g_final": g_final.reshape(1, dm).astype(F32),
        "w_in": w_in_mix.astype(BF16), "bcat": bcat, "ccat": ccat, "tre": tre, "tim": tim,
        "ssm_d": row3(ssm_d), "w_glu": ssm_w_glu.astype(BF16), "b_glu": row3(ssm_b_glu),
        "bias_p": bias.reshape(N_KV_HEADS, gq * WINDOW, 2 * WINDOW),
        "bias_s": bias[:, :tq, :w_rows + tq].reshape(N_KV_HEADS, gq * tq, w_rows + tq),
        "sink_p": jnp.broadcast_to(sinks[:, :, :, None, :], (n_even, N_KV_HEADS, gq, WINDOW, 1)
                                   ).reshape(n_even, N_KV_HEADS, gq * WINDOW, 1),
        "sink_s": jnp.broadcast_to(sinks[:, :, :, None, :], (n_even, N_KV_HEADS, gq, tq, 1)
                                   ).reshape(n_even, N_KV_HEADS, gq * tq, 1),
        "w_out": w_out_mix.astype(BF16),
        "w_pw1": conv_w_pw1.astype(BF16), "w_dw": jnp.broadcast_to(conv_w_dw.astype(F32)[:, :, None, :],
                                 conv_w_dw.shape[:2] + (SUBLANES, conv_w_dw.shape[2])), "b_dw": row3(conv_b_dw),
        "ln_g": row3(conv_ln_g), "ln_b": row3(conv_ln_b), "w_pw2": conv_w_pw2.astype(BF16),
        "w_up": ffn_w_up.astype(BF16), "w_cv": ffn_w_conv.astype(F32), "b_cv": row3(ffn_b_conv),
        "w_dn": ffn_w_down.astype(BF16),
    }

    pad_rows = WINDOW - n_meta
    tp = pad_rows + n_meta + seq
    tm = _pick_tile(tp, PROMPT_TILE_TARGET)
    xp = jnp.concatenate([jnp.zeros((nb, pad_rows, dm), F32),
                          jnp.broadcast_to(meta_tokens.astype(F32)[None], (nb, n_meta, dm)),
                          x_prompt.astype(F32)], axis=1)
    xs = x_sample.astype(F32).reshape(n_seq * tq, dm)
    ns = min(SAMPLE_SEQS, n_seq)
    assert n_seq % ns == 0

    sr_p, si_p, k_p, v_p, c_p, f_p = [], [], [], [], [], []
    sr_s, si_s, k_s, v_s, c_s, f_s = [], [], [], [], [], []
    for layer in range(depth):
        idx = layer // 2
        if layer % 2 == 0:
            xp, sre, sim, ko, vo = _mixer_prompt(xp, idx, layer, wts, tm, pad_rows)
            sr_p.append(sre.reshape(nb, g_ssm, p_ssm))
            si_p.append(sim.reshape(nb, g_ssm, p_ssm))
            k_p.append(ko.reshape(nb, WINDOW, N_KV_HEADS, HEAD_DIM))
            v_p.append(vo.reshape(nb, WINDOW, N_KV_HEADS, HEAD_DIM))
            xs, sre, sim, ko, vo = _mixer_sample(
                xs, state_ssm_re[idx].astype(F32).reshape(n_seq, g_ssm * p_ssm),
                state_ssm_im[idx].astype(F32).reshape(n_seq, g_ssm * p_ssm),
                cache_swa_k[idx].astype(F32).reshape(n_seq, w_rows, d_kv),
                cache_swa_v[idx].astype(F32).reshape(n_seq, w_rows, d_kv), idx, layer, wts, ns)
            sr_s.append(sre.reshape(n_seq, g_ssm, p_ssm))
            si_s.append(sim.reshape(n_seq, g_ssm, p_ssm))
            k_s.append(ko.reshape(n_seq, w_rows, N_KV_HEADS, HEAD_DIM))
            v_s.append(vo.reshape(n_seq, w_rows, N_KV_HEADS, HEAD_DIM))
        else:
            xp, st = _conformer(xp, None, idx, layer, wts, tm, pad_rows)
            c_p.append(st)
            past = jnp.pad(state_conv[idx].astype(F32), ((0, 0), (CONV_HALO - (conv_w - 1), 0), (0, 0)))
            xs, st = _conformer(xs, past, idx, layer, wts, ns, 0)
            c_s.append(st)
        last = layer == depth - 1
        xp, st = _ffn(xp, None, layer, wts, tm, last)
        f_p.append(st)
        past = jnp.pad(state_ffn[layer].astype(F32), ((0, 0), (FFN_HALO - (ffn_w - 1), 0), (0, 0)))
        xs, st = _ffn(xs, past, layer, wts, ns, last)
        f_s.append(st)

    yp = xp[:, pad_rows + n_meta:]
    ys = xs.reshape(n_seq, tq, dm)
    st = jnp.stack
    return (yp, ys, st(sr_p), st(si_p), st(k_p), st(v_p), st(c_p), st(f_p),
            st(sr_s), st(si_s), st(k_s), st(v_s), st(c_s), st(f_s))
```

```python
import functools
import math

import numpy as np
import jax
import jax.numpy as jnp
from jax import lax
from jax.experimental import pallas as pl
from jax.experimental.pallas import tpu as pltpu

F32 = jnp.float32
BF16 = jnp.bfloat16

EPS = 1e-6
NEG_INF = -1e30
WINDOW = 128
HEAD_DIM = 64
N_KV_HEADS = 2
N_BUCKETS = 32
MAX_DISTANCE = 128
SSM_GROUP = 16
SSM_STATE = 64
SSM_BLK_GROUPS = 8
SUBLANES = 8
CONV_HALO = 32
FFN_HALO = 8
CONV_ROWS = 32
PROMPT_TILE_TARGET = 768
SAMPLE_SEQS = 32
VMEM_LIMIT = 56 * 1024 * 1024


def _cparams(sem):
    return pltpu.CompilerParams(dimension_semantics=sem, vmem_limit_bytes=VMEM_LIMIT)


def _const_spec(shape):
    nd = len(shape)
    return pl.BlockSpec(shape, lambda *_: (0,) * nd, pipeline_mode=pl.Buffered(1))


def _rmsnorm(x, g):
    return x * lax.rsqrt(jnp.mean(x * x, axis=-1, keepdims=True) + EPS) * g


def _dot(a, b):
    return jnp.dot(a, b, preferred_element_type=F32)


def _ssm_prep_kernel(lr_ref, li_ref, ls_ref, bre_ref, bim_ref, cre_ref, cim_ref,
                     tre_ref, tim_ref, kst_ref, bst_ref, mst_ref):
    lr = lr_ref[0]
    li = li_ref[0]
    dt = jnp.exp(ls_ref[0])
    decay = jnp.exp(lr * dt)
    a_re = decay * jnp.cos(li * dt)
    a_im = decay * jnp.sin(li * dt)
    den = lr * lr + li * li
    num_re = a_re - 1.0
    coef_re = (num_re * lr + a_im * li) / den
    coef_im = (a_im * lr - num_re * li) / den

    cmul = lambda x, y: (x[0] * y[0] - x[1] * y[1], x[0] * y[1] + x[1] * y[0])
    n = lr.shape[-1]
    pw = [(jnp.ones((1, n), F32), jnp.zeros((1, n), F32)), (a_re, a_im)]
    for _ in range(SUBLANES - 1):
        pw.append(cmul(pw[-1], (a_re, a_im)))
    apw = [pw[SUBLANES]]
    for _ in range(SUBLANES - 1):
        apw.append(cmul(apw[-1], pw[SUBLANES]))
    row = lax.broadcasted_iota(jnp.int32, (SUBLANES, n), 0)
    zero = jnp.zeros((SUBLANES, n), F32)
    for t, k in enumerate((1, 2, 4)):
        tre_ref[0, t] = jnp.where(row >= k, jnp.broadcast_to(apw[k - 1][0], (SUBLANES, n)), zero)
        tim_ref[0, t] = jnp.where(row >= k, jnp.broadcast_to(apw[k - 1][1], (SUBLANES, n)), zero)
    ap_re, ap_im = zero, zero
    for j in range(SUBLANES):
        ap_re = jnp.where(row == j, jnp.broadcast_to(apw[j][0], (SUBLANES, n)), ap_re)
        ap_im = jnp.where(row == j, jnp.broadcast_to(apw[j][1], (SUBLANES, n)), ap_im)
    tre_ref[0, 3] = ap_re
    tim_ref[0, 3] = ap_im

    bre = bre_ref[0]
    bim = bim_ref[0]
    bbar = (coef_re * bre - coef_im * bim, coef_re * bim + coef_im * bre)
    rows = SSM_BLK_GROUPS * SSM_GROUP
    r_i = lax.broadcasted_iota(jnp.int32, (rows, n), 0)
    c_i = lax.broadcasted_iota(jnp.int32, (rows, n), 1)
    sh = lambda v, d: lax.shift_right_logical(v, jnp.full(v.shape, int(math.log2(d)), jnp.int32))
    diag = sh(r_i, SSM_GROUP) == (sh(c_i, SSM_STATE) & (SSM_BLK_GROUPS - 1))
    zer = jnp.zeros((rows, n), F32)
    blockdiag = lambda v: jnp.where(diag, jnp.concatenate([v] * SSM_BLK_GROUPS, axis=0), zer)
    bb = (blockdiag(bbar[0]), blockdiag(bbar[1]))
    cc = (blockdiag(cre_ref[0]), blockdiag(cim_ref[0]))
    w = SSM_BLK_GROUPS * SSM_STATE
    nblk = n // w
    for d in range(SUBLANES):
        ba = cmul(bb, pw[d])
        ca = cmul(cc, pw[d + 1])
        for blk in range(nblk):
            sl = slice(blk * w, (blk + 1) * w)
            rs = slice(d * rows, (d + 1) * rows)
            bcat = jnp.concatenate([ba[0][:, sl], ba[1][:, sl]], axis=1)
            ccat = jnp.concatenate([cc[0][:, sl], -cc[1][:, sl]], axis=1)
            bst_ref[0, blk, rs, :] = bcat.astype(BF16)
            kst_ref[0, blk, rs, :] = lax.dot_general(
                bcat, ccat, (((1,), (1,)), ((), ())), precision=lax.Precision.HIGHEST,
                preferred_element_type=F32).astype(BF16)
            mcat = jnp.concatenate([ca[0][:, sl], -ca[1][:, sl]], axis=1)
            mst_ref[0, blk, :, rs] = mcat.T.astype(BF16)


def _ssm_prep(lam_re, lam_im, log_step, b_re, b_im, c_re, c_im):
    ne, g, p = lam_re.shape
    n = g * p
    assert g % SSM_BLK_GROUPS == 0 and p == SSM_STATE and b_re.shape[-1] == SSM_GROUP
    nblk = g // SSM_BLK_GROUPS
    rows = SSM_BLK_GROUPS * SSM_GROUP
    w = SSM_BLK_GROUPS * SSM_STATE
    flat = lambda a: a.reshape(ne, 1, n)
    ls = jnp.broadcast_to(log_step[:, :, None], (ne, g, p))
    bt = lambda a: jnp.transpose(a, (0, 3, 1, 2)).reshape(ne, SSM_GROUP, n)
    ct = lambda a: jnp.transpose(a, (0, 2, 1, 3)).reshape(ne, SSM_GROUP, n)
    vec = pl.BlockSpec((1, 1, n), lambda i: (i, 0, 0))
    mat = pl.BlockSpec((1, SSM_GROUP, n), lambda i: (i, 0, 0))
    tab = pl.BlockSpec((1, 4, SUBLANES, n), lambda i: (i, 0, 0, 0))
    stk = lambda cols: pl.BlockSpec((1, nblk, SUBLANES * rows, cols), lambda i: (i, 0, 0, 0))
    mspec = pl.BlockSpec((1, nblk, 2 * w, SUBLANES * rows), lambda i: (i, 0, 0, 0))
    return pl.pallas_call(
        _ssm_prep_kernel,
        grid=(ne,),
        in_specs=[vec, vec, vec, mat, mat, mat, mat],
        out_specs=[tab, tab, stk(rows), stk(2 * w), mspec],
        out_shape=[jax.ShapeDtypeStruct((ne, 4, SUBLANES, n), F32)] * 2
        + [jax.ShapeDtypeStruct((ne, nblk, SUBLANES * rows, rows), BF16),
           jax.ShapeDtypeStruct((ne, nblk, SUBLANES * rows, 2 * w), BF16),
           jax.ShapeDtypeStruct((ne, nblk, 2 * w, SUBLANES * rows), BF16)],
        compiler_params=pltpu.CompilerParams(vmem_limit_bytes=VMEM_LIMIT),
        name="ssm_prep",
    )(flat(lam_re), flat(lam_im), flat(ls), bt(b_re), bt(b_im), ct(c_re), ct(c_im))


def _bucket_table():
    i = np.arange(WINDOW)[:, None]
    j = np.arange(2 * WINDOW)[None, :]
    dist = WINDOW + i - j
    nn = np.maximum(dist, 0)
    max_exact = N_BUCKETS // 2
    nf = np.maximum(nn, max_exact).astype(np.float32)
    large = max_exact + (np.log(nf / np.float32(max_exact)) / np.float32(math.log(MAX_DISTANCE / max_exact))
                         * np.float32(N_BUCKETS - max_exact)).astype(np.int32)
    large = np.minimum(large, N_BUCKETS - 1)
    bucket = np.where(nn < max_exact, nn, large)
    valid = (dist >= 0) & (dist < WINDOW)
    return np.where(valid, bucket, -1).astype(np.int32)


def _bias_kernel(rb_ref, bucket_ref, out_ref):
    bucket = bucket_ref[...]
    nh = out_ref.shape[0]
    for h in range(nh):
        acc = jnp.full(bucket.shape, NEG_INF, F32)
        for b in range(N_BUCKETS):
            acc = jnp.where(bucket == b, rb_ref[b, h], acc)
        out_ref[h] = acc


def _bias_table(rel_bias):
    nh = rel_bias.shape[1]
    bucket = jnp.asarray(_bucket_table())
    return pl.pallas_call(
        _bias_kernel,
        in_specs=[pl.BlockSpec(memory_space=pltpu.SMEM),
                  pl.BlockSpec(bucket.shape, lambda: (0, 0))],
        out_specs=pl.BlockSpec((nh,) + bucket.shape, lambda: (0, 0, 0)),
        out_shape=jax.ShapeDtypeStruct((nh,) + bucket.shape, F32),
        name="rel_bias_table",
    )(rel_bias.astype(F32), bucket)


def _block_state_scan(sbuf, tre_ref, tim_ref, lane0, w, carry):
    lanes = pl.ds(lane0, w)
    first = lax.broadcasted_iota(jnp.int32, (SUBLANES, w), 0) == 0

    def body(b, carry):
        hr, hi = carry
        r0 = pl.multiple_of(b * SUBLANES, SUBLANES)
        xr = sbuf[pl.ds(r0, SUBLANES), 0:w]
        xi = sbuf[pl.ds(r0, SUBLANES), w:2 * w]
        for t, k in enumerate((1, 2, 4)):
            cr = tre_ref[0, t, :, lanes]
            ci = tim_ref[0, t, :, lanes]
            rr = pltpu.roll(xr, k, 0)
            ri = pltpu.roll(xi, k, 0)
            xr, xi = xr + cr * rr - ci * ri, xi + cr * ri + ci * rr
        pr = tre_ref[0, 3, :, lanes]
        pi = tim_ref[0, 3, :, lanes]
        xr, xi = xr + pr * hr - pi * hi, xi + pr * hi + pi * hr
        sbuf[pl.ds(r0, SUBLANES), 0:w] = jnp.where(first, hr, pltpu.roll(xr, 1, 0))
        sbuf[pl.ds(r0, SUBLANES), w:2 * w] = jnp.where(first, hi, pltpu.roll(xi, 1, 0))
        return (jnp.broadcast_to(xr[SUBLANES - 1:SUBLANES, :], (SUBLANES, w)),
                jnp.broadcast_to(xi[SUBLANES - 1:SUBLANES, :], (SUBLANES, w)))

    return lax.fori_loop(0, sbuf.shape[0] // SUBLANES, body, carry)


def _s5_mix(u, uext, yint, kst_ref, bst_ref, mst_ref, d, wglu_ref, bglu, advance):
    nblk = kst_ref.shape[1]
    cw = kst_ref.shape[3]
    rows = u.shape[0]
    nb = rows // SUBLANES
    j_i = lax.broadcasted_iota(jnp.int32, (1, SUBLANES, 1), 1)
    ys = []
    for blk in range(nblk):
        ub = u[:, blk * cw:(blk + 1) * cw]
        u3 = ub.reshape(nb, SUBLANES, cw)
        parts = [ub]
        for dd in range(1, SUBLANES):
            parts.append(jnp.where(j_i >= dd, pltpu.roll(u3, dd, 1), 0.0).reshape(rows, cw))
        for dd in range(SUBLANES):
            uext[dd] = parts[dd]
        y_intra = _dot(jnp.concatenate(parts, axis=1).astype(BF16), kst_ref[0, blk])
        last = [uext[dd, pl.ds(SUBLANES - 1, nb, stride=SUBLANES), :] for dd in range(SUBLANES)]
        s = _dot(jnp.concatenate(last, axis=1).astype(BF16), bst_ref[0, blk])
        hprev = advance(blk, s)
        yi = _dot(hprev.astype(BF16), mst_ref[0, blk])
        for j in range(SUBLANES):
            yint[pl.ds(j, nb, stride=SUBLANES), :] = yi[:, j * cw:(j + 1) * cw]
        ys.append(y_intra + yint[...])
    y = jnp.concatenate(ys, axis=1) + d * u
    g = jax.nn.gelu(y)
    return g * jax.nn.sigmoid(_dot(g.astype(BF16), wglu_ref[0]) + bglu)


def _mixer_prompt_kernel(x_ref, g_ref, win_ref, kst_ref, bst_ref, mst_ref, tre_ref, tim_ref, d_ref, wglu_ref,
                         bglu_ref, bias_ref, sink_ref, wout_ref,
                         o_ref, sre_ref, sim_ref, ko_ref, vo_ref,
                         uext, yint, sbuf, hst, kbuf, vbuf, mix, sc, pb, *, pad_rows):
    t = pl.program_id(1)
    tm = x_ref.shape[1]
    d_ssm = d_ref.shape[-1]
    d_q = bias_ref.shape[0] * (bias_ref.shape[1] // WINDOW) * HEAD_DIM
    d_kv = N_KV_HEADS * HEAD_DIM
    gq = bias_ref.shape[1] // WINDOW

    @pl.when(t == 0)
    def _():
        hst[...] = jnp.zeros_like(hst)
        kbuf[0:WINDOW, :] = jnp.zeros((WINDOW, d_kv), F32)
        vbuf[0:WINDOW, :] = jnp.zeros((WINDOW, d_kv), F32)

    x = x_ref[0]
    h = _rmsnorm(x, g_ref[0]).astype(BF16)
    z = _dot(h, win_ref[0])
    u = z[:, :d_ssm]
    q = z[:, d_ssm:d_ssm + d_q] * (HEAD_DIM ** -0.5)
    kbuf[WINDOW:, :] = z[:, d_ssm + d_q:d_ssm + d_q + d_kv]
    vbuf[WINDOW:, :] = z[:, d_ssm + d_q + d_kv:]

    n_state = hst.shape[1] // 2

    w = sbuf.shape[1] // 2

    def advance(blk, s):
        lane0 = blk * w
        sbuf[...] = s
        init = (hst[:, lane0:lane0 + w], hst[:, n_state + lane0:n_state + lane0 + w])
        hr, hi = _block_state_scan(sbuf, tre_ref, tim_ref, lane0, w, init)
        hst[:, lane0:lane0 + w] = hr
        hst[:, n_state + lane0:n_state + lane0 + w] = hi
        return sbuf[...]

    ya = _s5_mix(u, uext, yint, kst_ref, bst_ref, mst_ref, d_ref[0], wglu_ref, bglu_ref[0], advance)
    mix[:, 0:d_ssm] = ya.astype(BF16)
    sre_ref[0] = hst[0:1, 0:n_state]
    sim_ref[0] = hst[0:1, n_state:]

    nqb = tm // WINDOW
    band = lambda buf, qb, j: buf[qb * WINDOW:(qb + 2) * WINDOW, j * HEAD_DIM:(j + 1) * HEAD_DIM].astype(BF16)
    for qb in range(nqb):
        r0 = qb * WINDOW
        for j in range(N_KV_HEADS):
            qs = jnp.concatenate(
                [q[r0:r0 + WINDOW, (j * gq + g) * HEAD_DIM:(j * gq + g + 1) * HEAD_DIM] for g in range(gq)],
                axis=0).astype(BF16)
            sc[j, qb] = lax.dot_general(qs, band(kbuf, qb, j), (((1,), (1,)), ((), ())),
                                        preferred_element_type=F32)
    qb_i = lax.broadcasted_iota(jnp.int32, (nqb, 1, 2 * WINDOW), 0)
    col = lax.broadcasted_iota(jnp.int32, (nqb, 1, 2 * WINDOW), 2)
    kmask = col >= pad_rows + WINDOW - t * tm - qb_i * WINDOW
    dens = []
    for j in range(N_KV_HEADS):
        s = jnp.where(kmask, sc[j] + bias_ref[j][None], NEG_INF)
        sink = sink_ref[0, j][None]
        m = jnp.maximum(jnp.max(s, axis=-1, keepdims=True), sink)
        p = jnp.exp(s - m)
        dens.append(jnp.sum(p, axis=-1, keepdims=True) + jnp.exp(sink - m))
        pb[j] = p.astype(BF16)
    for qb in range(nqb):
        pieces = []
        for j in range(N_KV_HEADS):
            o = _dot(pb[j, qb], band(vbuf, qb, j)) / dens[j][qb]
            pieces += [o[g * WINDOW:(g + 1) * WINDOW] for g in range(gq)]
        mix[qb * WINDOW:(qb + 1) * WINDOW, d_ssm:] = jnp.concatenate(pieces, axis=1).astype(BF16)

    kbuf[0:WINDOW, :] = kbuf[tm:tm + WINDOW, :]
    vbuf[0:WINDOW, :] = vbuf[tm:tm + WINDOW, :]
    ko_ref[0] = kbuf[0:WINDOW, :]
    vo_ref[0] = vbuf[0:WINDOW, :]
    o_ref[0] = x + _dot(mix[...], wout_ref[0])


def _mixer_prompt(x, idx, layer, wts, tm, pad_rows):
    nb, tp, dm = x.shape
    nt = tp // tm
    n_state = wts["tre"].shape[-1]
    d_ssm = wts["ssm_d"].shape[-1]
    d_kv = N_KV_HEADS * HEAD_DIM
    w = wts["bst"].shape[-1] // 2
    cw = wts["kst"].shape[-1]
    layer_spec = lambda a, i: pl.BlockSpec((1,) + a.shape[1:], lambda b, t: (i,) + (0,) * (a.ndim - 1),
                                           pipeline_mode=pl.Buffered(1))
    ins = [
        (x, pl.BlockSpec((1, tm, dm), lambda b, t: (b, t, 0))),
        (wts["g_mix"], layer_spec(wts["g_mix"], layer)),
        (wts["w_in"], layer_spec(wts["w_in"], idx)),
        (wts["kst"], layer_spec(wts["kst"], idx)),
        (wts["bst"], layer_spec(wts["bst"], idx)),
        (wts["mst"], layer_spec(wts["mst"], idx)),
        (wts["tre"], layer_spec(wts["tre"], idx)),
        (wts["tim"], layer_spec(wts["tim"], idx)),
        (wts["ssm_d"], layer_spec(wts["ssm_d"], idx)),
        (wts["w_glu"], layer_spec(wts["w_glu"], idx)),
        (wts["b_glu"], layer_spec(wts["b_glu"], idx)),
        (wts["bias_p"], _const_spec(wts["bias_p"].shape)),
        (wts["sink_p"], layer_spec(wts["sink_p"], idx)),
        (wts["w_out"], layer_spec(wts["w_out"], idx)),
    ]
    out_shape = [
        jax.ShapeDtypeStruct((nb, tp, dm), F32),
        jax.ShapeDtypeStruct((nb, 1, n_state), F32),
        jax.ShapeDtypeStruct((nb, 1, n_state), F32),
        jax.ShapeDtypeStruct((nb, WINDOW, d_kv), F32),
        jax.ShapeDtypeStruct((nb, WINDOW, d_kv), F32),
    ]
    out_specs = [
        pl.BlockSpec((1, tm, dm), lambda b, t: (b, t, 0)),
        pl.BlockSpec((1, 1, n_state), lambda b, t: (b, 0, 0)),
        pl.BlockSpec((1, 1, n_state), lambda b, t: (b, 0, 0)),
        pl.BlockSpec((1, WINDOW, d_kv), lambda b, t: (b, 0, 0)),
        pl.BlockSpec((1, WINDOW, d_kv), lambda b, t: (b, 0, 0)),
    ]
    scratch = [
        pltpu.VMEM((SUBLANES, tm, cw), F32),
        pltpu.VMEM((tm, cw), F32),
        pltpu.VMEM((tm // SUBLANES, 2 * w), F32),
        pltpu.VMEM((SUBLANES, 2 * n_state), F32),
        pltpu.VMEM((tm + WINDOW, d_kv), F32),
        pltpu.VMEM((tm + WINDOW, d_kv), F32),
        pltpu.VMEM((tm, dm), BF16),
        pltpu.VMEM((N_KV_HEADS, tm // WINDOW) + wts["bias_p"].shape[1:], F32),
        pltpu.VMEM((N_KV_HEADS, tm // WINDOW) + wts["bias_p"].shape[1:], BF16),
    ]
    return pl.pallas_call(
        functools.partial(_mixer_prompt_kernel, pad_rows=pad_rows),
        grid=(nb, nt),
        in_specs=[s for _, s in ins],
        out_specs=out_specs,
        out_shape=out_shape,
        scratch_shapes=scratch,
        compiler_params=_cparams(("arbitrary", "arbitrary")),
        name=f"mixer_prompt_l{layer}",
    )(*[a for a, _ in ins])


def _mixer_sample_kernel(x_ref, h0r_ref, h0i_ref, kc_ref, vc_ref, g_ref, win_ref, kst_ref, bst_ref, mst_ref,
                         tre_ref, tim_ref, d_ref, wglu_ref, bglu_ref, bias_ref, sink_ref, wout_ref,
                         o_ref, sre_ref, sim_ref, ko_ref, vo_ref, uext, yint):
    rows = x_ref.shape[0]
    ns = kc_ref.shape[0]
    tq = rows // ns
    assert tq == SUBLANES
    w_rows = kc_ref.shape[1]
    d_ssm = d_ref.shape[-1]
    gq = bias_ref.shape[1] // tq
    d_q = N_KV_HEADS * gq * HEAD_DIM
    d_kv = N_KV_HEADS * HEAD_DIM

    x = x_ref[...]
    h = _rmsnorm(x, g_ref[0]).astype(BF16)
    z = _dot(h, win_ref[0])
    u = z[:, :d_ssm]
    q = (z[:, d_ssm:d_ssm + d_q] * (HEAD_DIM ** -0.5)).reshape(ns, tq, d_q)
    kn = z[:, d_ssm + d_q:d_ssm + d_q + d_kv].reshape(ns, tq, d_kv)
    vn = z[:, d_ssm + d_q + d_kv:].reshape(ns, tq, d_kv)

    w = bst_ref.shape[-1] // 2

    def advance(blk, s):
        lanes = slice(blk * w, (blk + 1) * w)
        hr = h0r_ref[:, lanes]
        hi = h0i_ref[:, lanes]
        ar = tre_ref[0, 0, SUBLANES - 1:SUBLANES, lanes]
        ai = tim_ref[0, 0, SUBLANES - 1:SUBLANES, lanes]
        sre_ref[:, lanes] = ar * hr - ai * hi + s[:, 0:w]
        sim_ref[:, lanes] = ar * hi + ai * hr + s[:, w:2 * w]
        return jnp.concatenate([hr, hi], axis=1)

    ya = _s5_mix(u, uext, yint, kst_ref, bst_ref, mst_ref, d_ref[0], wglu_ref, bglu_ref[0], advance)

    kc = jnp.concatenate([kc_ref[...], kn], axis=1)
    vc = jnp.concatenate([vc_ref[...], vn], axis=1)
    ko_ref[...] = kc[:, tq:, :]
    vo_ref[...] = vc[:, tq:, :]
    pieces = []
    for j in range(N_KV_HEADS):
        kb = kc[:, :, j * HEAD_DIM:(j + 1) * HEAD_DIM].astype(BF16)
        vb = vc[:, :, j * HEAD_DIM:(j + 1) * HEAD_DIM].astype(BF16)
        qs = jnp.concatenate(
            [q[:, :, (j * gq + g) * HEAD_DIM:(j * gq + g + 1) * HEAD_DIM] for g in range(gq)],
            axis=1).astype(BF16)
        s = jnp.einsum("nqd,nkd->nqk", qs, kb, preferred_element_type=F32) + bias_ref[j]
        sink = sink_ref[0, j]
        m = jnp.maximum(jnp.max(s, axis=-1, keepdims=True), sink)
        p = jnp.exp(s - m)
        l = jnp.sum(p, axis=-1, keepdims=True) + jnp.exp(sink - m)
        o = jnp.einsum("nqk,nkd->nqd", p.astype(BF16), vb, preferred_element_type=F32) / l
        pieces += [o[:, g * tq:(g + 1) * tq, :] for g in range(gq)]
    yb = jnp.concatenate(pieces, axis=2).reshape(rows, d_q)
    mixed = jnp.concatenate([ya, yb], axis=1).astype(BF16)
    o_ref[...] = x + _dot(mixed, wout_ref[0])


def _mixer_sample(x2, h0r, h0i, kc, vc, idx, layer, wts, ns):
    rows_all, dm = x2.shape
    n_seq, w_rows, d_kv = kc.shape
    tq = rows_all // n_seq
    n_state = h0r.shape[-1]
    cw = wts["kst"].shape[-1]
    layer_spec = lambda a, i: pl.BlockSpec((1,) + a.shape[1:], lambda s: (i,) + (0,) * (a.ndim - 1),
                                           pipeline_mode=pl.Buffered(1))
    row_spec = pl.BlockSpec((ns * tq, dm), lambda s: (s, 0))
    st_spec = pl.BlockSpec((ns, n_state), lambda s: (s, 0))
    kv_spec = pl.BlockSpec((ns, w_rows, d_kv), lambda s: (s, 0, 0))
    ins = [
        (x2, row_spec), (h0r, st_spec), (h0i, st_spec), (kc, kv_spec), (vc, kv_spec),
        (wts["g_mix"], layer_spec(wts["g_mix"], layer)),
        (wts["w_in"], layer_spec(wts["w_in"], idx)),
        (wts["kst"], layer_spec(wts["kst"], idx)),
        (wts["bst"], layer_spec(wts["bst"], idx)),
        (wts["mst"], layer_spec(wts["mst"], idx)),
        (wts["tre"], layer_spec(wts["tre"], idx)),
        (wts["tim"], layer_spec(wts["tim"], idx)),
        (wts["ssm_d"], layer_spec(wts["ssm_d"], idx)),
        (wts["w_glu"], layer_spec(wts["w_glu"], idx)),
        (wts["b_glu"], layer_spec(wts["b_glu"], idx)),
        (wts["bias_s"], _const_spec(wts["bias_s"].shape)),
        (wts["sink_s"], layer_spec(wts["sink_s"], idx)),
        (wts["w_out"], layer_spec(wts["w_out"], idx)),
    ]
    return pl.pallas_call(
        _mixer_sample_kernel,
        grid=(n_seq // ns,),
        in_specs=[s for _, s in ins],
        out_specs=[row_spec, st_spec, st_spec, kv_spec, kv_spec],
        out_shape=[jax.ShapeDtypeStruct(x2.shape, F32), jax.ShapeDtypeStruct(h0r.shape, F32),
                   jax.ShapeDtypeStruct(h0i.shape, F32), jax.ShapeDtypeStruct(kc.shape, F32),
                   jax.ShapeDtypeStruct(vc.shape, F32)],
        scratch_shapes=[pltpu.VMEM((SUBLANES, ns * tq, cw), F32), pltpu.VMEM((ns * tq, cw), F32)],
        compiler_params=_cparams(("arbitrary",)),
        name=f"mixer_sample_l{layer}",
    )(*[a for a, _ in ins])


def _conformer_kernel(*refs, carry, pad_rows, width):
    if carry:
        (x_ref, g_ref, w1_ref, wdw_ref, bdw_ref, lng_ref, lnb_ref, w2_ref, o_ref, st_ref, ext, ybuf,
         stage) = refs
        past_ref = None
        t = pl.program_id(1)
        x = x_ref[0]
    else:
        (x_ref, past_ref, g_ref, w1_ref, wdw_ref, bdw_ref, lng_ref, lnb_ref, w2_ref, o_ref, st_ref,
         ext, ybuf, stage) = refs
        x = x_ref[...]
    ns, le, c = ext.shape
    l = le - CONV_HALO
    rows = ns * l
    off = CONV_HALO - (width - 1)

    if carry:
        @pl.when(t == 0)
        def _():
            ext[:, 0:CONV_HALO, :] = jnp.zeros((ns, CONV_HALO, c), F32)
    else:
        ext[:, 0:CONV_HALO, :] = past_ref[...]

    h = _rmsnorm(x, g_ref[0]).astype(BF16)
    z = _dot(h, w1_ref[0])
    gl = z[:, :c] * jax.nn.sigmoid(z[:, c:])
    ext[:, CONV_HALO:, :] = gl.reshape(ns, l, c)
    st_ref[...] = ext[:, l + off:, :].reshape(st_ref.shape)

    def conv_unit(n, r0, rc, stg):
        win = ext[n, pl.ds(r0, rc + CONV_HALO), :]
        for s in range(1, SUBLANES):
            stg[s - 1] = win[s:s + rc + CONV_HALO - SUBLANES]
        acc = jnp.zeros((rc // SUBLANES, SUBLANES, c), F32)
        for k in range(width):
            a, s = divmod(off + k, SUBLANES)
            if s == 0:
                xk = ext[n, pl.ds(r0 + a * SUBLANES, rc), :]
            else:
                xk = stg[s - 1, a * SUBLANES:a * SUBLANES + rc, :]
            acc = acc + wdw_ref[0, k][None] * xk.reshape(rc // SUBLANES, SUBLANES, c)
        return acc.reshape(rc, c)

    if carry:
        def chunk(i, _):
            for half in range(2):
                r0 = pl.multiple_of((2 * i + half) * CONV_ROWS, CONV_ROWS)
                ybuf[pl.ds(r0, CONV_ROWS), :] = conv_unit(0, r0, CONV_ROWS, stage.at[half])
            return 0
        assert l % (2 * CONV_ROWS) == 0
        lax.fori_loop(0, l // (2 * CONV_ROWS), chunk, 0)
    else:
        def seq(i, _):
            for half in range(2):
                n = 2 * i + half
                ybuf[pl.ds(pl.multiple_of(n * l, l), l), :] = conv_unit(n, 0, l, stage.at[half])
            return 0
        assert ns % 2 == 0 and l % SUBLANES == 0
        lax.fori_loop(0, ns // 2, seq, 0)

    y = ybuf[...] + bdw_ref[0]
    mu = jnp.mean(y, axis=-1, keepdims=True)
    yc = y - mu
    var = jnp.mean(yc * yc, axis=-1, keepdims=True)
    y = yc * lax.rsqrt(var + EPS) * lng_ref[0] + lnb_ref[0]
    y = jax.nn.silu(y).astype(BF16)
    out = _dot(y, w2_ref[0])
    if carry:
        rid = t * rows + lax.broadcasted_iota(jnp.int32, (rows, 1), 0)
        out = jnp.where(rid >= pad_rows, out, 0.0)
        o_ref[0] = x + out
        ext[:, 0:CONV_HALO, :] = ext[:, l:l + CONV_HALO, :]
    else:
        o_ref[...] = x + out


def _conformer(x, past, idx, layer, wts, tile, pad_rows):
    carry = past is None
    width = wts["w_dw"].shape[1]
    c = wts["w_dw"].shape[3]
    if carry:
        nb, tp, dm = x.shape
        grid = (nb, tp // tile)
        lspec = lambda a, i: pl.BlockSpec((1,) + a.shape[1:], lambda b, t: (i,) + (0,) * (a.ndim - 1),
                                          pipeline_mode=pl.Buffered(1))
        xspec = pl.BlockSpec((1, tile, dm), lambda b, t: (b, t, 0))
        ins = [(x, xspec)]
        st_shape = (nb, width - 1, c)
        st_spec = pl.BlockSpec((1, width - 1, c), lambda b, t: (b, 0, 0))
        ns, l = 1, tile
        sem = ("arbitrary", "arbitrary")
    else:
        rows_all, dm = x.shape
        n_seq = past.shape[0]
        l = rows_all // n_seq
        ns = tile
        grid = (n_seq // ns,)
        lspec = lambda a, i: pl.BlockSpec((1,) + a.shape[1:], lambda s: (i,) + (0,) * (a.ndim - 1),
                                          pipeline_mode=pl.Buffered(1))
        xspec = pl.BlockSpec((ns * l, dm), lambda s: (s, 0))
        ins = [(x, xspec), (past, pl.BlockSpec((ns, CONV_HALO, c), lambda s: (s, 0, 0)))]
        st_shape = (n_seq, width - 1, c)
        st_spec = pl.BlockSpec((ns, width - 1, c), lambda s: (s, 0, 0))
        sem = ("arbitrary",)
    ins += [
        (wts["g_mix"], lspec(wts["g_mix"], layer)),
        (wts["w_pw1"], lspec(wts["w_pw1"], idx)),
        (wts["w_dw"], lspec(wts["w_dw"], idx)),
        (wts["b_dw"], lspec(wts["b_dw"], idx)),
        (wts["ln_g"], lspec(wts["ln_g"], idx)),
        (wts["ln_b"], lspec(wts["ln_b"], idx)),
        (wts["w_pw2"], lspec(wts["w_pw2"], idx)),
    ]
    return pl.pallas_call(
        functools.partial(_conformer_kernel, carry=carry, pad_rows=pad_rows, width=width),
        grid=grid,
        in_specs=[s for _, s in ins],
        out_specs=[xspec, st_spec],
        out_shape=[jax.ShapeDtypeStruct(x.shape, F32), jax.ShapeDtypeStruct(st_shape, F32)],
        scratch_shapes=[pltpu.VMEM((ns, l + CONV_HALO, c), F32), pltpu.VMEM((ns * l, c), F32)]
        + [pltpu.VMEM((2, SUBLANES - 1, (CONV_ROWS if carry else l) + CONV_HALO - SUBLANES, c), F32)],
        compiler_params=_cparams(sem),
        name=f"conformer_{'prompt' if carry else 'sample'}_l{layer}",
    )(*[a for a, _ in ins])


def _ffn_kernel(*refs, carry, final_norm, n_chunks):
    if carry:
        (x_ref, g_ref, wup_ref, wcv_ref, bcv_ref, wdn_ref, gf_ref, o_ref, st_ref, ext) = refs
        past_ref = None
        t = pl.program_id(1)
        x = x_ref[0]
    else:
        (x_ref, past_ref, g_ref, wup_ref, wcv_ref, bcv_ref, wdn_ref, gf_ref, o_ref, st_ref, ext) = refs
        x = x_ref[...]
    ns, le, dff = ext.shape
    l = le - FFN_HALO
    rows = ns * l
    kw = wcv_ref.shape[1]
    ch = dff // n_chunks

    if carry:
        @pl.when(t == 0)
        def _():
            ext[:, 0:FFN_HALO, :] = jnp.zeros((ns, FFN_HALO, dff), F32)
    else:
        ext[:, 0:FFN_HALO, :] = past_ref[...]

    h = _rmsnorm(x, g_ref[0]).astype(BF16)
    acc = x
    for cix in range(n_chunks):
        c0 = cix * ch
        gate = _dot(h, wup_ref[0, :, c0:c0 + ch])
        up = _dot(h, wup_ref[0, :, dff + c0:dff + c0 + ch])
        ext[:, FFN_HALO:, c0:c0 + ch] = gate.reshape(ns, l, ch)
        gc = bcv_ref[0, :, c0:c0 + ch]
        for k in range(kw):
            o = FFN_HALO - (kw - 1) + k
            gc = gc + wcv_ref[0, k:k + 1, c0:c0 + ch] * ext[:, o:o + l, c0:c0 + ch]
        y = (jax.nn.gelu(gc).reshape(rows, ch) * up).astype(BF16)
        acc = acc + _dot(y, wdn_ref[0, c0:c0 + ch, :])
    st_ref[...] = ext[:, le - (kw - 1):, :].reshape(st_ref.shape)
    if final_norm:
        acc = _rmsnorm(acc, gf_ref[...])
    if carry:
        o_ref[0] = acc
        ext[:, 0:FFN_HALO, :] = ext[:, l:l + FFN_HALO, :]
    else:
        o_ref[...] = acc


def _ffn(x, past, layer, wts, tile, final_norm):
    carry = past is None
    dff = wts["w_dn"].shape[1]
    kw = wts["w_cv"].shape[1]
    if carry:
        nb, tp, dm = x.shape
        grid = (nb, tp // tile)
        lspec = lambda a, i: pl.BlockSpec((1,) + a.shape[1:], lambda b, t: (i,) + (0,) * (a.ndim - 1),
                                          pipeline_mode=pl.Buffered(1))
        xspec = pl.BlockSpec((1, tile, dm), lambda b, t: (b, t, 0))
        ins = [(x, xspec)]
        st_shape = (nb, kw - 1, dff)
        st_spec = pl.BlockSpec((1, kw - 1, dff), lambda b, t: (b, 0, 0))
        ns, l = 1, tile
        sem = ("arbitrary", "arbitrary")
    else:
        rows_all, dm = x.shape
        n_seq = past.shape[0]
        l = rows_all // n_seq
        ns = tile
        grid = (n_seq // ns,)
        lspec = lambda a, i: pl.BlockSpec((1,) + a.shape[1:], lambda s: (i,) + (0,) * (a.ndim - 1),
                                          pipeline_mode=pl.Buffered(1))
        xspec = pl.BlockSpec((ns * l, dm), lambda s: (s, 0))
        ins = [(x, xspec), (past, pl.BlockSpec((ns, FFN_HALO, dff), lambda s: (s, 0, 0)))]
        st_shape = (n_seq, kw - 1, dff)
        st_spec = pl.BlockSpec((ns, kw - 1, dff), lambda s: (s, 0, 0))
        sem = ("arbitrary",)
    ins += [
        (wts["g_ffn"], lspec(wts["g_ffn"], layer)),
        (wts["w_up"], lspec(wts["w_up"], layer)),
        (wts["w_cv"], lspec(wts["w_cv"], layer)),
        (wts["b_cv"], lspec(wts["b_cv"], layer)),
        (wts["w_dn"], lspec(wts["w_dn"], layer)),
        (wts["g_final"], _const_spec(wts["g_final"].shape)),
    ]
    n_chunks = 2 if dff % 256 == 0 else 1
    return pl.pallas_call(
        functools.partial(_ffn_kernel, carry=carry, final_norm=final_norm, n_chunks=n_chunks),
        grid=grid,
        in_specs=[s for _, s in ins],
        out_specs=[xspec, st_spec],
        out_shape=[jax.ShapeDtypeStruct(x.shape, F32), jax.ShapeDtypeStruct(st_shape, F32)],
        scratch_shapes=[pltpu.VMEM((ns, l + FFN_HALO, dff), F32)],
        compiler_params=_cparams(sem),
        name=f"ffn_{'prompt' if carry else 'sample'}_l{layer}",
    )(*[a for a, _ in ins])


def _pick_tile(total, target):
    best = WINDOW
    for m in range(1, total // WINDOW + 1):
        if total % (m * WINDOW) == 0 and m * WINDOW <= target:
            best = m * WINDOW
    return best


def kernel(x_prompt, x_sample, state_ssm_re, state_ssm_im, cache_swa_k, cache_swa_v, state_conv, state_ffn,
           meta_tokens, g_mix, g_ffn, g_final, w_in_mix, ssm_lambda_re, ssm_lambda_im, ssm_log_step,
           ssm_b_re, ssm_b_im, ssm_c_re, ssm_c_im, ssm_d, ssm_w_glu, ssm_b_glu, rel_bias, attn_sinks,
           w_out_mix, conv_w_pw1, conv_w_dw, conv_b_dw, conv_ln_g, conv_ln_b, conv_w_pw2,
           ffn_w_up, ffn_w_conv, ffn_b_conv, ffn_w_down):
    nb, seq, dm = x_prompt.shape
    n_seq, tq, _ = x_sample.shape
    depth = g_mix.shape[0]
    n_meta = meta_tokens.shape[0]
    n_even, g_ssm, p_ssm = ssm_lambda_re.shape
    n_heads = rel_bias.shape[1]
    gq = n_heads // N_KV_HEADS
    w_rows = cache_swa_k.shape[2]
    d_kv = N_KV_HEADS * HEAD_DIM
    conv_w = conv_w_dw.shape[1]
    ffn_w = ffn_w_conv.shape[1]
    assert tq == SUBLANES and w_rows == WINDOW and seq % WINDOW == 0 and n_meta <= WINDOW
    assert conv_w - 1 <= CONV_HALO and ffn_w - 1 <= FFN_HALO

    tre, tim, kst, bst, mst = _ssm_prep(ssm_lambda_re, ssm_lambda_im, ssm_log_step,
                                        ssm_b_re, ssm_b_im, ssm_c_re, ssm_c_im)
    bias = _bias_table(rel_bias)
    row3 = lambda a: a.reshape(a.shape[0], 1, a.shape[-1]).astype(F32)
    sinks = attn_sinks.astype(F32).reshape(n_even, N_KV_HEADS, gq, 1)
    wts = {
        "g_mix": row3(g_mix), "g_ffn": row3(g_ffn), "g_final": g_final.reshape(1, dm).astype(F32),
        "w_in": w_in_mix.astype(BF16), "kst": kst, "bst": bst, "mst": mst, "tre": tre, "tim": tim,
        "ssm_d": row3(ssm_d), "w_glu": ssm_w_glu.astype(BF16), "b_glu": row3(ssm_b_glu),
        "bias_p": bias.reshape(N_KV_HEADS, gq * WINDOW, 2 * WINDOW),
        "bias_s": bias[:, :tq, :w_rows + tq].reshape(N_KV_HEADS, gq * tq, w_rows + tq),
        "sink_p": jnp.broadcast_to(sinks[:, :, :, None, :], (n_even, N_KV_HEADS, gq, WINDOW, 1)
                                   ).reshape(n_even, N_KV_HEADS, gq * WINDOW, 1),
        "sink_s": jnp.broadcast_to(sinks[:, :, :, None, :], (n_even, N_KV_HEADS, gq, tq, 1)
                                   ).reshape(n_even, N_KV_HEADS, gq * tq, 1),
        "w_out": w_out_mix.astype(BF16),
        "w_pw1": conv_w_pw1.astype(BF16), "w_dw": jnp.broadcast_to(conv_w_dw.astype(F32)[:, :, None, :],
                                 conv_w_dw.shape[:2] + (SUBLANES, conv_w_dw.shape[2])), "b_dw": row3(conv_b_dw),
        "ln_g": row3(conv_ln_g), "ln_b": row3(conv_ln_b), "w_pw2": conv_w_pw2.astype(BF16),
        "w_up": ffn_w_up.astype(BF16), "w_cv": ffn_w_conv.astype(F32), "b_cv": row3(ffn_b_conv),
        "w_dn": ffn_w_down.astype(BF16),
    }

    pad_rows = WINDOW - n_meta
    tp = pad_rows + n_meta + seq
    tm = _pick_tile(tp, PROMPT_TILE_TARGET)
    xp = jnp.concatenate([jnp.zeros((nb, pad_rows, dm), F32),
                          jnp.broadcast_to(meta_tokens.astype(F32)[None], (nb, n_meta, dm)),
                          x_prompt.astype(F32)], axis=1)
    xs = x_sample.astype(F32).reshape(n_seq * tq, dm)
    ns = min(SAMPLE_SEQS, n_seq)
    assert n_seq % ns == 0

    sr_p, si_p, k_p, v_p, c_p, f_p = [], [], [], [], [], []
    sr_s, si_s, k_s, v_s, c_s, f_s = [], [], [], [], [], []
    for layer in range(depth):
        idx = layer // 2
        if layer % 2 == 0:
            xp, sre, sim, ko, vo = _mixer_prompt(xp, idx, layer, wts, tm, pad_rows)
            sr_p.append(sre.reshape(nb, g_ssm, p_ssm))
            si_p.append(sim.reshape(nb, g_ssm, p_ssm))
            k_p.append(ko.reshape(nb, WINDOW, N_KV_HEADS, HEAD_DIM))
            v_p.append(vo.reshape(nb, WINDOW, N_KV_HEADS, HEAD_DIM))
            xs, sre, sim, ko, vo = _mixer_sample(
                xs, state_ssm_re[idx].astype(F32).reshape(n_seq, g_ssm * p_ssm),
                state_ssm_im[idx].astype(F32).reshape(n_seq, g_ssm * p_ssm),
                cache_swa_k[idx].astype(F32).reshape(n_seq, w_rows, d_kv),
                cache_swa_v[idx].astype(F32).reshape(n_seq, w_rows, d_kv), idx, layer, wts, ns)
            sr_s.append(sre.reshape(n_seq, g_ssm, p_ssm))
            si_s.append(sim.reshape(n_seq, g_ssm, p_ssm))
            k_s.append(ko.reshape(n_seq, w_rows, N_KV_HEADS, HEAD_DIM))
            v_s.append(vo.reshape(n_seq, w_rows, N_KV_HEADS, HEAD_DIM))
        else:
            xp, st = _conformer(xp, None, idx, layer, wts, tm, pad_rows)
            c_p.append(st)
            past = jnp.pad(state_conv[idx].astype(F32), ((0, 0), (CONV_HALO - (conv_w - 1), 0), (0, 0)))
            xs, st = _conformer(xs, past, idx, layer, wts, ns, 0)
            c_s.append(st)
        last = layer == depth - 1
        xp, st = _ffn(xp, None, layer, wts, tm, last)
        f_p.append(st)
        past = jnp.pad(state_ffn[layer].astype(F32), ((0, 0), (FFN_HALO - (ffn_w - 1), 0), (0, 0)))
        xs, st = _ffn(xs, past, layer, wts, ns, last)
        f_s.append(st)

    yp = xp[:, pad_rows + n_meta:]
    ys = xs.reshape(n_seq, tq, dm)
    st = jnp.stack
    return (yp, ys, st(sr_p), st(si_p), st(k_p), st(v_p), st(c_p), st(f_p),
            st(sr_s), st(si_s), st(k_s), st(v_s), st(c_s), st(f_s))
```

```python
import functools
import math

import numpy as np
import jax
import jax.numpy as jnp
from jax import lax
from jax.experimental import pallas as pl
from jax.experimental.pallas import tpu as pltpu

F32 = jnp.float32
BF16 = jnp.bfloat16

EPS = 1e-6
NEG_INF = -1e30
WINDOW = 128
HEAD_DIM = 64
N_KV_HEADS = 2
N_BUCKETS = 32
MAX_DISTANCE = 128
SSM_GROUP = 16
SSM_STATE = 64
SSM_BLK_GROUPS = 8
SUBLANES = 8
CONV_HALO = 32
FFN_HALO = 8
CONV_ROWS = 32
PROMPT_TILE_TARGET = 768
SAMPLE_SEQS = 32
VMEM_LIMIT = 56 * 1024 * 1024


def _cparams(sem):
    return pltpu.CompilerParams(dimension_semantics=sem, vmem_limit_bytes=VMEM_LIMIT)


def _const_spec(shape):
    nd = len(shape)
    return pl.BlockSpec(shape, lambda *_: (0,) * nd, pipeline_mode=pl.Buffered(1))


def _rmsnorm(x, g):
    return x * lax.rsqrt(jnp.mean(x * x, axis=-1, keepdims=True) + EPS) * g


def _dot(a, b):
    return jnp.dot(a, b, preferred_element_type=F32)


def _ssm_prep_kernel(lr_ref, li_ref, ls_ref, bre_ref, bim_ref, cre_ref, cim_ref,
                     tre_ref, tim_ref, kst_ref, bst_ref, mst_ref):
    lr = lr_ref[0]
    li = li_ref[0]
    dt = jnp.exp(ls_ref[0])
    decay = jnp.exp(lr * dt)
    a_re = decay * jnp.cos(li * dt)
    a_im = decay * jnp.sin(li * dt)
    den = lr * lr + li * li
    num_re = a_re - 1.0
    coef_re = (num_re * lr + a_im * li) / den
    coef_im = (a_im * lr - num_re * li) / den

    cmul = lambda x, y: (x[0] * y[0] - x[1] * y[1], x[0] * y[1] + x[1] * y[0])
    n = lr.shape[-1]
    pw = [(jnp.ones((1, n), F32), jnp.zeros((1, n), F32)), (a_re, a_im)]
    for _ in range(SUBLANES - 1):
        pw.append(cmul(pw[-1], (a_re, a_im)))
    apw = [pw[SUBLANES]]
    for _ in range(SUBLANES - 1):
        apw.append(cmul(apw[-1], pw[SUBLANES]))
    row = lax.broadcasted_iota(jnp.int32, (SUBLANES, n), 0)
    zero = jnp.zeros((SUBLANES, n), F32)
    for t, k in enumerate((1, 2, 4)):
        tre_ref[0, t] = jnp.where(row >= k, jnp.broadcast_to(apw[k - 1][0], (SUBLANES, n)), zero)
        tim_ref[0, t] = jnp.where(row >= k, jnp.broadcast_to(apw[k - 1][1], (SUBLANES, n)), zero)
    ap_re, ap_im = zero, zero
    for j in range(SUBLANES):
        ap_re = jnp.where(row == j, jnp.broadcast_to(apw[j][0], (SUBLANES, n)), ap_re)
        ap_im = jnp.where(row == j, jnp.broadcast_to(apw[j][1], (SUBLANES, n)), ap_im)
    tre_ref[0, 3] = ap_re
    tim_ref[0, 3] = ap_im

    bre = bre_ref[0]
    bim = bim_ref[0]
    bbar = (coef_re * bre - coef_im * bim, coef_re * bim + coef_im * bre)
    rows = SSM_BLK_GROUPS * SSM_GROUP
    r_i = lax.broadcasted_iota(jnp.int32, (rows, n), 0)
    c_i = lax.broadcasted_iota(jnp.int32, (rows, n), 1)
    sh = lambda v, d: lax.shift_right_logical(v, jnp.full(v.shape, int(math.log2(d)), jnp.int32))
    diag = sh(r_i, SSM_GROUP) == (sh(c_i, SSM_STATE) & (SSM_BLK_GROUPS - 1))
    zer = jnp.zeros((rows, n), F32)
    blockdiag = lambda v: jnp.where(diag, jnp.concatenate([v] * SSM_BLK_GROUPS, axis=0), zer)
    bb = (blockdiag(bbar[0]), blockdiag(bbar[1]))
    cc = (blockdiag(cre_ref[0]), blockdiag(cim_ref[0]))
    w = SSM_BLK_GROUPS * SSM_STATE
    nblk = n // w
    for d in range(SUBLANES):
        ba = cmul(bb, pw[d])
        ca = cmul(cc, pw[d + 1])
        for blk in range(nblk):
            sl = slice(blk * w, (blk + 1) * w)
            rs = slice(d * rows, (d + 1) * rows)
            bcat = jnp.concatenate([ba[0][:, sl], ba[1][:, sl]], axis=1)
            ccat = jnp.concatenate([cc[0][:, sl], -cc[1][:, sl]], axis=1)
            bst_ref[0, blk, rs, :] = bcat.astype(BF16)
            kst_ref[0, blk, rs, :] = lax.dot_general(
                bcat, ccat, (((1,), (1,)), ((), ())), precision=lax.Precision.HIGHEST,
                preferred_element_type=F32).astype(BF16)
            mcat = jnp.concatenate([ca[0][:, sl], -ca[1][:, sl]], axis=1)
            mst_ref[0, blk, :, rs] = mcat.T.astype(BF16)


def _ssm_prep(lam_re, lam_im, log_step, b_re, b_im, c_re, c_im):
    ne, g, p = lam_re.shape
    n = g * p
    assert g % SSM_BLK_GROUPS == 0 and p == SSM_STATE and b_re.shape[-1] == SSM_GROUP
    nblk = g // SSM_BLK_GROUPS
    rows = SSM_BLK_GROUPS * SSM_GROUP
    w = SSM_BLK_GROUPS * SSM_STATE
    flat = lambda a: a.reshape(ne, 1, n)
    ls = jnp.broadcast_to(log_step[:, :, None], (ne, g, p))
    bt = lambda a: jnp.transpose(a, (0, 3, 1, 2)).reshape(ne, SSM_GROUP, n)
    ct = lambda a: jnp.transpose(a, (0, 2, 1, 3)).reshape(ne, SSM_GROUP, n)
    vec = pl.BlockSpec((1, 1, n), lambda i: (i, 0, 0))
    mat = pl.BlockSpec((1, SSM_GROUP, n), lambda i: (i, 0, 0))
    tab = pl.BlockSpec((1, 4, SUBLANES, n), lambda i: (i, 0, 0, 0))
    stk = lambda cols: pl.BlockSpec((1, nblk, SUBLANES * rows, cols), lambda i: (i, 0, 0, 0))
    mspec = pl.BlockSpec((1, nblk, 2 * w, SUBLANES * rows), lambda i: (i, 0, 0, 0))
    return pl.pallas_call(
        _ssm_prep_kernel,
        grid=(ne,),
        in_specs=[vec, vec, vec, mat, mat, mat, mat],
        out_specs=[tab, tab, stk(rows), stk(2 * w), mspec],
        out_shape=[jax.ShapeDtypeStruct((ne, 4, SUBLANES, n), F32)] * 2
        + [jax.ShapeDtypeStruct((ne, nblk, SUBLANES * rows, rows), BF16),
           jax.ShapeDtypeStruct((ne, nblk, SUBLANES * rows, 2 * w), BF16),
           jax.ShapeDtypeStruct((ne, nblk, 2 * w, SUBLANES * rows), BF16)],
        compiler_params=pltpu.CompilerParams(vmem_limit_bytes=VMEM_LIMIT),
        name="ssm_prep",
    )(flat(lam_re), flat(lam_im), flat(ls), bt(b_re), bt(b_im), ct(c_re), ct(c_im))


def _bucket_table():
    i = np.arange(WINDOW)[:, None]
    j = np.arange(2 * WINDOW)[None, :]
    dist = WINDOW + i - j
    nn = np.maximum(dist, 0)
    max_exact = N_BUCKETS // 2
    nf = np.maximum(nn, max_exact).astype(np.float32)
    large = max_exact + (np.log(nf / np.float32(max_exact)) / np.float32(math.log(MAX_DISTANCE / max_exact))
                         * np.float32(N_BUCKETS - max_exact)).astype(np.int32)
    large = np.minimum(large, N_BUCKETS - 1)
    bucket = np.where(nn < max_exact, nn, large)
    valid = (dist >= 0) & (dist < WINDOW)
    return np.where(valid, bucket, -1).astype(np.int32)


def _bias_kernel(rb_ref, bucket_ref, out_ref):
    bucket = bucket_ref[...]
    nh = out_ref.shape[0]
    for h in range(nh):
        acc = jnp.full(bucket.shape, NEG_INF, F32)
        for b in range(N_BUCKETS):
            acc = jnp.where(bucket == b, rb_ref[b, h], acc)
        out_ref[h] = acc


def _bias_table(rel_bias):
    nh = rel_bias.shape[1]
    bucket = jnp.asarray(_bucket_table())
    return pl.pallas_call(
        _bias_kernel,
        in_specs=[pl.BlockSpec(memory_space=pltpu.SMEM),
                  pl.BlockSpec(bucket.shape, lambda: (0, 0))],
        out_specs=pl.BlockSpec((nh,) + bucket.shape, lambda: (0, 0, 0)),
        out_shape=jax.ShapeDtypeStruct((nh,) + bucket.shape, F32),
        name="rel_bias_table",
    )(rel_bias.astype(F32), bucket)


def _block_state_scan(sbuf, tre_ref, tim_ref, lane0, w, carry):
    lanes = pl.ds(lane0, w)
    first = lax.broadcasted_iota(jnp.int32, (SUBLANES, w), 0) == 0

    def body(b, carry):
        hr, hi = carry
        r0 = pl.multiple_of(b * SUBLANES, SUBLANES)
        xr = sbuf[pl.ds(r0, SUBLANES), 0:w]
        xi = sbuf[pl.ds(r0, SUBLANES), w:2 * w]
        for t, k in enumerate((1, 2, 4)):
            cr = tre_ref[0, t, :, lanes]
            ci = tim_ref[0, t, :, lanes]
            rr = pltpu.roll(xr, k, 0)
            ri = pltpu.roll(xi, k, 0)
            xr, xi = xr + cr * rr - ci * ri, xi + cr * ri + ci * rr
        pr = tre_ref[0, 3, :, lanes]
        pi = tim_ref[0, 3, :, lanes]
        xr, xi = xr + pr * hr - pi * hi, xi + pr * hi + pi * hr
        sbuf[pl.ds(r0, SUBLANES), 0:w] = jnp.where(first, hr, pltpu.roll(xr, 1, 0))
        sbuf[pl.ds(r0, SUBLANES), w:2 * w] = jnp.where(first, hi, pltpu.roll(xi, 1, 0))
        return (jnp.broadcast_to(xr[SUBLANES - 1:SUBLANES, :], (SUBLANES, w)),
                jnp.broadcast_to(xi[SUBLANES - 1:SUBLANES, :], (SUBLANES, w)))

    return lax.fori_loop(0, sbuf.shape[0] // SUBLANES, body, carry)


def _s5_mix(u, uext, yint, kst_ref, bst_ref, mst_ref, d, wglu_ref, bglu, advance):
    nblk = kst_ref.shape[1]
    cw = kst_ref.shape[3]
    rows = u.shape[0]
    nb = rows // SUBLANES
    j_i = lax.broadcasted_iota(jnp.int32, (1, SUBLANES, 1), 1)
    ys = []
    for blk in range(nblk):
        ub = u[:, blk * cw:(blk + 1) * cw]
        u3 = ub.reshape(nb, SUBLANES, cw)
        parts = [ub]
        for dd in range(1, SUBLANES):
            parts.append(jnp.where(j_i >= dd, pltpu.roll(u3, dd, 1), 0.0).reshape(rows, cw))
        for dd in range(SUBLANES):
            uext[dd] = parts[dd]
        y_intra = _dot(jnp.concatenate(parts, axis=1).astype(BF16), kst_ref[0, blk])
        last = [uext[dd, pl.ds(SUBLANES - 1, nb, stride=SUBLANES), :] for dd in range(SUBLANES)]
        s = _dot(jnp.concatenate(last, axis=1).astype(BF16), bst_ref[0, blk])
        hprev = advance(blk, s)
        yi = _dot(hprev.astype(BF16), mst_ref[0, blk])
        for j in range(SUBLANES):
            yint[pl.ds(j, nb, stride=SUBLANES), :] = yi[:, j * cw:(j + 1) * cw]
        ys.append(y_intra + yint[...])
    y = jnp.concatenate(ys, axis=1) + d * u
    g = jax.nn.gelu(y)
    return g * jax.nn.sigmoid(_dot(g.astype(BF16), wglu_ref[0]) + bglu)


def _mixer_prompt_kernel(x_ref, g_ref, wukv_ref, wqt_ref, wvt_ref, kst_ref, bst_ref, mst_ref, tre_ref, tim_ref,
                         d_ref, wglu_ref, bglu_ref, bias_ref, sink_ref, wout_ref,
                         o_ref, sre_ref, sim_ref, ko_ref, vo_ref,
                         uext, yint, sbuf, hst, kbuf, vtbuf, mix, sc, pb):
    t = pl.program_id(1)
    tm = x_ref.shape[1]
    d_ssm = d_ref.shape[-1]
    d_kv = N_KV_HEADS * HEAD_DIM
    gq = bias_ref.shape[3] // WINDOW
    nt_dims = (((1,), (1,)), ((), ()))

    @pl.when(t == 0)
    def _():
        hst[...] = jnp.zeros_like(hst)
        kbuf[0:WINDOW, :] = jnp.zeros((WINDOW, d_kv), F32)
        vtbuf[:, 0:WINDOW] = jnp.zeros((d_kv, WINDOW), F32)

    x = x_ref[0]
    h = _rmsnorm(x, g_ref[0]).astype(BF16)
    z = _dot(h, wukv_ref[0])
    u = z[:, :d_ssm]
    kbuf[WINDOW:, :] = z[:, d_ssm:d_ssm + d_kv]
    v = z[:, d_ssm + d_kv:]
    qt = lax.dot_general(wqt_ref[0], h, nt_dims, preferred_element_type=F32) * (HEAD_DIM ** -0.5)
    vtbuf[:, WINDOW:] = lax.dot_general(wvt_ref[0], h, nt_dims, preferred_element_type=F32)

    n_state = hst.shape[1] // 2

    w = sbuf.shape[1] // 2

    def advance(blk, s):
        lane0 = blk * w
        sbuf[...] = s
        init = (hst[:, lane0:lane0 + w], hst[:, n_state + lane0:n_state + lane0 + w])
        hr, hi = _block_state_scan(sbuf, tre_ref, tim_ref, lane0, w, init)
        hst[:, lane0:lane0 + w] = hr
        hst[:, n_state + lane0:n_state + lane0 + w] = hi
        return sbuf[...]

    ya = _s5_mix(u, uext, yint, kst_ref, bst_ref, mst_ref, d_ref[0], wglu_ref, bglu_ref[0], advance)
    mix[:, 0:d_ssm] = ya.astype(BF16)
    sre_ref[0] = hst[0:1, 0:n_state]
    sim_ref[0] = hst[0:1, n_state:]

    nqb = tm // WINDOW
    for qb in range(nqb):
        c0 = qb * WINDOW
        var = jnp.where(t == 0, qb + 1, 0) if qb + 1 < bias_ref.shape[0] else 0
        for j in range(N_KV_HEADS):
            kb = kbuf[c0:c0 + 2 * WINDOW, j * HEAD_DIM:(j + 1) * HEAD_DIM].astype(BF16)
            qc = jnp.concatenate(
                [qt[(j * gq + g) * HEAD_DIM:(j * gq + g + 1) * HEAD_DIM, c0:c0 + WINDOW] for g in range(gq)],
                axis=1).astype(BF16)
            sc[j, qb] = _dot(kb, qc) + bias_ref[var, j]
    dens = []
    for j in range(N_KV_HEADS):
        s = sc[j]
        sink = sink_ref[0, j][None]
        m = jnp.maximum(jnp.max(s, axis=1, keepdims=True), sink)
        p = jnp.exp(s - m)
        dens.append(jnp.sum(p, axis=1, keepdims=True) + jnp.exp(sink - m))
        pb[j] = p.astype(BF16)
    for qb in range(nqb):
        c0 = qb * WINDOW
        pieces = []
        for j in range(N_KV_HEADS):
            vb = vtbuf[j * HEAD_DIM:(j + 1) * HEAD_DIM, c0:c0 + 2 * WINDOW].astype(BF16)
            o = _dot(vb, pb[j, qb]) / dens[j][qb]
            pieces += [o[:, g * WINDOW:(g + 1) * WINDOW] for g in range(gq)]
        ot = jnp.concatenate(pieces, axis=0)
        mix[c0:c0 + WINDOW, d_ssm:] = ot.T.astype(BF16)

    kbuf[0:WINDOW, :] = kbuf[tm:tm + WINDOW, :]
    vtbuf[:, 0:WINDOW] = vtbuf[:, tm:tm + WINDOW]
    ko_ref[0] = kbuf[0:WINDOW, :]
    vo_ref[0] = v[tm - WINDOW:, :]
    o_ref[0] = x + _dot(mix[...], wout_ref[0])


def _mixer_prompt(x, idx, layer, wts, tm):
    nb, tp, dm = x.shape
    nt = tp // tm
    n_state = wts["tre"].shape[-1]
    d_ssm = wts["ssm_d"].shape[-1]
    d_kv = N_KV_HEADS * HEAD_DIM
    w = wts["bst"].shape[-1] // 2
    cw = wts["kst"].shape[-1]
    layer_spec = lambda a, i: pl.BlockSpec((1,) + a.shape[1:], lambda b, t: (i,) + (0,) * (a.ndim - 1),
                                           pipeline_mode=pl.Buffered(1))
    ins = [
        (x, pl.BlockSpec((1, tm, dm), lambda b, t: (b, t, 0))),
        (wts["g_mix"], layer_spec(wts["g_mix"], layer)),
        (wts["w_ukv"], layer_spec(wts["w_ukv"], idx)),
        (wts["w_qt"], layer_spec(wts["w_qt"], idx)),
        (wts["w_vt"], layer_spec(wts["w_vt"], idx)),
        (wts["kst"], layer_spec(wts["kst"], idx)),
        (wts["bst"], layer_spec(wts["bst"], idx)),
        (wts["mst"], layer_spec(wts["mst"], idx)),
        (wts["tre"], layer_spec(wts["tre"], idx)),
        (wts["tim"], layer_spec(wts["tim"], idx)),
        (wts["ssm_d"], layer_spec(wts["ssm_d"], idx)),
        (wts["w_glu"], layer_spec(wts["w_glu"], idx)),
        (wts["b_glu"], layer_spec(wts["b_glu"], idx)),
        (wts["bias_p"], _const_spec(wts["bias_p"].shape)),
        (wts["sink_p"], layer_spec(wts["sink_p"], idx)),
        (wts["w_out"], layer_spec(wts["w_out"], idx)),
    ]
    out_shape = [
        jax.ShapeDtypeStruct((nb, tp, dm), F32),
        jax.ShapeDtypeStruct((nb, 1, n_state), F32),
        jax.ShapeDtypeStruct((nb, 1, n_state), F32),
        jax.ShapeDtypeStruct((nb, WINDOW, d_kv), F32),
        jax.ShapeDtypeStruct((nb, WINDOW, d_kv), F32),
    ]
    out_specs = [
        pl.BlockSpec((1, tm, dm), lambda b, t: (b, t, 0)),
        pl.BlockSpec((1, 1, n_state), lambda b, t: (b, 0, 0)),
        pl.BlockSpec((1, 1, n_state), lambda b, t: (b, 0, 0)),
        pl.BlockSpec((1, WINDOW, d_kv), lambda b, t: (b, 0, 0)),
        pl.BlockSpec((1, WINDOW, d_kv), lambda b, t: (b, 0, 0)),
    ]
    scratch = [
        pltpu.VMEM((SUBLANES, tm, cw), F32),
        pltpu.VMEM((tm, cw), F32),
        pltpu.VMEM((tm // SUBLANES, 2 * w), F32),
        pltpu.VMEM((SUBLANES, 2 * n_state), F32),
        pltpu.VMEM((tm + WINDOW, d_kv), F32),
        pltpu.VMEM((d_kv, tm + WINDOW), F32),
        pltpu.VMEM((tm, dm), BF16),
        pltpu.VMEM((N_KV_HEADS, tm // WINDOW) + wts["bias_p"].shape[2:], F32),
        pltpu.VMEM((N_KV_HEADS, tm // WINDOW) + wts["bias_p"].shape[2:], BF16),
    ]
    assert tm // WINDOW >= wts["bias_p"].shape[0] - 1
    return pl.pallas_call(
        _mixer_prompt_kernel,
        grid=(nb, nt),
        in_specs=[s for _, s in ins],
        out_specs=out_specs,
        out_shape=out_shape,
        scratch_shapes=scratch,
        compiler_params=_cparams(("arbitrary", "arbitrary")),
        name=f"mixer_prompt_l{layer}",
    )(*[a for a, _ in ins])


def _mixer_sample_kernel(x_ref, h0r_ref, h0i_ref, kc_ref, vc_ref, g_ref, win_ref, kst_ref, bst_ref, mst_ref,
                         tre_ref, tim_ref, d_ref, wglu_ref, bglu_ref, bias_ref, sink_ref, wout_ref,
                         o_ref, sre_ref, sim_ref, ko_ref, vo_ref, uext, yint):
    rows = x_ref.shape[0]
    ns = kc_ref.shape[0]
    tq = rows // ns
    assert tq == SUBLANES
    w_rows = kc_ref.shape[1]
    d_ssm = d_ref.shape[-1]
    gq = bias_ref.shape[1] // tq
    d_q = N_KV_HEADS * gq * HEAD_DIM
    d_kv = N_KV_HEADS * HEAD_DIM

    x = x_ref[...]
    h = _rmsnorm(x, g_ref[0]).astype(BF16)
    z = _dot(h, win_ref[0])
    u = z[:, :d_ssm]
    q = (z[:, d_ssm:d_ssm + d_q] * (HEAD_DIM ** -0.5)).reshape(ns, tq, d_q)
    kn = z[:, d_ssm + d_q:d_ssm + d_q + d_kv].reshape(ns, tq, d_kv)
    vn = z[:, d_ssm + d_q + d_kv:].reshape(ns, tq, d_kv)

    w = bst_ref.shape[-1] // 2

    def advance(blk, s):
        lanes = slice(blk * w, (blk + 1) * w)
        hr = h0r_ref[:, lanes]
        hi = h0i_ref[:, lanes]
        ar = tre_ref[0, 0, SUBLANES - 1:SUBLANES, lanes]
        ai = tim_ref[0, 0, SUBLANES - 1:SUBLANES, lanes]
        sre_ref[:, lanes] = ar * hr - ai * hi + s[:, 0:w]
        sim_ref[:, lanes] = ar * hi + ai * hr + s[:, w:2 * w]
        return jnp.concatenate([hr, hi], axis=1)

    ya = _s5_mix(u, uext, yint, kst_ref, bst_ref, mst_ref, d_ref[0], wglu_ref, bglu_ref[0], advance)

    kc = jnp.concatenate([kc_ref[...], kn], axis=1)
    vc = jnp.concatenate([vc_ref[...], vn], axis=1)
    ko_ref[...] = kc[:, tq:, :]
    vo_ref[...] = vc[:, tq:, :]
    pieces = []
    for j in range(N_KV_HEADS):
        kb = kc[:, :, j * HEAD_DIM:(j + 1) * HEAD_DIM].astype(BF16)
        vb = vc[:, :, j * HEAD_DIM:(j + 1) * HEAD_DIM].astype(BF16)
        qs = jnp.concatenate(
            [q[:, :, (j * gq + g) * HEAD_DIM:(j * gq + g + 1) * HEAD_DIM] for g in range(gq)],
            axis=1).astype(BF16)
        s = jnp.einsum("nqd,nkd->nqk", qs, kb, preferred_element_type=F32) + bias_ref[j]
        sink = sink_ref[0, j]
        m = jnp.maximum(jnp.max(s, axis=-1, keepdims=True), sink)
        p = jnp.exp(s - m)
        l = jnp.sum(p, axis=-1, keepdims=True) + jnp.exp(sink - m)
        o = jnp.einsum("nqk,nkd->nqd", p.astype(BF16), vb, preferred_element_type=F32) / l
        pieces += [o[:, g * tq:(g + 1) * tq, :] for g in range(gq)]
    yb = jnp.concatenate(pieces, axis=2).reshape(rows, d_q)
    mixed = jnp.concatenate([ya, yb], axis=1).astype(BF16)
    o_ref[...] = x + _dot(mixed, wout_ref[0])


def _mixer_sample(x2, h0r, h0i, kc, vc, idx, layer, wts, ns):
    rows_all, dm = x2.shape
    n_seq, w_rows, d_kv = kc.shape
    tq = rows_all // n_seq
    n_state = h0r.shape[-1]
    cw = wts["kst"].shape[-1]
    layer_spec = lambda a, i: pl.BlockSpec((1,) + a.shape[1:], lambda s: (i,) + (0,) * (a.ndim - 1),
                                           pipeline_mode=pl.Buffered(1))
    row_spec = pl.BlockSpec((ns * tq, dm), lambda s: (s, 0))
    st_spec = pl.BlockSpec((ns, n_state), lambda s: (s, 0))
    kv_spec = pl.BlockSpec((ns, w_rows, d_kv), lambda s: (s, 0, 0))
    ins = [
        (x2, row_spec), (h0r, st_spec), (h0i, st_spec), (kc, kv_spec), (vc, kv_spec),
        (wts["g_mix"], layer_spec(wts["g_mix"], layer)),
        (wts["w_in"], layer_spec(wts["w_in"], idx)),
        (wts["kst"], layer_spec(wts["kst"], idx)),
        (wts["bst"], layer_spec(wts["bst"], idx)),
        (wts["mst"], layer_spec(wts["mst"], idx)),
        (wts["tre"], layer_spec(wts["tre"], idx)),
        (wts["tim"], layer_spec(wts["tim"], idx)),
        (wts["ssm_d"], layer_spec(wts["ssm_d"], idx)),
        (wts["w_glu"], layer_spec(wts["w_glu"], idx)),
        (wts["b_glu"], layer_spec(wts["b_glu"], idx)),
        (wts["bias_s"], _const_spec(wts["bias_s"].shape)),
        (wts["sink_s"], layer_spec(wts["sink_s"], idx)),
        (wts["w_out"], layer_spec(wts["w_out"], idx)),
    ]
    return pl.pallas_call(
        _mixer_sample_kernel,
        grid=(n_seq // ns,),
        in_specs=[s for _, s in ins],
        out_specs=[row_spec, st_spec, st_spec, kv_spec, kv_spec],
        out_shape=[jax.ShapeDtypeStruct(x2.shape, F32), jax.ShapeDtypeStruct(h0r.shape, F32),
                   jax.ShapeDtypeStruct(h0i.shape, F32), jax.ShapeDtypeStruct(kc.shape, F32),
                   jax.ShapeDtypeStruct(vc.shape, F32)],
        scratch_shapes=[pltpu.VMEM((SUBLANES, ns * tq, cw), F32), pltpu.VMEM((ns * tq, cw), F32)],
        compiler_params=_cparams(("arbitrary",)),
        name=f"mixer_sample_l{layer}",
    )(*[a for a, _ in ins])


def _conformer_kernel(*refs, carry, pad_rows, width):
    if carry:
        (x_ref, g_ref, w1_ref, wdw_ref, bdw_ref, lng_ref, lnb_ref, w2_ref, o_ref, st_ref, ext, ybuf,
         stage) = refs
        past_ref = None
        t = pl.program_id(1)
        x = x_ref[0]
    else:
        (x_ref, past_ref, g_ref, w1_ref, wdw_ref, bdw_ref, lng_ref, lnb_ref, w2_ref, o_ref, st_ref,
         ext, ybuf, stage) = refs
        x = x_ref[...]
    ns, le, c = ext.shape
    l = le - CONV_HALO
    rows = ns * l
    off = CONV_HALO - (width - 1)

    if carry:
        @pl.when(t == 0)
        def _():
            ext[:, 0:CONV_HALO, :] = jnp.zeros((ns, CONV_HALO, c), F32)
    else:
        ext[:, 0:CONV_HALO, :] = past_ref[...]

    h = _rmsnorm(x, g_ref[0]).astype(BF16)
    z = _dot(h, w1_ref[0])
    gl = z[:, :c] * jax.nn.sigmoid(z[:, c:])
    ext[:, CONV_HALO:, :] = gl.reshape(ns, l, c)
    st_ref[...] = ext[:, l + off:, :].reshape(st_ref.shape)

    def conv_unit(n, r0, rc, stg):
        win = ext[n, pl.ds(r0, rc + CONV_HALO), :]
        for s in range(1, SUBLANES):
            stg[s - 1] = win[s:s + rc + CONV_HALO - SUBLANES]
        acc = jnp.zeros((rc // SUBLANES, SUBLANES, c), F32)
        for k in range(width):
            a, s = divmod(off + k, SUBLANES)
            if s == 0:
                xk = ext[n, pl.ds(r0 + a * SUBLANES, rc), :]
            else:
                xk = stg[s - 1, a * SUBLANES:a * SUBLANES + rc, :]
            acc = acc + wdw_ref[0, k][None] * xk.reshape(rc // SUBLANES, SUBLANES, c)
        return acc.reshape(rc, c)

    if carry:
        def chunk(i, _):
            for half in range(2):
                r0 = pl.multiple_of((2 * i + half) * CONV_ROWS, CONV_ROWS)
                ybuf[pl.ds(r0, CONV_ROWS), :] = conv_unit(0, r0, CONV_ROWS, stage.at[half])
            return 0
        assert l % (2 * CONV_ROWS) == 0
        lax.fori_loop(0, l // (2 * CONV_ROWS), chunk, 0)
    else:
        def seq(i, _):
            for half in range(2):
                n = 2 * i + half
                ybuf[pl.ds(pl.multiple_of(n * l, l), l), :] = conv_unit(n, 0, l, stage.at[half])
            return 0
        assert ns % 2 == 0 and l % SUBLANES == 0
        lax.fori_loop(0, ns // 2, seq, 0)

    y = ybuf[...] + bdw_ref[0]
    mu = jnp.mean(y, axis=-1, keepdims=True)
    yc = y - mu
    var = jnp.mean(yc * yc, axis=-1, keepdims=True)
    y = yc * lax.rsqrt(var + EPS) * lng_ref[0] + lnb_ref[0]
    y = jax.nn.silu(y).astype(BF16)
    out = _dot(y, w2_ref[0])
    if carry:
        rid = t * rows + lax.broadcasted_iota(jnp.int32, (rows, 1), 0)
        out = jnp.where(rid >= pad_rows, out, 0.0)
        o_ref[0] = x + out
        ext[:, 0:CONV_HALO, :] = ext[:, l:l + CONV_HALO, :]
    else:
        o_ref[...] = x + out


def _conformer(x, past, idx, layer, wts, tile, pad_rows):
    carry = past is None
    width = wts["w_dw"].shape[1]
    c = wts["w_dw"].shape[3]
    if carry:
        nb, tp, dm = x.shape
        grid = (nb, tp // tile)
        lspec = lambda a, i: pl.BlockSpec((1,) + a.shape[1:], lambda b, t: (i,) + (0,) * (a.ndim - 1),
                                          pipeline_mode=pl.Buffered(1))
        xspec = pl.BlockSpec((1, tile, dm), lambda b, t: (b, t, 0))
        ins = [(x, xspec)]
        st_shape = (nb, width - 1, c)
        st_spec = pl.BlockSpec((1, width - 1, c), lambda b, t: (b, 0, 0))
        ns, l = 1, tile
        sem = ("arbitrary", "arbitrary")
    else:
        rows_all, dm = x.shape
        n_seq = past.shape[0]
        l = rows_all // n_seq
        ns = tile
        grid = (n_seq // ns,)
        lspec = lambda a, i: pl.BlockSpec((1,) + a.shape[1:], lambda s: (i,) + (0,) * (a.ndim - 1),
                                          pipeline_mode=pl.Buffered(1))
        xspec = pl.BlockSpec((ns * l, dm), lambda s: (s, 0))
        ins = [(x, xspec), (past, pl.BlockSpec((ns, CONV_HALO, c), lambda s: (s, 0, 0)))]
        st_shape = (n_seq, width - 1, c)
        st_spec = pl.BlockSpec((ns, width - 1, c), lambda s: (s, 0, 0))
        sem = ("arbitrary",)
    ins += [
        (wts["g_mix"], lspec(wts["g_mix"], layer)),
        (wts["w_pw1"], lspec(wts["w_pw1"], idx)),
        (wts["w_dw"], lspec(wts["w_dw"], idx)),
        (wts["b_dw"], lspec(wts["b_dw"], idx)),
        (wts["ln_g"], lspec(wts["ln_g"], idx)),
        (wts["ln_b"], lspec(wts["ln_b"], idx)),
        (wts["w_pw2"], lspec(wts["w_pw2"], idx)),
    ]
    return pl.pallas_call(
        functools.partial(_conformer_kernel, carry=carry, pad_rows=pad_rows, width=width),
        grid=grid,
        in_specs=[s for _, s in ins],
        out_specs=[xspec, st_spec],
        out_shape=[jax.ShapeDtypeStruct(x.shape, F32), jax.ShapeDtypeStruct(st_shape, F32)],
        scratch_shapes=[pltpu.VMEM((ns, l + CONV_HALO, c), F32), pltpu.VMEM((ns * l, c), F32)]
        + [pltpu.VMEM((2, SUBLANES - 1, (CONV_ROWS if carry else l) + CONV_HALO - SUBLANES, c), F32)],
        compiler_params=_cparams(sem),
        name=f"conformer_{'prompt' if carry else 'sample'}_l{layer}",
    )(*[a for a, _ in ins])


def _ffn_kernel(*refs, carry, final_norm, n_chunks):
    if carry:
        (x_ref, g_ref, wup_ref, wcv_ref, bcv_ref, wdn_ref, gf_ref, o_ref, st_ref, ext) = refs
        past_ref = None
        t = pl.program_id(1)
        x = x_ref[0]
    else:
        (x_ref, past_ref, g_ref, wup_ref, wcv_ref, bcv_ref, wdn_ref, gf_ref, o_ref, st_ref, ext) = refs
        x = x_ref[...]
    ns, le, dff = ext.shape
    l = le - FFN_HALO
    rows = ns * l
    kw = wcv_ref.shape[1]
    ch = dff // n_chunks

    if carry:
        @pl.when(t == 0)
        def _():
            ext[:, 0:FFN_HALO, :] = jnp.zeros((ns, FFN_HALO, dff), F32)
    else:
        ext[:, 0:FFN_HALO, :] = past_ref[...]

    h = _rmsnorm(x, g_ref[0]).astype(BF16)
    acc = x
    for cix in range(n_chunks):
        c0 = cix * ch
        gate = _dot(h, wup_ref[0, :, c0:c0 + ch])
        up = _dot(h, wup_ref[0, :, dff + c0:dff + c0 + ch])
        ext[:, FFN_HALO:, c0:c0 + ch] = gate.reshape(ns, l, ch)
        gc = bcv_ref[0, :, c0:c0 + ch]
        for k in range(kw):
            o = FFN_HALO - (kw - 1) + k
            gc = gc + wcv_ref[0, k:k + 1, c0:c0 + ch] * ext[:, o:o + l, c0:c0 + ch]
        y = (jax.nn.gelu(gc).reshape(rows, ch) * up).astype(BF16)
        acc = acc + _dot(y, wdn_ref[0, c0:c0 + ch, :])
    st_ref[...] = ext[:, le - (kw - 1):, :].reshape(st_ref.shape)
    if final_norm:
        acc = _rmsnorm(acc, gf_ref[...])
    if carry:
        o_ref[0] = acc
        ext[:, 0:FFN_HALO, :] = ext[:, l:l + FFN_HALO, :]
    else:
        o_ref[...] = acc


def _ffn(x, past, layer, wts, tile, final_norm):
    carry = past is None
    dff = wts["w_dn"].shape[1]
    kw = wts["w_cv"].shape[1]
    if carry:
        nb, tp, dm = x.shape
        grid = (nb, tp // tile)
        lspec = lambda a, i: pl.BlockSpec((1,) + a.shape[1:], lambda b, t: (i,) + (0,) * (a.ndim - 1),
                                          pipeline_mode=pl.Buffered(1))
        xspec = pl.BlockSpec((1, tile, dm), lambda b, t: (b, t, 0))
        ins = [(x, xspec)]
        st_shape = (nb, kw - 1, dff)
        st_spec = pl.BlockSpec((1, kw - 1, dff), lambda b, t: (b, 0, 0))
        ns, l = 1, tile
        sem = ("arbitrary", "arbitrary")
    else:
        rows_all, dm = x.shape
        n_seq = past.shape[0]
        l = rows_all // n_seq
        ns = tile
        grid = (n_seq // ns,)
        lspec = lambda a, i: pl.BlockSpec((1,) + a.shape[1:], lambda s: (i,) + (0,) * (a.ndim - 1),
                                          pipeline_mode=pl.Buffered(1))
        xspec = pl.BlockSpec((ns * l, dm), lambda s: (s, 0))
        ins = [(x, xspec), (past, pl.BlockSpec((ns, FFN_HALO, dff), lambda s: (s, 0, 0)))]
        st_shape = (n_seq, kw - 1, dff)
        st_spec = pl.BlockSpec((ns, kw - 1, dff), lambda s: (s, 0, 0))
        sem = ("arbitrary",)
    ins += [
        (wts["g_ffn"], lspec(wts["g_ffn"], layer)),
        (wts["w_up"], lspec(wts["w_up"], layer)),
        (wts["w_cv"], lspec(wts["w_cv"], layer)),
        (wts["b_cv"], lspec(wts["b_cv"], layer)),
        (wts["w_dn"], lspec(wts["w_dn"], layer)),
        (wts["g_final"], _const_spec(wts["g_final"].shape)),
    ]
    n_chunks = 2 if dff % 256 == 0 else 1
    return pl.pallas_call(
        functools.partial(_ffn_kernel, carry=carry, final_norm=final_norm, n_chunks=n_chunks),
        grid=grid,
        in_specs=[s for _, s in ins],
        out_specs=[xspec, st_spec],
        out_shape=[jax.ShapeDtypeStruct(x.shape, F32), jax.ShapeDtypeStruct(st_shape, F32)],
        scratch_shapes=[pltpu.VMEM((ns, l + FFN_HALO, dff), F32)],
        compiler_params=_cparams(sem),
        name=f"ffn_{'prompt' if carry else 'sample'}_l{layer}",
    )(*[a for a, _ in ins])


def _pick_tile(total, target):
    best = WINDOW
    for m in range(1, total // WINDOW + 1):
        if total % (m * WINDOW) == 0 and m * WINDOW <= target:
            best = m * WINDOW
    return best


def kernel(x_prompt, x_sample, state_ssm_re, state_ssm_im, cache_swa_k, cache_swa_v, state_conv, state_ffn,
           meta_tokens, g_mix, g_ffn, g_final, w_in_mix, ssm_lambda_re, ssm_lambda_im, ssm_log_step,
           ssm_b_re, ssm_b_im, ssm_c_re, ssm_c_im, ssm_d, ssm_w_glu, ssm_b_glu, rel_bias, attn_sinks,
           w_out_mix, conv_w_pw1, conv_w_dw, conv_b_dw, conv_ln_g, conv_ln_b, conv_w_pw2,
           ffn_w_up, ffn_w_conv, ffn_b_conv, ffn_w_down):
    nb, seq, dm = x_prompt.shape
    n_seq, tq, _ = x_sample.shape
    depth = g_mix.shape[0]
    n_meta = meta_tokens.shape[0]
    n_even, g_ssm, p_ssm = ssm_lambda_re.shape
    n_heads = rel_bias.shape[1]
    gq = n_heads // N_KV_HEADS
    w_rows = cache_swa_k.shape[2]
    d_kv = N_KV_HEADS * HEAD_DIM
    conv_w = conv_w_dw.shape[1]
    ffn_w = ffn_w_conv.shape[1]
    assert tq == SUBLANES and w_rows == WINDOW and seq % WINDOW == 0 and n_meta <= WINDOW
    assert conv_w - 1 <= CONV_HALO and ffn_w - 1 <= FFN_HALO

    tre, tim, kst, bst, mst = _ssm_prep(ssm_lambda_re, ssm_lambda_im, ssm_log_step,
                                        ssm_b_re, ssm_b_im, ssm_c_re, ssm_c_im)
    bias = _bias_table(rel_bias)
    row3 = lambda a: a.reshape(a.shape[0], 1, a.shape[-1]).astype(F32)
    sinks = attn_sinks.astype(F32).reshape(n_even, N_KV_HEADS, gq, 1)
    pad_rows = WINDOW - n_meta
    bias_t = jnp.transpose(bias.reshape(N_KV_HEADS, gq, WINDOW, 2 * WINDOW), (0, 3, 1, 2)
                           ).reshape(N_KV_HEADS, 2 * WINDOW, gq * WINDOW)
    key_i = jnp.arange(2 * WINDOW)[None, :, None]
    n_var = 1 + -(-(pad_rows + WINDOW) // WINDOW)
    bias_p = jnp.stack([bias_t] + [jnp.where(key_i >= pad_rows + WINDOW - qb * WINDOW, bias_t, NEG_INF)
                                   for qb in range(n_var - 1)])
    d_ssm = ssm_d.shape[-1]
    d_q = n_heads * HEAD_DIM
    w_in_bf = w_in_mix.astype(BF16)
    wts = {
        "g_mix": row3(g_mix), "g_ffn": row3(g_ffn), "g_final": g_final.reshape(1, dm).astype(F32),
        "w_in": w_in_bf, "kst": kst, "bst": bst, "mst": mst, "tre": tre, "tim": tim,
        "w_ukv": jnp.concatenate([w_in_bf[:, :, :d_ssm], w_in_bf[:, :, d_ssm + d_q:]], axis=-1),
        "w_qt": jnp.transpose(w_in_bf[:, :, d_ssm:d_ssm + d_q], (0, 2, 1)),
        "w_vt": jnp.transpose(w_in_bf[:, :, d_ssm + d_q + d_kv:], (0, 2, 1)),
        "ssm_d": row3(ssm_d), "w_glu": ssm_w_glu.astype(BF16), "b_glu": row3(ssm_b_glu),
        "bias_p": bias_p,
        "bias_s": bias[:, :tq, :w_rows + tq].reshape(N_KV_HEADS, gq * tq, w_rows + tq),
        "sink_p": jnp.broadcast_to(sinks[:, :, :, None, :], (n_even, N_KV_HEADS, gq, WINDOW, 1)
                                   ).reshape(n_even, N_KV_HEADS, 1, gq * WINDOW),
        "sink_s": jnp.broadcast_to(sinks[:, :, :, None, :], (n_even, N_KV_HEADS, gq, tq, 1)
                                   ).reshape(n_even, N_KV_HEADS, gq * tq, 1),
        "w_out": w_out_mix.astype(BF16),
        "w_pw1": conv_w_pw1.astype(BF16), "w_dw": jnp.broadcast_to(conv_w_dw.astype(F32)[:, :, None, :],
                                 conv_w_dw.shape[:2] + (SUBLANES, conv_w_dw.shape[2])), "b_dw": row3(conv_b_dw),
        "ln_g": row3(conv_ln_g), "ln_b": row3(conv_ln_b), "w_pw2": conv_w_pw2.astype(BF16),
        "w_up": ffn_w_up.astype(BF16), "w_cv": ffn_w_conv.astype(F32), "b_cv": row3(ffn_b_conv),
        "w_dn": ffn_w_down.astype(BF16),
    }

    tp = pad_rows + n_meta + seq
    tm = _pick_tile(tp, PROMPT_TILE_TARGET)
    xp = jnp.concatenate([jnp.zeros((nb, pad_rows, dm), F32),
                          jnp.broadcast_to(meta_tokens.astype(F32)[None], (nb, n_meta, dm)),
                          x_prompt.astype(F32)], axis=1)
    xs = x_sample.astype(F32).reshape(n_seq * tq, dm)
    ns = min(SAMPLE_SEQS, n_seq)
    assert n_seq % ns == 0

    sr_p, si_p, k_p, v_p, c_p, f_p = [], [], [], [], [], []
    sr_s, si_s, k_s, v_s, c_s, f_s = [], [], [], [], [], []
    for layer in range(depth):
        idx = layer // 2
        if layer % 2 == 0:
            xp, sre, sim, ko, vo = _mixer_prompt(xp, idx, layer, wts, tm)
            sr_p.append(sre.reshape(nb, g_ssm, p_ssm))
            si_p.append(sim.reshape(nb, g_ssm, p_ssm))
            k_p.append(ko.reshape(nb, WINDOW, N_KV_HEADS, HEAD_DIM))
            v_p.append(vo.reshape(nb, WINDOW, N_KV_HEADS, HEAD_DIM))
            xs, sre, sim, ko, vo = _mixer_sample(
                xs, state_ssm_re[idx].astype(F32).reshape(n_seq, g_ssm * p_ssm),
                state_ssm_im[idx].astype(F32).reshape(n_seq, g_ssm * p_ssm),
                cache_swa_k[idx].astype(F32).reshape(n_seq, w_rows, d_kv),
                cache_swa_v[idx].astype(F32).reshape(n_seq, w_rows, d_kv), idx, layer, wts, ns)
            sr_s.append(sre.reshape(n_seq, g_ssm, p_ssm))
            si_s.append(sim.reshape(n_seq, g_ssm, p_ssm))
            k_s.append(ko.reshape(n_seq, w_rows, N_KV_HEADS, HEAD_DIM))
            v_s.append(vo.reshape(n_seq, w_rows, N_KV_HEADS, HEAD_DIM))
        else:
            xp, st = _conformer(xp, None, idx, layer, wts, tm, pad_rows)
            c_p.append(st)
            past = jnp.pad(state_conv[idx].astype(F32), ((0, 0), (CONV_HALO - (conv_w - 1), 0), (0, 0)))
            xs, st = _conformer(xs, past, idx, layer, wts, ns, 0)
            c_s.append(st)
        last = layer == depth - 1
        xp, st = _ffn(xp, None, layer, wts, tm, last)
        f_p.append(st)
        past = jnp.pad(state_ffn[layer].astype(F32), ((0, 0), (FFN_HALO - (ffn_w - 1), 0), (0, 0)))
        xs, st = _ffn(xs, past, layer, wts, ns, last)
        f_s.append(st)

    yp = xp[:, pad_rows + n_meta:]
    ys = xs.reshape(n_seq, tq, dm)
    st = jnp.stack
    return (yp, ys, st(sr_p), st(si_p), st(k_p), st(v_p), st(c_p), st(f_p),
            st(sr_s), st(si_s), st(k_s), st(v_s), st(c_s), st(f_s))
```

```python
import functools
import math

import numpy as np
import jax
import jax.numpy as jnp
from jax import lax
from jax.experimental import pallas as pl
from jax.experimental.pallas import tpu as pltpu

F32 = jnp.float32
BF16 = jnp.bfloat16

EPS = 1e-6
NEG_INF = -1e30
WINDOW = 128
HEAD_DIM = 64
N_KV_HEADS = 2
N_BUCKETS = 32
MAX_DISTANCE = 128
SSM_GROUP = 16
SSM_STATE = 64
SSM_BLK_GROUPS = 8
SUBLANES = 8
CONV_HALO = 32
FFN_HALO = 8
LANES = 128
CONV_PHASES = 4
CONV_PHASE_VREGS = 4
CONV_SLABS = 2
CONF_BLOCK = 128
PROMPT_TILE_TARGET = 768
SAMPLE_SEQS = 32
VMEM_LIMIT = 56 * 1024 * 1024


def _cparams(sem):
    return pltpu.CompilerParams(dimension_semantics=sem, vmem_limit_bytes=VMEM_LIMIT)


def _const_spec(shape):
    nd = len(shape)
    return pl.BlockSpec(shape, lambda *_: (0,) * nd, pipeline_mode=pl.Buffered(1))


def _rmsnorm(x, g):
    return x * lax.rsqrt(jnp.mean(x * x, axis=-1, keepdims=True) + EPS) * g


def _dot(a, b):
    return jnp.dot(a, b, preferred_element_type=F32)


def _ssm_prep_kernel(lr_ref, li_ref, ls_ref, bre_ref, bim_ref, cre_ref, cim_ref,
                     tre_ref, tim_ref, kst_ref, bst_ref, mst_ref):
    lr = lr_ref[0]
    li = li_ref[0]
    dt = jnp.exp(ls_ref[0])
    decay = jnp.exp(lr * dt)
    a_re = decay * jnp.cos(li * dt)
    a_im = decay * jnp.sin(li * dt)
    den = lr * lr + li * li
    num_re = a_re - 1.0
    coef_re = (num_re * lr + a_im * li) / den
    coef_im = (a_im * lr - num_re * li) / den

    cmul = lambda x, y: (x[0] * y[0] - x[1] * y[1], x[0] * y[1] + x[1] * y[0])
    n = lr.shape[-1]
    pw = [(jnp.ones((1, n), F32), jnp.zeros((1, n), F32)), (a_re, a_im)]
    for _ in range(SUBLANES - 1):
        pw.append(cmul(pw[-1], (a_re, a_im)))
    apw = [pw[SUBLANES]]
    for _ in range(SUBLANES - 1):
        apw.append(cmul(apw[-1], pw[SUBLANES]))
    row = lax.broadcasted_iota(jnp.int32, (SUBLANES, n), 0)
    zero = jnp.zeros((SUBLANES, n), F32)
    for t, k in enumerate((1, 2, 4)):
        tre_ref[0, t] = jnp.where(row >= k, jnp.broadcast_to(apw[k - 1][0], (SUBLANES, n)), zero)
        tim_ref[0, t] = jnp.where(row >= k, jnp.broadcast_to(apw[k - 1][1], (SUBLANES, n)), zero)
    ap_re, ap_im = zero, zero
    for j in range(SUBLANES):
        ap_re = jnp.where(row == j, jnp.broadcast_to(apw[j][0], (SUBLANES, n)), ap_re)
        ap_im = jnp.where(row == j, jnp.broadcast_to(apw[j][1], (SUBLANES, n)), ap_im)
    tre_ref[0, 3] = ap_re
    tim_ref[0, 3] = ap_im

    bre = bre_ref[0]
    bim = bim_ref[0]
    bbar = (coef_re * bre - coef_im * bim, coef_re * bim + coef_im * bre)
    rows = SSM_BLK_GROUPS * SSM_GROUP
    r_i = lax.broadcasted_iota(jnp.int32, (rows, n), 0)
    c_i = lax.broadcasted_iota(jnp.int32, (rows, n), 1)
    sh = lambda v, d: lax.shift_right_logical(v, jnp.full(v.shape, int(math.log2(d)), jnp.int32))
    diag = sh(r_i, SSM_GROUP) == (sh(c_i, SSM_STATE) & (SSM_BLK_GROUPS - 1))
    zer = jnp.zeros((rows, n), F32)
    blockdiag = lambda v: jnp.where(diag, jnp.concatenate([v] * SSM_BLK_GROUPS, axis=0), zer)
    bb = (blockdiag(bbar[0]), blockdiag(bbar[1]))
    cc = (blockdiag(cre_ref[0]), blockdiag(cim_ref[0]))
    w = SSM_BLK_GROUPS * SSM_STATE
    nblk = n // w
    for d in range(SUBLANES):
        ba = cmul(bb, pw[d])
        ca = cmul(cc, pw[d + 1])
        for blk in range(nblk):
            sl = slice(blk * w, (blk + 1) * w)
            rs = slice(d * rows, (d + 1) * rows)
            bcat = jnp.concatenate([ba[0][:, sl], ba[1][:, sl]], axis=1)
            ccat = jnp.concatenate([cc[0][:, sl], -cc[1][:, sl]], axis=1)
            bst_ref[0, blk, rs, :] = bcat.astype(BF16)
            kst_ref[0, blk, rs, :] = lax.dot_general(
                bcat, ccat, (((1,), (1,)), ((), ())), precision=lax.Precision.HIGHEST,
                preferred_element_type=F32).astype(BF16)
            mcat = jnp.concatenate([ca[0][:, sl], -ca[1][:, sl]], axis=1)
            mst_ref[0, blk, :, rs] = mcat.T.astype(BF16)


def _ssm_prep(lam_re, lam_im, log_step, b_re, b_im, c_re, c_im):
    ne, g, p = lam_re.shape
    n = g * p
    assert g % SSM_BLK_GROUPS == 0 and p == SSM_STATE and b_re.shape[-1] == SSM_GROUP
    nblk = g // SSM_BLK_GROUPS
    rows = SSM_BLK_GROUPS * SSM_GROUP
    w = SSM_BLK_GROUPS * SSM_STATE
    flat = lambda a: a.reshape(ne, 1, n)
    ls = jnp.broadcast_to(log_step[:, :, None], (ne, g, p))
    bt = lambda a: jnp.transpose(a, (0, 3, 1, 2)).reshape(ne, SSM_GROUP, n)
    ct = lambda a: jnp.transpose(a, (0, 2, 1, 3)).reshape(ne, SSM_GROUP, n)
    vec = pl.BlockSpec((1, 1, n), lambda i: (i, 0, 0))
    mat = pl.BlockSpec((1, SSM_GROUP, n), lambda i: (i, 0, 0))
    tab = pl.BlockSpec((1, 4, SUBLANES, n), lambda i: (i, 0, 0, 0))
    stk = lambda cols: pl.BlockSpec((1, nblk, SUBLANES * rows, cols), lambda i: (i, 0, 0, 0))
    mspec = pl.BlockSpec((1, nblk, 2 * w, SUBLANES * rows), lambda i: (i, 0, 0, 0))
    return pl.pallas_call(
        _ssm_prep_kernel,
        grid=(ne,),
        in_specs=[vec, vec, vec, mat, mat, mat, mat],
        out_specs=[tab, tab, stk(rows), stk(2 * w), mspec],
        out_shape=[jax.ShapeDtypeStruct((ne, 4, SUBLANES, n), F32)] * 2
        + [jax.ShapeDtypeStruct((ne, nblk, SUBLANES * rows, rows), BF16),
           jax.ShapeDtypeStruct((ne, nblk, SUBLANES * rows, 2 * w), BF16),
           jax.ShapeDtypeStruct((ne, nblk, 2 * w, SUBLANES * rows), BF16)],
        compiler_params=pltpu.CompilerParams(vmem_limit_bytes=VMEM_LIMIT),
        name="ssm_prep",
    )(flat(lam_re), flat(lam_im), flat(ls), bt(b_re), bt(b_im), ct(c_re), ct(c_im))


def _bucket_table():
    i = np.arange(WINDOW)[:, None]
    j = np.arange(2 * WINDOW)[None, :]
    dist = WINDOW + i - j
    nn = np.maximum(dist, 0)
    max_exact = N_BUCKETS // 2
    nf = np.maximum(nn, max_exact).astype(np.float32)
    large = max_exact + (np.log(nf / np.float32(max_exact)) / np.float32(math.log(MAX_DISTANCE / max_exact))
                         * np.float32(N_BUCKETS - max_exact)).astype(np.int32)
    large = np.minimum(large, N_BUCKETS - 1)
    bucket = np.where(nn < max_exact, nn, large)
    valid = (dist >= 0) & (dist < WINDOW)
    return np.where(valid, bucket, -1).astype(np.int32)


def _bias_kernel(rb_ref, bucket_ref, out_ref):
    bucket = bucket_ref[...]
    nh = out_ref.shape[0]
    for h in range(nh):
        acc = jnp.full(bucket.shape, NEG_INF, F32)
        for b in range(N_BUCKETS):
            acc = jnp.where(bucket == b, rb_ref[b, h], acc)
        out_ref[h] = acc


def _bias_table(rel_bias):
    nh = rel_bias.shape[1]
    bucket = jnp.asarray(_bucket_table())
    return pl.pallas_call(
        _bias_kernel,
        in_specs=[pl.BlockSpec(memory_space=pltpu.SMEM),
                  pl.BlockSpec(bucket.shape, lambda: (0, 0))],
        out_specs=pl.BlockSpec((nh,) + bucket.shape, lambda: (0, 0, 0)),
        out_shape=jax.ShapeDtypeStruct((nh,) + bucket.shape, F32),
        name="rel_bias_table",
    )(rel_bias.astype(F32), bucket)


def _block_state_scan(sbuf, tre_ref, tim_ref, lane0, w, carry):
    lanes = pl.ds(lane0, w)
    first = lax.broadcasted_iota(jnp.int32, (SUBLANES, w), 0) == 0

    def body(b, carry):
        hr, hi = carry
        r0 = pl.multiple_of(b * SUBLANES, SUBLANES)
        xr = sbuf[pl.ds(r0, SUBLANES), 0:w]
        xi = sbuf[pl.ds(r0, SUBLANES), w:2 * w]
        for t, k in enumerate((1, 2, 4)):
            cr = tre_ref[0, t, :, lanes]
            ci = tim_ref[0, t, :, lanes]
            rr = pltpu.roll(xr, k, 0)
            ri = pltpu.roll(xi, k, 0)
            xr, xi = xr + cr * rr - ci * ri, xi + cr * ri + ci * rr
        pr = tre_ref[0, 3, :, lanes]
        pi = tim_ref[0, 3, :, lanes]
        xr, xi = xr + pr * hr - pi * hi, xi + pr * hi + pi * hr
        sbuf[pl.ds(r0, SUBLANES), 0:w] = jnp.where(first, hr, pltpu.roll(xr, 1, 0))
        sbuf[pl.ds(r0, SUBLANES), w:2 * w] = jnp.where(first, hi, pltpu.roll(xi, 1, 0))
        return (jnp.broadcast_to(xr[SUBLANES - 1:SUBLANES, :], (SUBLANES, w)),
                jnp.broadcast_to(xi[SUBLANES - 1:SUBLANES, :], (SUBLANES, w)))

    return lax.fori_loop(0, sbuf.shape[0] // SUBLANES, body, carry)


def _s5_mix(u, uext, yint, kst_ref, bst_ref, mst_ref, d, wglu_ref, bglu, advance):
    nblk = kst_ref.shape[1]
    cw = kst_ref.shape[3]
    rows = u.shape[0]
    nb = rows // SUBLANES
    j_i = lax.broadcasted_iota(jnp.int32, (1, SUBLANES, 1), 1)
    ys = []
    for blk in range(nblk):
        ub = u[:, blk * cw:(blk + 1) * cw]
        u3 = ub.reshape(nb, SUBLANES, cw)
        parts = [ub]
        for dd in range(1, SUBLANES):
            parts.append(jnp.where(j_i >= dd, pltpu.roll(u3, dd, 1), 0.0).reshape(rows, cw))
        for dd in range(SUBLANES):
            uext[dd] = parts[dd]
        y_intra = _dot(jnp.concatenate(parts, axis=1).astype(BF16), kst_ref[0, blk])
        last = [uext[dd, pl.ds(SUBLANES - 1, nb, stride=SUBLANES), :] for dd in range(SUBLANES)]
        s = _dot(jnp.concatenate(last, axis=1).astype(BF16), bst_ref[0, blk])
        hprev = advance(blk, s)
        yi = _dot(hprev.astype(BF16), mst_ref[0, blk])
        for j in range(SUBLANES):
            yint[pl.ds(j, nb, stride=SUBLANES), :] = yi[:, j * cw:(j + 1) * cw]
        ys.append(y_intra + yint[...])
    y = jnp.concatenate(ys, axis=1) + d * u
    g = jax.nn.gelu(y)
    return g * jax.nn.sigmoid(_dot(g.astype(BF16), wglu_ref[0]) + bglu)


def _mixer_prompt_kernel(x_ref, g_ref, wukv_ref, wqt_ref, wvt_ref, kst_ref, bst_ref, mst_ref, tre_ref, tim_ref,
                         d_ref, wglu_ref, bglu_ref, bias_ref, sink_ref, wout_ref,
                         o_ref, sre_ref, sim_ref, ko_ref, vo_ref,
                         uext, yint, sbuf, hst, kbuf, vtbuf, mix, sc, pb):
    t = pl.program_id(1)
    tm = x_ref.shape[1]
    d_ssm = d_ref.shape[-1]
    d_kv = N_KV_HEADS * HEAD_DIM
    gq = bias_ref.shape[3] // WINDOW
    nt_dims = (((1,), (1,)), ((), ()))

    @pl.when(t == 0)
    def _():
        hst[...] = jnp.zeros_like(hst)
        kbuf[0:WINDOW, :] = jnp.zeros((WINDOW, d_kv), F32)
        vtbuf[:, 0:WINDOW] = jnp.zeros((d_kv, WINDOW), F32)

    x = x_ref[0]
    h = _rmsnorm(x, g_ref[0]).astype(BF16)
    z = _dot(h, wukv_ref[0])
    u = z[:, :d_ssm]
    kbuf[WINDOW:, :] = z[:, d_ssm:d_ssm + d_kv]
    v = z[:, d_ssm + d_kv:]
    qt = lax.dot_general(wqt_ref[0], h, nt_dims, preferred_element_type=F32) * (HEAD_DIM ** -0.5)
    vtbuf[:, WINDOW:] = lax.dot_general(wvt_ref[0], h, nt_dims, preferred_element_type=F32)

    n_state = hst.shape[1] // 2

    w = sbuf.shape[1] // 2

    def advance(blk, s):
        lane0 = blk * w
        sbuf[...] = s
        init = (hst[:, lane0:lane0 + w], hst[:, n_state + lane0:n_state + lane0 + w])
        hr, hi = _block_state_scan(sbuf, tre_ref, tim_ref, lane0, w, init)
        hst[:, lane0:lane0 + w] = hr
        hst[:, n_state + lane0:n_state + lane0 + w] = hi
        return sbuf[...]

    ya = _s5_mix(u, uext, yint, kst_ref, bst_ref, mst_ref, d_ref[0], wglu_ref, bglu_ref[0], advance)
    mix[:, 0:d_ssm] = ya.astype(BF16)
    sre_ref[0] = hst[0:1, 0:n_state]
    sim_ref[0] = hst[0:1, n_state:]

    nqb = tm // WINDOW
    for qb in range(nqb):
        c0 = qb * WINDOW
        var = jnp.where(t == 0, qb + 1, 0) if qb + 1 < bias_ref.shape[0] else 0
        for j in range(N_KV_HEADS):
            kb = kbuf[c0:c0 + 2 * WINDOW, j * HEAD_DIM:(j + 1) * HEAD_DIM].astype(BF16)
            qc = jnp.concatenate(
                [qt[(j * gq + g) * HEAD_DIM:(j * gq + g + 1) * HEAD_DIM, c0:c0 + WINDOW] for g in range(gq)],
                axis=1).astype(BF16)
            sc[j, qb] = _dot(kb, qc) + bias_ref[var, j]
    dens = []
    for j in range(N_KV_HEADS):
        s = sc[j]
        sink = sink_ref[0, j][None]
        m = jnp.maximum(jnp.max(s, axis=1, keepdims=True), sink)
        p = jnp.exp(s - m)
        dens.append(jnp.sum(p, axis=1, keepdims=True) + jnp.exp(sink - m))
        pb[j] = p.astype(BF16)
    for qb in range(nqb):
        c0 = qb * WINDOW
        pieces = []
        for j in range(N_KV_HEADS):
            vb = vtbuf[j * HEAD_DIM:(j + 1) * HEAD_DIM, c0:c0 + 2 * WINDOW].astype(BF16)
            o = _dot(vb, pb[j, qb]) / dens[j][qb]
            pieces += [o[:, g * WINDOW:(g + 1) * WINDOW] for g in range(gq)]
        ot = jnp.concatenate(pieces, axis=0)
        mix[c0:c0 + WINDOW, d_ssm:] = ot.T.astype(BF16)

    kbuf[0:WINDOW, :] = kbuf[tm:tm + WINDOW, :]
    vtbuf[:, 0:WINDOW] = vtbuf[:, tm:tm + WINDOW]
    ko_ref[0] = kbuf[0:WINDOW, :]
    vo_ref[0] = v[tm - WINDOW:, :]
    o_ref[0] = x + _dot(mix[...], wout_ref[0])


def _mixer_prompt(x, idx, layer, wts, tm):
    nb, tp, dm = x.shape
    nt = tp // tm
    n_state = wts["tre"].shape[-1]
    d_ssm = wts["ssm_d"].shape[-1]
    d_kv = N_KV_HEADS * HEAD_DIM
    w = wts["bst"].shape[-1] // 2
    cw = wts["kst"].shape[-1]
    layer_spec = lambda a, i: pl.BlockSpec((1,) + a.shape[1:], lambda b, t: (i,) + (0,) * (a.ndim - 1),
                                           pipeline_mode=pl.Buffered(1))
    ins = [
        (x, pl.BlockSpec((1, tm, dm), lambda b, t: (b, t, 0))),
        (wts["g_mix"], layer_spec(wts["g_mix"], layer)),
        (wts["w_ukv"], layer_spec(wts["w_ukv"], idx)),
        (wts["w_qt"], layer_spec(wts["w_qt"], idx)),
        (wts["w_vt"], layer_spec(wts["w_vt"], idx)),
        (wts["kst"], layer_spec(wts["kst"], idx)),
        (wts["bst"], layer_spec(wts["bst"], idx)),
        (wts["mst"], layer_spec(wts["mst"], idx)),
        (wts["tre"], layer_spec(wts["tre"], idx)),
        (wts["tim"], layer_spec(wts["tim"], idx)),
        (wts["ssm_d"], layer_spec(wts["ssm_d"], idx)),
        (wts["w_glu"], layer_spec(wts["w_glu"], idx)),
        (wts["b_glu"], layer_spec(wts["b_glu"], idx)),
        (wts["bias_p"], _const_spec(wts["bias_p"].shape)),
        (wts["sink_p"], layer_spec(wts["sink_p"], idx)),
        (wts["w_out"], layer_spec(wts["w_out"], idx)),
    ]
    out_shape = [
        jax.ShapeDtypeStruct((nb, tp, dm), F32),
        jax.ShapeDtypeStruct((nb, 1, n_state), F32),
        jax.ShapeDtypeStruct((nb, 1, n_state), F32),
        jax.ShapeDtypeStruct((nb, WINDOW, d_kv), F32),
        jax.ShapeDtypeStruct((nb, WINDOW, d_kv), F32),
    ]
    out_specs = [
        pl.BlockSpec((1, tm, dm), lambda b, t: (b, t, 0)),
        pl.BlockSpec((1, 1, n_state), lambda b, t: (b, 0, 0)),
        pl.BlockSpec((1, 1, n_state), lambda b, t: (b, 0, 0)),
        pl.BlockSpec((1, WINDOW, d_kv), lambda b, t: (b, 0, 0)),
        pl.BlockSpec((1, WINDOW, d_kv), lambda b, t: (b, 0, 0)),
    ]
    scratch = [
        pltpu.VMEM((SUBLANES, tm, cw), F32),
        pltpu.VMEM((tm, cw), F32),
        pltpu.VMEM((tm // SUBLANES, 2 * w), F32),
        pltpu.VMEM((SUBLANES, 2 * n_state), F32),
        pltpu.VMEM((tm + WINDOW, d_kv), F32),
        pltpu.VMEM((d_kv, tm + WINDOW), F32),
        pltpu.VMEM((tm, dm), BF16),
        pltpu.VMEM((N_KV_HEADS, tm // WINDOW) + wts["bias_p"].shape[2:], F32),
        pltpu.VMEM((N_KV_HEADS, tm // WINDOW) + wts["bias_p"].shape[2:], BF16),
    ]
    assert tm // WINDOW >= wts["bias_p"].shape[0] - 1
    return pl.pallas_call(
        _mixer_prompt_kernel,
        grid=(nb, nt),
        in_specs=[s for _, s in ins],
        out_specs=out_specs,
        out_shape=out_shape,
        scratch_shapes=scratch,
        compiler_params=_cparams(("arbitrary", "arbitrary")),
        name=f"mixer_prompt_l{layer}",
    )(*[a for a, _ in ins])


def _mixer_sample_kernel(x_ref, h0r_ref, h0i_ref, kc_ref, vc_ref, g_ref, win_ref, kst_ref, bst_ref, mst_ref,
                         tre_ref, tim_ref, d_ref, wglu_ref, bglu_ref, bias_ref, sink_ref, wout_ref,
                         o_ref, sre_ref, sim_ref, ko_ref, vo_ref, uext, yint):
    rows = x_ref.shape[0]
    ns = kc_ref.shape[0]
    tq = rows // ns
    assert tq == SUBLANES
    w_rows = kc_ref.shape[1]
    d_ssm = d_ref.shape[-1]
    gq = bias_ref.shape[1] // tq
    d_q = N_KV_HEADS * gq * HEAD_DIM
    d_kv = N_KV_HEADS * HEAD_DIM

    x = x_ref[...]
    h = _rmsnorm(x, g_ref[0]).astype(BF16)
    z = _dot(h, win_ref[0])
    u = z[:, :d_ssm]
    q = (z[:, d_ssm:d_ssm + d_q] * (HEAD_DIM ** -0.5)).reshape(ns, tq, d_q)
    kn = z[:, d_ssm + d_q:d_ssm + d_q + d_kv].reshape(ns, tq, d_kv)
    vn = z[:, d_ssm + d_q + d_kv:].reshape(ns, tq, d_kv)

    w = bst_ref.shape[-1] // 2

    def advance(blk, s):
        lanes = slice(blk * w, (blk + 1) * w)
        hr = h0r_ref[:, lanes]
        hi = h0i_ref[:, lanes]
        ar = tre_ref[0, 0, SUBLANES - 1:SUBLANES, lanes]
        ai = tim_ref[0, 0, SUBLANES - 1:SUBLANES, lanes]
        sre_ref[:, lanes] = ar * hr - ai * hi + s[:, 0:w]
        sim_ref[:, lanes] = ar * hi + ai * hr + s[:, w:2 * w]
        return jnp.concatenate([hr, hi], axis=1)

    ya = _s5_mix(u, uext, yint, kst_ref, bst_ref, mst_ref, d_ref[0], wglu_ref, bglu_ref[0], advance)

    kc = jnp.concatenate([kc_ref[...], kn], axis=1)
    vc = jnp.concatenate([vc_ref[...], vn], axis=1)
    ko_ref[...] = kc[:, tq:, :]
    vo_ref[...] = vc[:, tq:, :]
    pieces = []
    for j in range(N_KV_HEADS):
        kb = kc[:, :, j * HEAD_DIM:(j + 1) * HEAD_DIM].astype(BF16)
        vb = vc[:, :, j * HEAD_DIM:(j + 1) * HEAD_DIM].astype(BF16)
        qs = jnp.concatenate(
            [q[:, :, (j * gq + g) * HEAD_DIM:(j * gq + g + 1) * HEAD_DIM] for g in range(gq)],
            axis=1).astype(BF16)
        s = jnp.einsum("nqd,nkd->nqk", qs, kb, preferred_element_type=F32) + bias_ref[j]
        sink = sink_ref[0, j]
        m = jnp.maximum(jnp.max(s, axis=-1, keepdims=True), sink)
        p = jnp.exp(s - m)
        l = jnp.sum(p, axis=-1, keepdims=True) + jnp.exp(sink - m)
        o = jnp.einsum("nqk,nkd->nqd", p.astype(BF16), vb, preferred_element_type=F32) / l
        pieces += [o[:, g * tq:(g + 1) * tq, :] for g in range(gq)]
    yb = jnp.concatenate(pieces, axis=2).reshape(rows, d_q)
    mixed = jnp.concatenate([ya, yb], axis=1).astype(BF16)
    o_ref[...] = x + _dot(mixed, wout_ref[0])


def _mixer_sample(x2, h0r, h0i, kc, vc, idx, layer, wts, ns):
    rows_all, dm = x2.shape
    n_seq, w_rows, d_kv = kc.shape
    tq = rows_all // n_seq
    n_state = h0r.shape[-1]
    cw = wts["kst"].shape[-1]
    layer_spec = lambda a, i: pl.BlockSpec((1,) + a.shape[1:], lambda s: (i,) + (0,) * (a.ndim - 1),
                                           pipeline_mode=pl.Buffered(1))
    row_spec = pl.BlockSpec((ns * tq, dm), lambda s: (s, 0))
    st_spec = pl.BlockSpec((ns, n_state), lambda s: (s, 0))
    kv_spec = pl.BlockSpec((ns, w_rows, d_kv), lambda s: (s, 0, 0))
    ins = [
        (x2, row_spec), (h0r, st_spec), (h0i, st_spec), (kc, kv_spec), (vc, kv_spec),
        (wts["g_mix"], layer_spec(wts["g_mix"], layer)),
        (wts["w_in"], layer_spec(wts["w_in"], idx)),
        (wts["kst"], layer_spec(wts["kst"], idx)),
        (wts["bst"], layer_spec(wts["bst"], idx)),
        (wts["mst"], layer_spec(wts["mst"], idx)),
        (wts["tre"], layer_spec(wts["tre"], idx)),
        (wts["tim"], layer_spec(wts["tim"], idx)),
        (wts["ssm_d"], layer_spec(wts["ssm_d"], idx)),
        (wts["w_glu"], layer_spec(wts["w_glu"], idx)),
        (wts["b_glu"], layer_spec(wts["b_glu"], idx)),
        (wts["bias_s"], _const_spec(wts["bias_s"].shape)),
        (wts["sink_s"], layer_spec(wts["sink_s"], idx)),
        (wts["w_out"], layer_spec(wts["w_out"], idx)),
    ]
    return pl.pallas_call(
        _mixer_sample_kernel,
        grid=(n_seq // ns,),
        in_specs=[s for _, s in ins],
        out_specs=[row_spec, st_spec, st_spec, kv_spec, kv_spec],
        out_shape=[jax.ShapeDtypeStruct(x2.shape, F32), jax.ShapeDtypeStruct(h0r.shape, F32),
                   jax.ShapeDtypeStruct(h0i.shape, F32), jax.ShapeDtypeStruct(kc.shape, F32),
                   jax.ShapeDtypeStruct(vc.shape, F32)],
        scratch_shapes=[pltpu.VMEM((SUBLANES, ns * tq, cw), F32), pltpu.VMEM((ns * tq, cw), F32)],
        compiler_params=_cparams(("arbitrary",)),
        name=f"mixer_sample_l{layer}",
    )(*[a for a, _ in ins])


def _conformer_head(x, g_ref, w1_ref, c):
    h = _rmsnorm(x, g_ref[0]).astype(BF16)
    z = _dot(h, w1_ref[0])
    return z[:, :c] * jax.nn.sigmoid(z[:, c:])


def _conformer_tail(y, bdw_ref, lng_ref, lnb_ref, w2_ref):
    y = y + bdw_ref[0]
    mu = jnp.mean(y, axis=-1, keepdims=True)
    yc = y - mu
    var = jnp.mean(yc * yc, axis=-1, keepdims=True)
    y = yc * lax.rsqrt(var + EPS) * lng_ref[0] + lnb_ref[0]
    return _dot(jax.nn.silu(y).astype(BF16), w2_ref[0])


def _conformer_prompt_kernel(x_ref, g_ref, w1_ref, wdw_ref, bdw_ref, lng_ref, lnb_ref, w2_ref, o_ref, st_ref,
                             ext, ybuf, *, pad_rows, width):
    t = pl.program_id(1)
    x = x_ref[0]
    nslab, le, lanes = ext.shape
    l = le - CONV_HALO
    c = nslab * lanes
    off = CONV_HALO - (width - 1)
    mrows = CONV_PHASE_VREGS * SUBLANES
    span = CONV_PHASES * mrows
    assert l % CONF_BLOCK == 0 and CONF_BLOCK % span == 0

    @pl.when(t == 0)
    def _():
        ext[:, 0:CONV_HALO, :] = jnp.zeros((nslab, CONV_HALO, lanes), F32)

    gl = _conformer_head(x, g_ref, w1_ref, c)
    for q in range(nslab):
        ext[q, CONV_HALO:, :] = gl[:, q * lanes:(q + 1) * lanes]
    st_ref[0] = jnp.concatenate([ext[q, l + off:, :] for q in range(nslab)], axis=1)

    def block(cb, _):
        r0 = pl.multiple_of(cb * CONF_BLOCK, CONF_BLOCK)
        for grp in range(CONF_BLOCK // span):
            g0 = r0 + grp * span
            for q0 in range(0, nslab, CONV_SLABS):
                slabs = range(q0, q0 + CONV_SLABS)
                acc = {(r, q): jnp.zeros((CONV_PHASE_VREGS, SUBLANES, lanes), F32)
                       for r in range(CONV_PHASES) for q in slabs}
                for s in range(width + CONV_PHASES - 1):
                    for q in slabs:
                        xs = ext[q, pl.ds(g0 + (off + s), mrows, stride=CONV_PHASES), :]
                        xs = xs.reshape(CONV_PHASE_VREGS, SUBLANES, lanes)
                        for r in range(CONV_PHASES):
                            k = s - r
                            if 0 <= k < width:
                                wk = wdw_ref[0, k, :, q * lanes:(q + 1) * lanes]
                                acc[r, q] = acc[r, q] + wk[None] * xs
                for r in range(CONV_PHASES):
                    for q in slabs:
                        ybuf[q, pl.ds(g0 + r, mrows, stride=CONV_PHASES), :] = acc[r, q].reshape(mrows, lanes)
        y = jnp.concatenate([ybuf[q, pl.ds(r0, CONF_BLOCK), :] for q in range(nslab)], axis=1)
        out = _conformer_tail(y, bdw_ref, lng_ref, lnb_ref, w2_ref)
        rid = t * l + r0 + lax.broadcasted_iota(jnp.int32, (CONF_BLOCK, 1), 0)
        o_ref[0, pl.ds(r0, CONF_BLOCK), :] = x_ref[0, pl.ds(r0, CONF_BLOCK), :] + jnp.where(rid >= pad_rows, out, 0.0)
        return 0

    lax.fori_loop(0, l // CONF_BLOCK, block, 0)
    ext[:, 0:CONV_HALO, :] = ext[:, l:l + CONV_HALO, :]


def _conformer_sample_kernel(x_ref, past_ref, g_ref, w1_ref, wdw_ref, bdw_ref, lng_ref, lnb_ref, w2_ref,
                             o_ref, st_ref, ext, ybuf, stage, *, width):
    x = x_ref[...]
    ns, le, c = ext.shape
    l = le - CONV_HALO
    off = CONV_HALO - (width - 1)
    ext[:, 0:off, :] = jnp.zeros((ns, off, c), F32)
    ext[:, off:CONV_HALO, :] = past_ref[...]
    ext[:, CONV_HALO:, :] = _conformer_head(x, g_ref, w1_ref, c).reshape(ns, l, c)
    st_ref[...] = ext[:, l + off:, :]

    def conv_unit(n, stg):
        win = ext[n]
        for s in range(1, SUBLANES):
            stg[s - 1] = win[s:s + l + CONV_HALO - SUBLANES]
        acc = jnp.zeros((l // SUBLANES, SUBLANES, c), F32)
        for k in range(width):
            a, s = divmod(off + k, SUBLANES)
            if s == 0:
                xk = ext[n, a * SUBLANES:a * SUBLANES + l, :]
            else:
                xk = stg[s - 1, a * SUBLANES:a * SUBLANES + l, :]
            acc = acc + wdw_ref[0, k][None] * xk.reshape(l // SUBLANES, SUBLANES, c)
        return acc.reshape(l, c)

    def seq(i, _):
        for half in range(2):
            n = 2 * i + half
            ybuf[pl.ds(pl.multiple_of(n * l, l), l), :] = conv_unit(n, stage.at[half])
        return 0
    assert ns % 2 == 0 and l % SUBLANES == 0
    lax.fori_loop(0, ns // 2, seq, 0)
    o_ref[...] = x + _conformer_tail(ybuf[...], bdw_ref, lng_ref, lnb_ref, w2_ref)


def _conformer(x, past, idx, layer, wts, tile, pad_rows):
    carry = past is None
    width = wts["w_dw"].shape[1]
    c = wts["w_dw"].shape[3]
    if carry:
        nb, tp, dm = x.shape
        grid = (nb, tp // tile)
        lspec = lambda a, i: pl.BlockSpec((1,) + a.shape[1:], lambda b, t: (i,) + (0,) * (a.ndim - 1),
                                          pipeline_mode=pl.Buffered(1))
        xspec = pl.BlockSpec((1, tile, dm), lambda b, t: (b, t, 0))
        ins = [(x, xspec)]
        st_shape = (nb, width - 1, c)
        st_spec = pl.BlockSpec((1, width - 1, c), lambda b, t: (b, 0, 0))
        ns, l = 1, tile
        sem = ("arbitrary", "arbitrary")
    else:
        rows_all, dm = x.shape
        n_seq = past.shape[0]
        l = rows_all // n_seq
        ns = tile
        grid = (n_seq // ns,)
        lspec = lambda a, i: pl.BlockSpec((1,) + a.shape[1:], lambda s: (i,) + (0,) * (a.ndim - 1),
                                          pipeline_mode=pl.Buffered(1))
        xspec = pl.BlockSpec((ns * l, dm), lambda s: (s, 0))
        ins = [(x, xspec), (past, pl.BlockSpec((ns, width - 1, c), lambda s: (s, 0, 0)))]
        st_shape = (n_seq, width - 1, c)
        st_spec = pl.BlockSpec((ns, width - 1, c), lambda s: (s, 0, 0))
        sem = ("arbitrary",)
    ins += [
        (wts["g_mix"], lspec(wts["g_mix"], layer)),
        (wts["w_pw1"], lspec(wts["w_pw1"], idx)),
        (wts["w_dw"], lspec(wts["w_dw"], idx)),
        (wts["b_dw"], lspec(wts["b_dw"], idx)),
        (wts["ln_g"], lspec(wts["ln_g"], idx)),
        (wts["ln_b"], lspec(wts["ln_b"], idx)),
        (wts["w_pw2"], lspec(wts["w_pw2"], idx)),
    ]
    if carry:
        body = functools.partial(_conformer_prompt_kernel, pad_rows=pad_rows, width=width)
        scratch = [pltpu.VMEM((c // LANES, l + CONV_HALO, LANES), F32), pltpu.VMEM((c // LANES, l, LANES), F32)]
    else:
        body = functools.partial(_conformer_sample_kernel, width=width)
        scratch = [pltpu.VMEM((ns, l + CONV_HALO, c), F32), pltpu.VMEM((ns * l, c), F32),
                   pltpu.VMEM((2, SUBLANES - 1, l + CONV_HALO - SUBLANES, c), F32)]
    return pl.pallas_call(
        body,
        grid=grid,
        in_specs=[s for _, s in ins],
        out_specs=[xspec, st_spec],
        out_shape=[jax.ShapeDtypeStruct(x.shape, F32), jax.ShapeDtypeStruct(st_shape, F32)],
        scratch_shapes=scratch,
        compiler_params=_cparams(sem),
        name=f"conformer_{'prompt' if carry else 'sample'}_l{layer}",
    )(*[a for a, _ in ins])


def _ffn_kernel(*refs, carry, final_norm, n_chunks):
    if carry:
        (x_ref, g_ref, wup_ref, wcv_ref, bcv_ref, wdn_ref, gf_ref, o_ref, st_ref, ext) = refs
        past_ref = None
        t = pl.program_id(1)
        x = x_ref[0]
    else:
        (x_ref, past_ref, g_ref, wup_ref, wcv_ref, bcv_ref, wdn_ref, gf_ref, o_ref, st_ref, ext) = refs
        x = x_ref[...]
    ns, le, dff = ext.shape
    l = le - FFN_HALO
    rows = ns * l
    kw = wcv_ref.shape[1]
    ch = dff // n_chunks

    if carry:
        @pl.when(t == 0)
        def _():
            ext[:, 0:FFN_HALO, :] = jnp.zeros((ns, FFN_HALO, dff), F32)
    else:
        ext[:, FFN_HALO - (kw - 1):FFN_HALO, :] = past_ref[...]

    h = _rmsnorm(x, g_ref[0]).astype(BF16)
    acc = x
    for cix in range(n_chunks):
        c0 = cix * ch
        gate = _dot(h, wup_ref[0, :, c0:c0 + ch])
        up = _dot(h, wup_ref[0, :, dff + c0:dff + c0 + ch])
        ext[:, FFN_HALO:, c0:c0 + ch] = gate.reshape(ns, l, ch)
        gc = bcv_ref[0, :, c0:c0 + ch]
        for k in range(kw):
            o = FFN_HALO - (kw - 1) + k
            gc = gc + wcv_ref[0, k:k + 1, c0:c0 + ch] * ext[:, o:o + l, c0:c0 + ch]
        y = (jax.nn.gelu(gc).reshape(rows, ch) * up).astype(BF16)
        acc = acc + _dot(y, wdn_ref[0, c0:c0 + ch, :])
    st_ref[...] = ext[:, le - (kw - 1):, :].reshape(st_ref.shape)
    if final_norm:
        acc = _rmsnorm(acc, gf_ref[...])
    if carry:
        o_ref[0] = acc
        ext[:, 0:FFN_HALO, :] = ext[:, l:l + FFN_HALO, :]
    else:
        o_ref[...] = acc


def _ffn(x, past, layer, wts, tile, final_norm):
    carry = past is None
    dff = wts["w_dn"].shape[1]
    kw = wts["w_cv"].shape[1]
    if carry:
        nb, tp, dm = x.shape
        grid = (nb, tp // tile)
        lspec = lambda a, i: pl.BlockSpec((1,) + a.shape[1:], lambda b, t: (i,) + (0,) * (a.ndim - 1),
                                          pipeline_mode=pl.Buffered(1))
        xspec = pl.BlockSpec((1, tile, dm), lambda b, t: (b, t, 0))
        ins = [(x, xspec)]
        st_shape = (nb, kw - 1, dff)
        st_spec = pl.BlockSpec((1, kw - 1, dff), lambda b, t: (b, 0, 0))
        ns, l = 1, tile
        sem = ("arbitrary", "arbitrary")
    else:
        rows_all, dm = x.shape
        n_seq = past.shape[0]
        l = rows_all // n_seq
        ns = tile
        grid = (n_seq // ns,)
        lspec = lambda a, i: pl.BlockSpec((1,) + a.shape[1:], lambda s: (i,) + (0,) * (a.ndim - 1),
                                          pipeline_mode=pl.Buffered(1))
        xspec = pl.BlockSpec((ns * l, dm), lambda s: (s, 0))
        ins = [(x, xspec), (past, pl.BlockSpec((ns, kw - 1, dff), lambda s: (s, 0, 0)))]
        st_shape = (n_seq, kw - 1, dff)
        st_spec = pl.BlockSpec((ns, kw - 1, dff), lambda s: (s, 0, 0))
        sem = ("arbitrary",)
    ins += [
        (wts["g_ffn"], lspec(wts["g_ffn"], layer)),
        (wts["w_up"], lspec(wts["w_up"], layer)),
        (wts["w_cv"], lspec(wts["w_cv"], layer)),
        (wts["b_cv"], lspec(wts["b_cv"], layer)),
        (wts["w_dn"], lspec(wts["w_dn"], layer)),
        (wts["g_final"], _const_spec(wts["g_final"].shape)),
    ]
    n_chunks = 2 if dff % 256 == 0 else 1
    return pl.pallas_call(
        functools.partial(_ffn_kernel, carry=carry, final_norm=final_norm, n_chunks=n_chunks),
        grid=grid,
        in_specs=[s for _, s in ins],
        out_specs=[xspec, st_spec],
        out_shape=[jax.ShapeDtypeStruct(x.shape, F32), jax.ShapeDtypeStruct(st_shape, F32)],
        scratch_shapes=[pltpu.VMEM((ns, l + FFN_HALO, dff), F32)],
        compiler_params=_cparams(sem),
        name=f"ffn_{'prompt' if carry else 'sample'}_l{layer}",
    )(*[a for a, _ in ins])


def _pick_tile(total, target):
    best = WINDOW
    for m in range(1, total // WINDOW + 1):
        if total % (m * WINDOW) == 0 and m * WINDOW <= target:
            best = m * WINDOW
    return best


def kernel(x_prompt, x_sample, state_ssm_re, state_ssm_im, cache_swa_k, cache_swa_v, state_conv, state_ffn,
           meta_tokens, g_mix, g_ffn, g_final, w_in_mix, ssm_lambda_re, ssm_lambda_im, ssm_log_step,
           ssm_b_re, ssm_b_im, ssm_c_re, ssm_c_im, ssm_d, ssm_w_glu, ssm_b_glu, rel_bias, attn_sinks,
           w_out_mix, conv_w_pw1, conv_w_dw, conv_b_dw, conv_ln_g, conv_ln_b, conv_w_pw2,
           ffn_w_up, ffn_w_conv, ffn_b_conv, ffn_w_down):
    nb, seq, dm = x_prompt.shape
    n_seq, tq, _ = x_sample.shape
    depth = g_mix.shape[0]
    n_meta = meta_tokens.shape[0]
    n_even, g_ssm, p_ssm = ssm_lambda_re.shape
    n_heads = rel_bias.shape[1]
    gq = n_heads // N_KV_HEADS
    w_rows = cache_swa_k.shape[2]
    d_kv = N_KV_HEADS * HEAD_DIM
    conv_w = conv_w_dw.shape[1]
    ffn_w = ffn_w_conv.shape[1]
    assert tq == SUBLANES and w_rows == WINDOW and seq % WINDOW == 0 and n_meta <= WINDOW
    assert conv_w - 1 <= CONV_HALO and ffn_w - 1 <= FFN_HALO

    tre, tim, kst, bst, mst = _ssm_prep(ssm_lambda_re, ssm_lambda_im, ssm_log_step,
                                        ssm_b_re, ssm_b_im, ssm_c_re, ssm_c_im)
    bias = _bias_table(rel_bias)
    row3 = lambda a: a.reshape(a.shape[0], 1, a.shape[-1]).astype(F32)
    sinks = attn_sinks.astype(F32).reshape(n_even, N_KV_HEADS, gq, 1)
    pad_rows = WINDOW - n_meta
    bias_t = jnp.transpose(bias.reshape(N_KV_HEADS, gq, WINDOW, 2 * WINDOW), (0, 3, 1, 2)
                           ).reshape(N_KV_HEADS, 2 * WINDOW, gq * WINDOW)
    key_i = jnp.arange(2 * WINDOW)[None, :, None]
    n_var = 1 + -(-(pad_rows + WINDOW) // WINDOW)
    bias_p = jnp.stack([bias_t] + [jnp.where(key_i >= pad_rows + WINDOW - qb * WINDOW, bias_t, NEG_INF)
                                   for qb in range(n_var - 1)])
    d_ssm = ssm_d.shape[-1]
    d_q = n_heads * HEAD_DIM
    w_in_bf = w_in_mix.astype(BF16)
    wts = {
        "g_mix": row3(g_mix), "g_ffn": row3(g_ffn), "g_final": g_final.reshape(1, dm).astype(F32),
        "w_in": w_in_bf, "kst": kst, "bst": bst, "mst": mst, "tre": tre, "tim": tim,
        "w_ukv": jnp.concatenate([w_in_bf[:, :, :d_ssm], w_in_bf[:, :, d_ssm + d_q:]], axis=-1),
        "w_qt": jnp.transpose(w_in_bf[:, :, d_ssm:d_ssm + d_q], (0, 2, 1)),
        "w_vt": jnp.transpose(w_in_bf[:, :, d_ssm + d_q + d_kv:], (0, 2, 1)),
        "ssm_d": row3(ssm_d), "w_glu": ssm_w_glu.astype(BF16), "b_glu": row3(ssm_b_glu),
        "bias_p": bias_p,
        "bias_s": bias[:, :tq, :w_rows + tq].reshape(N_KV_HEADS, gq * tq, w_rows + tq),
        "sink_p": jnp.broadcast_to(sinks[:, :, :, None, :], (n_even, N_KV_HEADS, gq, WINDOW, 1)
                                   ).reshape(n_even, N_KV_HEADS, 1, gq * WINDOW),
        "sink_s": jnp.broadcast_to(sinks[:, :, :, None, :], (n_even, N_KV_HEADS, gq, tq, 1)
                                   ).reshape(n_even, N_KV_HEADS, gq * tq, 1),
        "w_out": w_out_mix.astype(BF16),
        "w_pw1": conv_w_pw1.astype(BF16), "w_dw": jnp.broadcast_to(conv_w_dw.astype(F32)[:, :, None, :],
                                 conv_w_dw.shape[:2] + (SUBLANES, conv_w_dw.shape[2])), "b_dw": row3(conv_b_dw),
        "ln_g": row3(conv_ln_g), "ln_b": row3(conv_ln_b), "w_pw2": conv_w_pw2.astype(BF16),
        "w_up": ffn_w_up.astype(BF16), "w_cv": ffn_w_conv.astype(F32), "b_cv": row3(ffn_b_conv),
        "w_dn": ffn_w_down.astype(BF16),
    }

    tp = pad_rows + n_meta + seq
    tm = _pick_tile(tp, PROMPT_TILE_TARGET)
    xp = jnp.concatenate([jnp.zeros((nb, pad_rows, dm), F32),
                          jnp.broadcast_to(meta_tokens.astype(F32)[None], (nb, n_meta, dm)),
                          x_prompt.astype(F32)], axis=1)
    xs = x_sample.astype(F32).reshape(n_seq * tq, dm)
    ns = min(SAMPLE_SEQS, n_seq)
    assert n_seq % ns == 0

    sr_p, si_p, k_p, v_p, c_p, f_p = [], [], [], [], [], []
    sr_s, si_s, k_s, v_s, c_s, f_s = [], [], [], [], [], []
    for layer in range(depth):
        idx = layer // 2
        if layer % 2 == 0:
            xp, sre, sim, ko, vo = _mixer_prompt(xp, idx, layer, wts, tm)
            sr_p.append(sre.reshape(nb, g_ssm, p_ssm))
            si_p.append(sim.reshape(nb, g_ssm, p_ssm))
            k_p.append(ko.reshape(nb, WINDOW, N_KV_HEADS, HEAD_DIM))
            v_p.append(vo.reshape(nb, WINDOW, N_KV_HEADS, HEAD_DIM))
            xs, sre, sim, ko, vo = _mixer_sample(
                xs, state_ssm_re[idx].astype(F32).reshape(n_seq, g_ssm * p_ssm),
                state_ssm_im[idx].astype(F32).reshape(n_seq, g_ssm * p_ssm),
                cache_swa_k[idx].astype(F32).reshape(n_seq, w_rows, d_kv),
                cache_swa_v[idx].astype(F32).reshape(n_seq, w_rows, d_kv), idx, layer, wts, ns)
            sr_s.append(sre.reshape(n_seq, g_ssm, p_ssm))
            si_s.append(sim.reshape(n_seq, g_ssm, p_ssm))
            k_s.append(ko.reshape(n_seq, w_rows, N_KV_HEADS, HEAD_DIM))
            v_s.append(vo.reshape(n_seq, w_rows, N_KV_HEADS, HEAD_DIM))
        else:
            xp, st = _conformer(xp, None, idx, layer, wts, tm, pad_rows)
            c_p.append(st)
            xs, st = _conformer(xs, state_conv[idx].astype(F32), idx, layer, wts, ns, 0)
            c_s.append(st)
        last = layer == depth - 1
        xp, st = _ffn(xp, None, layer, wts, tm, last)
        f_p.append(st)
        xs, st = _ffn(xs, state_ffn[layer].astype(F32), layer, wts, ns, last)
        f_s.append(st)

    yp = xp[:, pad_rows + n_meta:]
    ys = xs.reshape(n_seq, tq, dm)
    st = jnp.stack
    return (yp, ys, st(sr_p), st(si_p), st(k_p), st(v_p), st(c_p), st(f_p),
            st(sr_s), st(si_s), st(k_s), st(v_s), st(c_s), st(f_s))
```

```python
import functools
import math

import numpy as np
import jax
import jax.numpy as jnp
from jax import lax
from jax.experimental import pallas as pl
from jax.experimental.pallas import tpu as pltpu

F32 = jnp.float32
BF16 = jnp.bfloat16

EPS = 1e-6
NEG_INF = -1e30
WINDOW = 128
HEAD_DIM = 64
N_KV_HEADS = 2
N_BUCKETS = 32
MAX_DISTANCE = 128
SSM_GROUP = 16
SSM_STATE = 64
SSM_BLK_GROUPS = 8
SUBLANES = 8
CONV_HALO = 32
FFN_HALO = 8
FFN_CHUNKS = 1
LANES = 128
CONV_PHASES = 4
CONV_PHASE_VREGS = 4
CONV_SLABS = 2
CONF_BLOCK = 128
PROMPT_TILE_TARGET = 768
SAMPLE_SEQS = 32
VMEM_LIMIT = 56 * 1024 * 1024


def _cparams(sem):
    return pltpu.CompilerParams(dimension_semantics=sem, vmem_limit_bytes=VMEM_LIMIT)


def _const_spec(shape):
    nd = len(shape)
    return pl.BlockSpec(shape, lambda *_: (0,) * nd, pipeline_mode=pl.Buffered(1))


def _rmsnorm(x, g):
    return x * lax.rsqrt(jnp.mean(x * x, axis=-1, keepdims=True) + EPS) * g


def _dot(a, b):
    return jnp.dot(a, b, preferred_element_type=F32)


def _ssm_prep_kernel(lr_ref, li_ref, ls_ref, bre_ref, bim_ref, cre_ref, cim_ref,
                     tre_ref, tim_ref, kst_ref, bst_ref, mst_ref):
    lr = lr_ref[0]
    li = li_ref[0]
    dt = jnp.exp(ls_ref[0])
    decay = jnp.exp(lr * dt)
    a_re = decay * jnp.cos(li * dt)
    a_im = decay * jnp.sin(li * dt)
    den = lr * lr + li * li
    num_re = a_re - 1.0
    coef_re = (num_re * lr + a_im * li) / den
    coef_im = (a_im * lr - num_re * li) / den

    cmul = lambda x, y: (x[0] * y[0] - x[1] * y[1], x[0] * y[1] + x[1] * y[0])
    n = lr.shape[-1]
    pw = [(jnp.ones((1, n), F32), jnp.zeros((1, n), F32)), (a_re, a_im)]
    for _ in range(SUBLANES - 1):
        pw.append(cmul(pw[-1], (a_re, a_im)))
    apw = [pw[SUBLANES]]
    for _ in range(SUBLANES - 1):
        apw.append(cmul(apw[-1], pw[SUBLANES]))
    row = lax.broadcasted_iota(jnp.int32, (SUBLANES, n), 0)
    zero = jnp.zeros((SUBLANES, n), F32)
    for t, k in enumerate((1, 2, 4)):
        tre_ref[0, t] = jnp.where(row >= k, jnp.broadcast_to(apw[k - 1][0], (SUBLANES, n)), zero)
        tim_ref[0, t] = jnp.where(row >= k, jnp.broadcast_to(apw[k - 1][1], (SUBLANES, n)), zero)
    ap_re, ap_im = zero, zero
    for j in range(SUBLANES):
        ap_re = jnp.where(row == j, jnp.broadcast_to(apw[j][0], (SUBLANES, n)), ap_re)
        ap_im = jnp.where(row == j, jnp.broadcast_to(apw[j][1], (SUBLANES, n)), ap_im)
    tre_ref[0, 3] = ap_re
    tim_ref[0, 3] = ap_im

    bre = bre_ref[0]
    bim = bim_ref[0]
    bbar = (coef_re * bre - coef_im * bim, coef_re * bim + coef_im * bre)
    rows = SSM_BLK_GROUPS * SSM_GROUP
    r_i = lax.broadcasted_iota(jnp.int32, (rows, n), 0)
    c_i = lax.broadcasted_iota(jnp.int32, (rows, n), 1)
    sh = lambda v, d: lax.shift_right_logical(v, jnp.full(v.shape, int(math.log2(d)), jnp.int32))
    diag = sh(r_i, SSM_GROUP) == (sh(c_i, SSM_STATE) & (SSM_BLK_GROUPS - 1))
    zer = jnp.zeros((rows, n), F32)
    blockdiag = lambda v: jnp.where(diag, jnp.concatenate([v] * SSM_BLK_GROUPS, axis=0), zer)
    bb = (blockdiag(bbar[0]), blockdiag(bbar[1]))
    cc = (blockdiag(cre_ref[0]), blockdiag(cim_ref[0]))
    w = SSM_BLK_GROUPS * SSM_STATE
    nblk = n // w
    for d in range(SUBLANES):
        ba = cmul(bb, pw[d])
        ca = cmul(cc, pw[d + 1])
        for blk in range(nblk):
            sl = slice(blk * w, (blk + 1) * w)
            rs = slice(d * rows, (d + 1) * rows)
            bcat = jnp.concatenate([ba[0][:, sl], ba[1][:, sl]], axis=1)
            ccat = jnp.concatenate([cc[0][:, sl], -cc[1][:, sl]], axis=1)
            bst_ref[0, blk, rs, :] = bcat.astype(BF16)
            kst_ref[0, blk, rs, :] = lax.dot_general(
                bcat, ccat, (((1,), (1,)), ((), ())), precision=lax.Precision.HIGHEST,
                preferred_element_type=F32).astype(BF16)
            mcat = jnp.concatenate([ca[0][:, sl], -ca[1][:, sl]], axis=1)
            mst_ref[0, blk, :, rs] = mcat.T.astype(BF16)


def _ssm_prep(lam_re, lam_im, log_step, b_re, b_im, c_re, c_im):
    ne, g, p = lam_re.shape
    n = g * p
    assert g % SSM_BLK_GROUPS == 0 and p == SSM_STATE and b_re.shape[-1] == SSM_GROUP
    nblk = g // SSM_BLK_GROUPS
    rows = SSM_BLK_GROUPS * SSM_GROUP
    w = SSM_BLK_GROUPS * SSM_STATE
    flat = lambda a: a.reshape(ne, 1, n)
    ls = jnp.broadcast_to(log_step[:, :, None], (ne, g, p))
    bt = lambda a: jnp.transpose(a, (0, 3, 1, 2)).reshape(ne, SSM_GROUP, n)
    ct = lambda a: jnp.transpose(a, (0, 2, 1, 3)).reshape(ne, SSM_GROUP, n)
    vec = pl.BlockSpec((1, 1, n), lambda i: (i, 0, 0))
    mat = pl.BlockSpec((1, SSM_GROUP, n), lambda i: (i, 0, 0))
    tab = pl.BlockSpec((1, 4, SUBLANES, n), lambda i: (i, 0, 0, 0))
    stk = lambda cols: pl.BlockSpec((1, nblk, SUBLANES * rows, cols), lambda i: (i, 0, 0, 0))
    mspec = pl.BlockSpec((1, nblk, 2 * w, SUBLANES * rows), lambda i: (i, 0, 0, 0))
    return pl.pallas_call(
        _ssm_prep_kernel,
        grid=(ne,),
        in_specs=[vec, vec, vec, mat, mat, mat, mat],
        out_specs=[tab, tab, stk(rows), stk(2 * w), mspec],
        out_shape=[jax.ShapeDtypeStruct((ne, 4, SUBLANES, n), F32)] * 2
        + [jax.ShapeDtypeStruct((ne, nblk, SUBLANES * rows, rows), BF16),
           jax.ShapeDtypeStruct((ne, nblk, SUBLANES * rows, 2 * w), BF16),
           jax.ShapeDtypeStruct((ne, nblk, 2 * w, SUBLANES * rows), BF16)],
        compiler_params=pltpu.CompilerParams(vmem_limit_bytes=VMEM_LIMIT),
        name="ssm_prep",
    )(flat(lam_re), flat(lam_im), flat(ls), bt(b_re), bt(b_im), ct(c_re), ct(c_im))


def _bucket_table():
    i = np.arange(WINDOW)[:, None]
    j = np.arange(2 * WINDOW)[None, :]
    dist = WINDOW + i - j
    nn = np.maximum(dist, 0)
    max_exact = N_BUCKETS // 2
    nf = np.maximum(nn, max_exact).astype(np.float32)
    large = max_exact + (np.log(nf / np.float32(max_exact)) / np.float32(math.log(MAX_DISTANCE / max_exact))
                         * np.float32(N_BUCKETS - max_exact)).astype(np.int32)
    large = np.minimum(large, N_BUCKETS - 1)
    bucket = np.where(nn < max_exact, nn, large)
    valid = (dist >= 0) & (dist < WINDOW)
    return np.where(valid, bucket, -1).astype(np.int32)


def _bias_kernel(rb_ref, bucket_ref, out_ref):
    bucket = bucket_ref[...]
    nh = out_ref.shape[0]
    for h in range(nh):
        acc = jnp.full(bucket.shape, NEG_INF, F32)
        for b in range(N_BUCKETS):
            acc = jnp.where(bucket == b, rb_ref[b, h], acc)
        out_ref[h] = acc


def _bias_table(rel_bias):
    nh = rel_bias.shape[1]
    bucket = jnp.asarray(_bucket_table())
    return pl.pallas_call(
        _bias_kernel,
        in_specs=[pl.BlockSpec(memory_space=pltpu.SMEM),
                  pl.BlockSpec(bucket.shape, lambda: (0, 0))],
        out_specs=pl.BlockSpec((nh,) + bucket.shape, lambda: (0, 0, 0)),
        out_shape=jax.ShapeDtypeStruct((nh,) + bucket.shape, F32),
        name="rel_bias_table",
    )(rel_bias.astype(F32), bucket)


def _block_state_scan(sbuf, tre_ref, tim_ref, lane0, w, carry):
    lanes = pl.ds(lane0, w)
    first = lax.broadcasted_iota(jnp.int32, (SUBLANES, w), 0) == 0

    def body(b, carry):
        hr, hi = carry
        r0 = pl.multiple_of(b * SUBLANES, SUBLANES)
        xr = sbuf[pl.ds(r0, SUBLANES), 0:w]
        xi = sbuf[pl.ds(r0, SUBLANES), w:2 * w]
        for t, k in enumerate((1, 2, 4)):
            cr = tre_ref[0, t, :, lanes]
            ci = tim_ref[0, t, :, lanes]
            rr = pltpu.roll(xr, k, 0)
            ri = pltpu.roll(xi, k, 0)
            xr, xi = xr + cr * rr - ci * ri, xi + cr * ri + ci * rr
        pr = tre_ref[0, 3, :, lanes]
        pi = tim_ref[0, 3, :, lanes]
        xr, xi = xr + pr * hr - pi * hi, xi + pr * hi + pi * hr
        sbuf[pl.ds(r0, SUBLANES), 0:w] = jnp.where(first, hr, pltpu.roll(xr, 1, 0))
        sbuf[pl.ds(r0, SUBLANES), w:2 * w] = jnp.where(first, hi, pltpu.roll(xi, 1, 0))
        return (jnp.broadcast_to(xr[SUBLANES - 1:SUBLANES, :], (SUBLANES, w)),
                jnp.broadcast_to(xi[SUBLANES - 1:SUBLANES, :], (SUBLANES, w)))

    return lax.fori_loop(0, sbuf.shape[0] // SUBLANES, body, carry)


def _s5_mix(u, uext, yint, kst_ref, bst_ref, mst_ref, d, wglu_ref, bglu, advance):
    nblk = kst_ref.shape[1]
    cw = kst_ref.shape[3]
    rows = u.shape[0]
    nb = rows // SUBLANES
    j_i = lax.broadcasted_iota(jnp.int32, (1, SUBLANES, 1), 1)
    ys = []
    for blk in range(nblk):
        ub = u[:, blk * cw:(blk + 1) * cw]
        u3 = ub.reshape(nb, SUBLANES, cw)
        parts = [ub]
        for dd in range(1, SUBLANES):
            parts.append(jnp.where(j_i >= dd, pltpu.roll(u3, dd, 1), 0.0).reshape(rows, cw))
        for dd in range(SUBLANES):
            uext[dd] = parts[dd]
        y_intra = _dot(jnp.concatenate(parts, axis=1).astype(BF16), kst_ref[0, blk])
        last = [uext[dd, pl.ds(SUBLANES - 1, nb, stride=SUBLANES), :] for dd in range(SUBLANES)]
        s = _dot(jnp.concatenate(last, axis=1).astype(BF16), bst_ref[0, blk])
        hprev = advance(blk, s)
        yi = _dot(hprev.astype(BF16), mst_ref[0, blk])
        for j in range(SUBLANES):
            yint[pl.ds(j, nb, stride=SUBLANES), :] = yi[:, j * cw:(j + 1) * cw]
        ys.append(y_intra + yint[...])
    y = jnp.concatenate(ys, axis=1) + d * u
    g = jax.nn.gelu(y)
    return g * jax.nn.sigmoid(_dot(g.astype(BF16), wglu_ref[0]) + bglu)


def _mixer_prompt_kernel(x_ref, g_ref, wukv_ref, wqt_ref, wvt_ref, kst_ref, bst_ref, mst_ref, tre_ref, tim_ref,
                         d_ref, wglu_ref, bglu_ref, bias_ref, sink_ref, wout_ref,
                         o_ref, sre_ref, sim_ref, ko_ref, vo_ref,
                         uext, yint, sbuf, hst, kbuf, vtbuf, mix, sc, pb):
    t = pl.program_id(1)
    tm = x_ref.shape[1]
    d_ssm = d_ref.shape[-1]
    d_kv = N_KV_HEADS * HEAD_DIM
    gq = bias_ref.shape[3] // WINDOW
    nt_dims = (((1,), (1,)), ((), ()))

    @pl.when(t == 0)
    def _():
        hst[...] = jnp.zeros_like(hst)
        kbuf[0:WINDOW, :] = jnp.zeros((WINDOW, d_kv), F32)
        vtbuf[:, 0:WINDOW] = jnp.zeros((d_kv, WINDOW), F32)

    x = x_ref[0]
    h = _rmsnorm(x, g_ref[0]).astype(BF16)
    z = _dot(h, wukv_ref[0])
    u = z[:, :d_ssm]
    kbuf[WINDOW:, :] = z[:, d_ssm:d_ssm + d_kv]
    v = z[:, d_ssm + d_kv:]
    qt = lax.dot_general(wqt_ref[0], h, nt_dims, preferred_element_type=F32) * (HEAD_DIM ** -0.5)
    vtbuf[:, WINDOW:] = lax.dot_general(wvt_ref[0], h, nt_dims, preferred_element_type=F32)

    n_state = hst.shape[1] // 2

    w = sbuf.shape[1] // 2

    def advance(blk, s):
        lane0 = blk * w
        sbuf[...] = s
        init = (hst[:, lane0:lane0 + w], hst[:, n_state + lane0:n_state + lane0 + w])
        hr, hi = _block_state_scan(sbuf, tre_ref, tim_ref, lane0, w, init)
        hst[:, lane0:lane0 + w] = hr
        hst[:, n_state + lane0:n_state + lane0 + w] = hi
        return sbuf[...]

    ya = _s5_mix(u, uext, yint, kst_ref, bst_ref, mst_ref, d_ref[0], wglu_ref, bglu_ref[0], advance)
    mix[:, 0:d_ssm] = ya.astype(BF16)
    sre_ref[0] = hst[0:1, 0:n_state]
    sim_ref[0] = hst[0:1, n_state:]

    nqb = tm // WINDOW
    for qb in range(nqb):
        c0 = qb * WINDOW
        var = jnp.where(t == 0, qb + 1, 0) if qb + 1 < bias_ref.shape[0] else 0
        for j in range(N_KV_HEADS):
            kb = kbuf[c0:c0 + 2 * WINDOW, j * HEAD_DIM:(j + 1) * HEAD_DIM].astype(BF16)
            qc = jnp.concatenate(
                [qt[(j * gq + g) * HEAD_DIM:(j * gq + g + 1) * HEAD_DIM, c0:c0 + WINDOW] for g in range(gq)],
                axis=1).astype(BF16)
            sc[j, qb] = _dot(kb, qc) + bias_ref[var, j]
    dens = []
    for j in range(N_KV_HEADS):
        s = sc[j]
        sink = sink_ref[0, j][None]
        m = jnp.maximum(jnp.max(s, axis=1, keepdims=True), sink)
        p = jnp.exp(s - m)
        dens.append(jnp.sum(p, axis=1, keepdims=True) + jnp.exp(sink - m))
        pb[j] = p.astype(BF16)
    for qb in range(nqb):
        c0 = qb * WINDOW
        pieces = []
        for j in range(N_KV_HEADS):
            vb = vtbuf[j * HEAD_DIM:(j + 1) * HEAD_DIM, c0:c0 + 2 * WINDOW].astype(BF16)
            o = _dot(vb, pb[j, qb]) / dens[j][qb]
            pieces += [o[:, g * WINDOW:(g + 1) * WINDOW] for g in range(gq)]
        ot = jnp.concatenate(pieces, axis=0)
        mix[c0:c0 + WINDOW, d_ssm:] = ot.T.astype(BF16)

    kbuf[0:WINDOW, :] = kbuf[tm:tm + WINDOW, :]
    vtbuf[:, 0:WINDOW] = vtbuf[:, tm:tm + WINDOW]
    ko_ref[0] = kbuf[0:WINDOW, :]
    vo_ref[0] = v[tm - WINDOW:, :]
    o_ref[0] = x + _dot(mix[...], wout_ref[0])


def _mixer_prompt(x, idx, layer, wts, tm):
    nb, tp, dm = x.shape
    nt = tp // tm
    n_state = wts["tre"].shape[-1]
    d_ssm = wts["ssm_d"].shape[-1]
    d_kv = N_KV_HEADS * HEAD_DIM
    w = wts["bst"].shape[-1] // 2
    cw = wts["kst"].shape[-1]
    layer_spec = lambda a, i: pl.BlockSpec((1,) + a.shape[1:], lambda b, t: (i,) + (0,) * (a.ndim - 1),
                                           pipeline_mode=pl.Buffered(1))
    ins = [
        (x, pl.BlockSpec((1, tm, dm), lambda b, t: (b, t, 0))),
        (wts["g_mix"], layer_spec(wts["g_mix"], layer)),
        (wts["w_ukv"], layer_spec(wts["w_ukv"], idx)),
        (wts["w_qt"], layer_spec(wts["w_qt"], idx)),
        (wts["w_vt"], layer_spec(wts["w_vt"], idx)),
        (wts["kst"], layer_spec(wts["kst"], idx)),
        (wts["bst"], layer_spec(wts["bst"], idx)),
        (wts["mst"], layer_spec(wts["mst"], idx)),
        (wts["tre"], layer_spec(wts["tre"], idx)),
        (wts["tim"], layer_spec(wts["tim"], idx)),
        (wts["ssm_d"], layer_spec(wts["ssm_d"], idx)),
        (wts["w_glu"], layer_spec(wts["w_glu"], idx)),
        (wts["b_glu"], layer_spec(wts["b_glu"], idx)),
        (wts["bias_p"], _const_spec(wts["bias_p"].shape)),
        (wts["sink_p"], layer_spec(wts["sink_p"], idx)),
        (wts["w_out"], layer_spec(wts["w_out"], idx)),
    ]
    out_shape = [
        jax.ShapeDtypeStruct((nb, tp, dm), F32),
        jax.ShapeDtypeStruct((nb, 1, n_state), F32),
        jax.ShapeDtypeStruct((nb, 1, n_state), F32),
        jax.ShapeDtypeStruct((nb, WINDOW, d_kv), F32),
        jax.ShapeDtypeStruct((nb, WINDOW, d_kv), F32),
    ]
    out_specs = [
        pl.BlockSpec((1, tm, dm), lambda b, t: (b, t, 0)),
        pl.BlockSpec((1, 1, n_state), lambda b, t: (b, 0, 0)),
        pl.BlockSpec((1, 1, n_state), lambda b, t: (b, 0, 0)),
        pl.BlockSpec((1, WINDOW, d_kv), lambda b, t: (b, 0, 0)),
        pl.BlockSpec((1, WINDOW, d_kv), lambda b, t: (b, 0, 0)),
    ]
    scratch = [
        pltpu.VMEM((SUBLANES, tm, cw), F32),
        pltpu.VMEM((tm, cw), F32),
        pltpu.VMEM((tm // SUBLANES, 2 * w), F32),
        pltpu.VMEM((SUBLANES, 2 * n_state), F32),
        pltpu.VMEM((tm + WINDOW, d_kv), F32),
        pltpu.VMEM((d_kv, tm + WINDOW), F32),
        pltpu.VMEM((tm, dm), BF16),
        pltpu.VMEM((N_KV_HEADS, tm // WINDOW) + wts["bias_p"].shape[2:], F32),
        pltpu.VMEM((N_KV_HEADS, tm // WINDOW) + wts["bias_p"].shape[2:], BF16),
    ]
    assert tm // WINDOW >= wts["bias_p"].shape[0] - 1
    return pl.pallas_call(
        _mixer_prompt_kernel,
        grid=(nb, nt),
        in_specs=[s for _, s in ins],
        out_specs=out_specs,
        out_shape=out_shape,
        scratch_shapes=scratch,
        compiler_params=_cparams(("arbitrary", "arbitrary")),
        name=f"mixer_prompt_l{layer}",
    )(*[a for a, _ in ins])


def _mixer_sample_kernel(x_ref, h0r_ref, h0i_ref, kc_ref, vc_ref, g_ref, win_ref, kst_ref, bst_ref, mst_ref,
                         tre_ref, tim_ref, d_ref, wglu_ref, bglu_ref, bias_ref, sink_ref, wout_ref,
                         o_ref, sre_ref, sim_ref, ko_ref, vo_ref, uext, yint):
    rows = x_ref.shape[0]
    ns = kc_ref.shape[0]
    tq = rows // ns
    assert tq == SUBLANES
    w_rows = kc_ref.shape[1]
    d_ssm = d_ref.shape[-1]
    gq = bias_ref.shape[1] // tq
    d_q = N_KV_HEADS * gq * HEAD_DIM
    d_kv = N_KV_HEADS * HEAD_DIM

    x = x_ref[...]
    h = _rmsnorm(x, g_ref[0]).astype(BF16)
    z = _dot(h, win_ref[0])
    u = z[:, :d_ssm]
    q = (z[:, d_ssm:d_ssm + d_q] * (HEAD_DIM ** -0.5)).reshape(ns, tq, d_q)
    kn = z[:, d_ssm + d_q:d_ssm + d_q + d_kv].reshape(ns, tq, d_kv)
    vn = z[:, d_ssm + d_q + d_kv:].reshape(ns, tq, d_kv)

    w = bst_ref.shape[-1] // 2

    def advance(blk, s):
        lanes = slice(blk * w, (blk + 1) * w)
        hr = h0r_ref[:, lanes]
        hi = h0i_ref[:, lanes]
        ar = tre_ref[0, 0, SUBLANES - 1:SUBLANES, lanes]
        ai = tim_ref[0, 0, SUBLANES - 1:SUBLANES, lanes]
        sre_ref[:, lanes] = ar * hr - ai * hi + s[:, 0:w]
        sim_ref[:, lanes] = ar * hi + ai * hr + s[:, w:2 * w]
        return jnp.concatenate([hr, hi], axis=1)

    ya = _s5_mix(u, uext, yint, kst_ref, bst_ref, mst_ref, d_ref[0], wglu_ref, bglu_ref[0], advance)

    kc = jnp.concatenate([kc_ref[...], kn], axis=1)
    vc = jnp.concatenate([vc_ref[...], vn], axis=1)
    ko_ref[...] = kc[:, tq:, :]
    vo_ref[...] = vc[:, tq:, :]
    pieces = []
    for j in range(N_KV_HEADS):
        kb = kc[:, :, j * HEAD_DIM:(j + 1) * HEAD_DIM].astype(BF16)
        vb = vc[:, :, j * HEAD_DIM:(j + 1) * HEAD_DIM].astype(BF16)
        qs = jnp.concatenate(
            [q[:, :, (j * gq + g) * HEAD_DIM:(j * gq + g + 1) * HEAD_DIM] for g in range(gq)],
            axis=1).astype(BF16)
        s = jnp.einsum("nqd,nkd->nqk", qs, kb, preferred_element_type=F32) + bias_ref[j]
        sink = sink_ref[0, j]
        m = jnp.maximum(jnp.max(s, axis=-1, keepdims=True), sink)
        p = jnp.exp(s - m)
        l = jnp.sum(p, axis=-1, keepdims=True) + jnp.exp(sink - m)
        o = jnp.einsum("nqk,nkd->nqd", p.astype(BF16), vb, preferred_element_type=F32) / l
        pieces += [o[:, g * tq:(g + 1) * tq, :] for g in range(gq)]
    yb = jnp.concatenate(pieces, axis=2).reshape(rows, d_q)
    mixed = jnp.concatenate([ya, yb], axis=1).astype(BF16)
    o_ref[...] = x + _dot(mixed, wout_ref[0])


def _mixer_sample(x2, h0r, h0i, kc, vc, idx, layer, wts, ns):
    rows_all, dm = x2.shape
    n_seq, w_rows, d_kv = kc.shape
    tq = rows_all // n_seq
    n_state = h0r.shape[-1]
    cw = wts["kst"].shape[-1]
    layer_spec = lambda a, i: pl.BlockSpec((1,) + a.shape[1:], lambda s: (i,) + (0,) * (a.ndim - 1),
                                           pipeline_mode=pl.Buffered(1))
    row_spec = pl.BlockSpec((ns * tq, dm), lambda s: (s, 0))
    st_spec = pl.BlockSpec((ns, n_state), lambda s: (s, 0))
    kv_spec = pl.BlockSpec((ns, w_rows, d_kv), lambda s: (s, 0, 0))
    ins = [
        (x2, row_spec), (h0r, st_spec), (h0i, st_spec), (kc, kv_spec), (vc, kv_spec),
        (wts["g_mix"], layer_spec(wts["g_mix"], layer)),
        (wts["w_in"], layer_spec(wts["w_in"], idx)),
        (wts["kst"], layer_spec(wts["kst"], idx)),
        (wts["bst"], layer_spec(wts["bst"], idx)),
        (wts["mst"], layer_spec(wts["mst"], idx)),
        (wts["tre"], layer_spec(wts["tre"], idx)),
        (wts["tim"], layer_spec(wts["tim"], idx)),
        (wts["ssm_d"], layer_spec(wts["ssm_d"], idx)),
        (wts["w_glu"], layer_spec(wts["w_glu"], idx)),
        (wts["b_glu"], layer_spec(wts["b_glu"], idx)),
        (wts["bias_s"], _const_spec(wts["bias_s"].shape)),
        (wts["sink_s"], layer_spec(wts["sink_s"], idx)),
        (wts["w_out"], layer_spec(wts["w_out"], idx)),
    ]
    return pl.pallas_call(
        _mixer_sample_kernel,
        grid=(n_seq // ns,),
        in_specs=[s for _, s in ins],
        out_specs=[row_spec, st_spec, st_spec, kv_spec, kv_spec],
        out_shape=[jax.ShapeDtypeStruct(x2.shape, F32), jax.ShapeDtypeStruct(h0r.shape, F32),
                   jax.ShapeDtypeStruct(h0i.shape, F32), jax.ShapeDtypeStruct(kc.shape, F32),
                   jax.ShapeDtypeStruct(vc.shape, F32)],
        scratch_shapes=[pltpu.VMEM((SUBLANES, ns * tq, cw), F32), pltpu.VMEM((ns * tq, cw), F32)],
        compiler_params=_cparams(("arbitrary",)),
        name=f"mixer_sample_l{layer}",
    )(*[a for a, _ in ins])


def _conformer_head(x, g_ref, w1_ref, c):
    h = _rmsnorm(x, g_ref[0]).astype(BF16)
    z = _dot(h, w1_ref[0])
    return z[:, :c] * jax.nn.sigmoid(z[:, c:])


def _conformer_tail(y, bdw_ref, lng_ref, lnb_ref, w2_ref):
    y = y + bdw_ref[0]
    mu = jnp.mean(y, axis=-1, keepdims=True)
    yc = y - mu
    var = jnp.mean(yc * yc, axis=-1, keepdims=True)
    y = yc * lax.rsqrt(var + EPS) * lng_ref[0] + lnb_ref[0]
    return _dot(jax.nn.silu(y).astype(BF16), w2_ref[0])


def _conformer_prompt_kernel(x_ref, g_ref, w1_ref, wdw_ref, bdw_ref, lng_ref, lnb_ref, w2_ref, o_ref, st_ref,
                             ext, ybuf, *, pad_rows, width):
    t = pl.program_id(1)
    x = x_ref[0]
    nslab, le, lanes = ext.shape
    l = le - CONV_HALO
    c = nslab * lanes
    off = CONV_HALO - (width - 1)
    mrows = CONV_PHASE_VREGS * SUBLANES
    span = CONV_PHASES * mrows
    assert l % CONF_BLOCK == 0 and CONF_BLOCK % span == 0

    @pl.when(t == 0)
    def _():
        ext[:, 0:CONV_HALO, :] = jnp.zeros((nslab, CONV_HALO, lanes), F32)

    gl = _conformer_head(x, g_ref, w1_ref, c)
    for q in range(nslab):
        ext[q, CONV_HALO:, :] = gl[:, q * lanes:(q + 1) * lanes]
    st_ref[0] = jnp.concatenate([ext[q, l + off:, :] for q in range(nslab)], axis=1)

    def block(cb, _):
        r0 = pl.multiple_of(cb * CONF_BLOCK, CONF_BLOCK)
        for grp in range(CONF_BLOCK // span):
            g0 = r0 + grp * span
            for q0 in range(0, nslab, CONV_SLABS):
                slabs = range(q0, q0 + CONV_SLABS)
                acc = {(r, q): jnp.zeros((CONV_PHASE_VREGS, SUBLANES, lanes), F32)
                       for r in range(CONV_PHASES) for q in slabs}
                for s in range(width + CONV_PHASES - 1):
                    for q in slabs:
                        xs = ext[q, pl.ds(g0 + (off + s), mrows, stride=CONV_PHASES), :]
                        xs = xs.reshape(CONV_PHASE_VREGS, SUBLANES, lanes)
                        for r in range(CONV_PHASES):
                            k = s - r
                            if 0 <= k < width:
                                wk = wdw_ref[0, k, :, q * lanes:(q + 1) * lanes]
                                acc[r, q] = acc[r, q] + wk[None] * xs
                for r in range(CONV_PHASES):
                    for q in slabs:
                        ybuf[q, pl.ds(g0 + r, mrows, stride=CONV_PHASES), :] = acc[r, q].reshape(mrows, lanes)
        y = jnp.concatenate([ybuf[q, pl.ds(r0, CONF_BLOCK), :] for q in range(nslab)], axis=1)
        out = _conformer_tail(y, bdw_ref, lng_ref, lnb_ref, w2_ref)
        rid = t * l + r0 + lax.broadcasted_iota(jnp.int32, (CONF_BLOCK, 1), 0)
        o_ref[0, pl.ds(r0, CONF_BLOCK), :] = x_ref[0, pl.ds(r0, CONF_BLOCK), :] + jnp.where(rid >= pad_rows, out, 0.0)
        return 0

    lax.fori_loop(0, l // CONF_BLOCK, block, 0)
    ext[:, 0:CONV_HALO, :] = ext[:, l:l + CONV_HALO, :]


def _conformer_sample_kernel(x_ref, past_ref, g_ref, w1_ref, wdw_ref, bdw_ref, lng_ref, lnb_ref, w2_ref,
                             o_ref, st_ref, ext, ybuf, stage, *, width):
    x = x_ref[...]
    ns, le, c = ext.shape
    l = le - CONV_HALO
    off = CONV_HALO - (width - 1)
    ext[:, 0:off, :] = jnp.zeros((ns, off, c), F32)
    ext[:, off:CONV_HALO, :] = past_ref[...]
    ext[:, CONV_HALO:, :] = _conformer_head(x, g_ref, w1_ref, c).reshape(ns, l, c)
    st_ref[...] = ext[:, l + off:, :]

    def conv_unit(n, stg):
        win = ext[n]
        for s in range(1, SUBLANES):
            stg[s - 1] = win[s:s + l + CONV_HALO - SUBLANES]
        acc = jnp.zeros((l // SUBLANES, SUBLANES, c), F32)
        for k in range(width):
            a, s = divmod(off + k, SUBLANES)
            if s == 0:
                xk = ext[n, a * SUBLANES:a * SUBLANES + l, :]
            else:
                xk = stg[s - 1, a * SUBLANES:a * SUBLANES + l, :]
            acc = acc + wdw_ref[0, k][None] * xk.reshape(l // SUBLANES, SUBLANES, c)
        return acc.reshape(l, c)

    def seq(i, _):
        for half in range(2):
            n = 2 * i + half
            ybuf[pl.ds(pl.multiple_of(n * l, l), l), :] = conv_unit(n, stage.at[half])
        return 0
    assert ns % 2 == 0 and l % SUBLANES == 0
    lax.fori_loop(0, ns // 2, seq, 0)
    o_ref[...] = x + _conformer_tail(ybuf[...], bdw_ref, lng_ref, lnb_ref, w2_ref)


def _conformer(x, past, idx, layer, wts, tile, pad_rows):
    carry = past is None
    width = wts["w_dw"].shape[1]
    c = wts["w_dw"].shape[3]
    if carry:
        nb, tp, dm = x.shape
        grid = (nb, tp // tile)
        lspec = lambda a, i: pl.BlockSpec((1,) + a.shape[1:], lambda b, t: (i,) + (0,) * (a.ndim - 1),
                                          pipeline_mode=pl.Buffered(1))
        xspec = pl.BlockSpec((1, tile, dm), lambda b, t: (b, t, 0))
        ins = [(x, xspec)]
        st_shape = (nb, width - 1, c)
        st_spec = pl.BlockSpec((1, width - 1, c), lambda b, t: (b, 0, 0))
        ns, l = 1, tile
        sem = ("arbitrary", "arbitrary")
    else:
        rows_all, dm = x.shape
        n_seq = past.shape[0]
        l = rows_all // n_seq
        ns = tile
        grid = (n_seq // ns,)
        lspec = lambda a, i: pl.BlockSpec((1,) + a.shape[1:], lambda s: (i,) + (0,) * (a.ndim - 1),
                                          pipeline_mode=pl.Buffered(1))
        xspec = pl.BlockSpec((ns * l, dm), lambda s: (s, 0))
        ins = [(x, xspec), (past, pl.BlockSpec((ns, width - 1, c), lambda s: (s, 0, 0)))]
        st_shape = (n_seq, width - 1, c)
        st_spec = pl.BlockSpec((ns, width - 1, c), lambda s: (s, 0, 0))
        sem = ("arbitrary",)
    ins += [
        (wts["g_mix"], lspec(wts["g_mix"], layer)),
        (wts["w_pw1"], lspec(wts["w_pw1"], idx)),
        (wts["w_dw"], lspec(wts["w_dw"], idx)),
        (wts["b_dw"], lspec(wts["b_dw"], idx)),
        (wts["ln_g"], lspec(wts["ln_g"], idx)),
        (wts["ln_b"], lspec(wts["ln_b"], idx)),
        (wts["w_pw2"], lspec(wts["w_pw2"], idx)),
    ]
    if carry:
        body = functools.partial(_conformer_prompt_kernel, pad_rows=pad_rows, width=width)
        scratch = [pltpu.VMEM((c // LANES, l + CONV_HALO, LANES), F32), pltpu.VMEM((c // LANES, l, LANES), F32)]
    else:
        body = functools.partial(_conformer_sample_kernel, width=width)
        scratch = [pltpu.VMEM((ns, l + CONV_HALO, c), F32), pltpu.VMEM((ns * l, c), F32),
                   pltpu.VMEM((2, SUBLANES - 1, l + CONV_HALO - SUBLANES, c), F32)]
    return pl.pallas_call(
        body,
        grid=grid,
        in_specs=[s for _, s in ins],
        out_specs=[xspec, st_spec],
        out_shape=[jax.ShapeDtypeStruct(x.shape, F32), jax.ShapeDtypeStruct(st_shape, F32)],
        scratch_shapes=scratch,
        compiler_params=_cparams(sem),
        name=f"conformer_{'prompt' if carry else 'sample'}_l{layer}",
    )(*[a for a, _ in ins])


def _ffn_kernel(*refs, carry, final_norm, n_chunks):
    if carry:
        (x_ref, g_ref, wup_ref, wcv_ref, bcv_ref, wdn_ref, gf_ref, o_ref, st_ref, ext) = refs
        past_ref = None
        t = pl.program_id(1)
        x = x_ref[0]
    else:
        (x_ref, past_ref, g_ref, wup_ref, wcv_ref, bcv_ref, wdn_ref, gf_ref, o_ref, st_ref, ext) = refs
        x = x_ref[...]
    ns, le, dff = ext.shape
    l = le - FFN_HALO
    rows = ns * l
    kw = wcv_ref.shape[1]
    ch = dff // n_chunks

    if carry:
        @pl.when(t == 0)
        def _():
            ext[:, 0:FFN_HALO, :] = jnp.zeros((ns, FFN_HALO, dff), F32)
    else:
        ext[:, FFN_HALO - (kw - 1):FFN_HALO, :] = past_ref[...]

    h = _rmsnorm(x, g_ref[0]).astype(BF16)
    acc = x
    for cix in range(n_chunks):
        c0 = cix * ch
        gate = _dot(h, wup_ref[0, :, c0:c0 + ch])
        up = _dot(h, wup_ref[0, :, dff + c0:dff + c0 + ch])
        ext[:, FFN_HALO:, c0:c0 + ch] = gate.reshape(ns, l, ch)
        gc = bcv_ref[0, :, c0:c0 + ch]
        for k in range(kw):
            o = FFN_HALO - (kw - 1) + k
            gc = gc + wcv_ref[0, k:k + 1, c0:c0 + ch] * ext[:, o:o + l, c0:c0 + ch]
        y = (jax.nn.gelu(gc).reshape(rows, ch) * up).astype(BF16)
        acc = acc + _dot(y, wdn_ref[0, c0:c0 + ch, :])
    st_ref[...] = ext[:, le - (kw - 1):, :].reshape(st_ref.shape)
    if final_norm:
        acc = _rmsnorm(acc, gf_ref[...])
    if carry:
        o_ref[0] = acc
        ext[:, 0:FFN_HALO, :] = ext[:, l:l + FFN_HALO, :]
    else:
        o_ref[...] = acc


def _ffn(x, past, layer, wts, tile, final_norm):
    carry = past is None
    dff = wts["w_dn"].shape[1]
    kw = wts["w_cv"].shape[1]
    if carry:
        nb, tp, dm = x.shape
        grid = (nb, tp // tile)
        lspec = lambda a, i: pl.BlockSpec((1,) + a.shape[1:], lambda b, t: (i,) + (0,) * (a.ndim - 1),
                                          pipeline_mode=pl.Buffered(1))
        xspec = pl.BlockSpec((1, tile, dm), lambda b, t: (b, t, 0))
        ins = [(x, xspec)]
        st_shape = (nb, kw - 1, dff)
        st_spec = pl.BlockSpec((1, kw - 1, dff), lambda b, t: (b, 0, 0))
        ns, l = 1, tile
        sem = ("arbitrary", "arbitrary")
    else:
        rows_all, dm = x.shape
        n_seq = past.shape[0]
        l = rows_all // n_seq
        ns = tile
        grid = (n_seq // ns,)
        lspec = lambda a, i: pl.BlockSpec((1,) + a.shape[1:], lambda s: (i,) + (0,) * (a.ndim - 1),
                                          pipeline_mode=pl.Buffered(1))
        xspec = pl.BlockSpec((ns * l, dm), lambda s: (s, 0))
        ins = [(x, xspec), (past, pl.BlockSpec((ns, kw - 1, dff), lambda s: (s, 0, 0)))]
        st_shape = (n_seq, kw - 1, dff)
        st_spec = pl.BlockSpec((ns, kw - 1, dff), lambda s: (s, 0, 0))
        sem = ("arbitrary",)
    ins += [
        (wts["g_ffn"], lspec(wts["g_ffn"], layer)),
        (wts["w_up"], lspec(wts["w_up"], layer)),
        (wts["w_cv"], lspec(wts["w_cv"], layer)),
        (wts["b_cv"], lspec(wts["b_cv"], layer)),
        (wts["w_dn"], lspec(wts["w_dn"], layer)),
        (wts["g_final"], _const_spec(wts["g_final"].shape)),
    ]
    n_chunks = FFN_CHUNKS if dff % (FFN_CHUNKS * LANES) == 0 else 1
    return pl.pallas_call(
        functools.partial(_ffn_kernel, carry=carry, final_norm=final_norm, n_chunks=n_chunks),
        grid=grid,
        in_specs=[s for _, s in ins],
        out_specs=[xspec, st_spec],
        out_shape=[jax.ShapeDtypeStruct(x.shape, F32), jax.ShapeDtypeStruct(st_shape, F32)],
        scratch_shapes=[pltpu.VMEM((ns, l + FFN_HALO, dff), F32)],
        compiler_params=_cparams(sem),
        name=f"ffn_{'prompt' if carry else 'sample'}_l{layer}",
    )(*[a for a, _ in ins])


def _pick_tile(total, target):
    best = WINDOW
    for m in range(1, total // WINDOW + 1):
        if total % (m * WINDOW) == 0 and m * WINDOW <= target:
            best = m * WINDOW
    return best


def kernel(x_prompt, x_sample, state_ssm_re, state_ssm_im, cache_swa_k, cache_swa_v, state_conv, state_ffn,
           meta_tokens, g_mix, g_ffn, g_final, w_in_mix, ssm_lambda_re, ssm_lambda_im, ssm_log_step,
           ssm_b_re, ssm_b_im, ssm_c_re, ssm_c_im, ssm_d, ssm_w_glu, ssm_b_glu, rel_bias, attn_sinks,
           w_out_mix, conv_w_pw1, conv_w_dw, conv_b_dw, conv_ln_g, conv_ln_b, conv_w_pw2,
           ffn_w_up, ffn_w_conv, ffn_b_conv, ffn_w_down):
    nb, seq, dm = x_prompt.shape
    n_seq, tq, _ = x_sample.shape
    depth = g_mix.shape[0]
    n_meta = meta_tokens.shape[0]
    n_even, g_ssm, p_ssm = ssm_lambda_re.shape
    n_heads = rel_bias.shape[1]
    gq = n_heads // N_KV_HEADS
    w_rows = cache_swa_k.shape[2]
    d_kv = N_KV_HEADS * HEAD_DIM
    conv_w = conv_w_dw.shape[1]
    ffn_w = ffn_w_conv.shape[1]
    assert tq == SUBLANES and w_rows == WINDOW and seq % WINDOW == 0 and n_meta <= WINDOW
    assert conv_w - 1 <= CONV_HALO and ffn_w - 1 <= FFN_HALO

    tre, tim, kst, bst, mst = _ssm_prep(ssm_lambda_re, ssm_lambda_im, ssm_log_step,
                                        ssm_b_re, ssm_b_im, ssm_c_re, ssm_c_im)
    bias = _bias_table(rel_bias)
    row3 = lambda a: a.reshape(a.shape[0], 1, a.shape[-1]).astype(F32)
    sinks = attn_sinks.astype(F32).reshape(n_even, N_KV_HEADS, gq, 1)
    pad_rows = WINDOW - n_meta
    bias_t = jnp.transpose(bias.reshape(N_KV_HEADS, gq, WINDOW, 2 * WINDOW), (0, 3, 1, 2)
                           ).reshape(N_KV_HEADS, 2 * WINDOW, gq * WINDOW)
    key_i = jnp.arange(2 * WINDOW)[None, :, None]
    n_var = 1 + -(-(pad_rows + WINDOW) // WINDOW)
    bias_p = jnp.stack([bias_t] + [jnp.where(key_i >= pad_rows + WINDOW - qb * WINDOW, bias_t, NEG_INF)
                                   for qb in range(n_var - 1)])
    d_ssm = ssm_d.shape[-1]
    d_q = n_heads * HEAD_DIM
    w_in_bf = w_in_mix.astype(BF16)
    wts = {
        "g_mix": row3(g_mix), "g_ffn": row3(g_ffn), "g_final": g_final.reshape(1, dm).astype(F32),
        "w_in": w_in_bf, "kst": kst, "bst": bst, "mst": mst, "tre": tre, "tim": tim,
        "w_ukv": jnp.concatenate([w_in_bf[:, :, :d_ssm], w_in_bf[:, :, d_ssm + d_q:]], axis=-1),
        "w_qt": jnp.transpose(w_in_bf[:, :, d_ssm:d_ssm + d_q], (0, 2, 1)),
        "w_vt": jnp.transpose(w_in_bf[:, :, d_ssm + d_q + d_kv:], (0, 2, 1)),
        "ssm_d": row3(ssm_d), "w_glu": ssm_w_glu.astype(BF16), "b_glu": row3(ssm_b_glu),
        "bias_p": bias_p,
        "bias_s": bias[:, :tq, :w_rows + tq].reshape(N_KV_HEADS, gq * tq, w_rows + tq),
        "sink_p": jnp.broadcast_to(sinks[:, :, :, None, :], (n_even, N_KV_HEADS, gq, WINDOW, 1)
                                   ).reshape(n_even, N_KV_HEADS, 1, gq * WINDOW),
        "sink_s": jnp.broadcast_to(sinks[:, :, :, None, :], (n_even, N_KV_HEADS, gq, tq, 1)
                                   ).reshape(n_even, N_KV_HEADS, gq * tq, 1),
        "w_out": w_out_mix.astype(BF16),
        "w_pw1": conv_w_pw1.astype(BF16), "w_dw": jnp.broadcast_to(conv_w_dw.astype(F32)[:, :, None, :],
                                 conv_w_dw.shape[:2] + (SUBLANES, conv_w_dw.shape[2])), "b_dw": row3(conv_b_dw),
        "ln_g": row3(conv_ln_g), "ln_b": row3(conv_ln_b), "w_pw2": conv_w_pw2.astype(BF16),
        "w_up": ffn_w_up.astype(BF16), "w_cv": ffn_w_conv.astype(F32), "b_cv": row3(ffn_b_conv),
        "w_dn": ffn_w_down.astype(BF16),
    }

    tp = pad_rows + n_meta + seq
    tm = _pick_tile(tp, PROMPT_TILE_TARGET)
    xp = jnp.concatenate([jnp.zeros((nb, pad_rows, dm), F32),
                          jnp.broadcast_to(meta_tokens.astype(F32)[None], (nb, n_meta, dm)),
                          x_prompt.astype(F32)], axis=1)
    xs = x_sample.astype(F32).reshape(n_seq * tq, dm)
    ns = min(SAMPLE_SEQS, n_seq)
    assert n_seq % ns == 0

    sr_p, si_p, k_p, v_p, c_p, f_p = [], [], [], [], [], []
    sr_s, si_s, k_s, v_s, c_s, f_s = [], [], [], [], [], []
    for layer in range(depth):
        idx = layer // 2
        if layer % 2 == 0:
            xp, sre, sim, ko, vo = _mixer_prompt(xp, idx, layer, wts, tm)
            sr_p.append(sre.reshape(nb, g_ssm, p_ssm))
            si_p.append(sim.reshape(nb, g_ssm, p_ssm))
            k_p.append(ko.reshape(nb, WINDOW, N_KV_HEADS, HEAD_DIM))
            v_p.append(vo.reshape(nb, WINDOW, N_KV_HEADS, HEAD_DIM))
            xs, sre, sim, ko, vo = _mixer_sample(
                xs, state_ssm_re[idx].astype(F32).reshape(n_seq, g_ssm * p_ssm),
                state_ssm_im[idx].astype(F32).reshape(n_seq, g_ssm * p_ssm),
                cache_swa_k[idx].astype(F32).reshape(n_seq, w_rows, d_kv),
                cache_swa_v[idx].astype(F32).reshape(n_seq, w_rows, d_kv), idx, layer, wts, ns)
            sr_s.append(sre.reshape(n_seq, g_ssm, p_ssm))
            si_s.append(sim.reshape(n_seq, g_ssm, p_ssm))
            k_s.append(ko.reshape(n_seq, w_rows, N_KV_HEADS, HEAD_DIM))
            v_s.append(vo.reshape(n_seq, w_rows, N_KV_HEADS, HEAD_DIM))
        else:
            xp, st = _conformer(xp, None, idx, layer, wts, tm, pad_rows)
            c_p.append(st)
            xs, st = _conformer(xs, state_conv[idx].astype(F32), idx, layer, wts, ns, 0)
            c_s.append(st)
        last = layer == depth - 1
        xp, st = _ffn(xp, None, layer, wts, tm, last)
        f_p.append(st)
        xs, st = _ffn(xs, state_ffn[layer].astype(F32), layer, wts, ns, last)
        f_s.append(st)

    yp = xp[:, pad_rows + n_meta:]
    ys = xs.reshape(n_seq, tq, dm)
    st = jnp.stack
    return (yp, ys, st(sr_p), st(si_p), st(k_p), st(v_p), st(c_p), st(f_p),
            st(sr_s), st(si_s), st(k_s), st(v_s), st(c_s), st(f_s))
```

```python
import functools
import math

import numpy as np
import jax
import jax.numpy as jnp
from jax import lax
from jax.experimental import pallas as pl
from jax.experimental.pallas import tpu as pltpu

F32 = jnp.float32
BF16 = jnp.bfloat16

EPS = 1e-6
NEG_INF = -1e30
WINDOW = 128
HEAD_DIM = 64
N_KV_HEADS = 2
N_BUCKETS = 32
MAX_DISTANCE = 128
SSM_GROUP = 16
SSM_STATE = 64
SSM_BLK_GROUPS = 8
SUBLANES = 8
CONV_HALO = 32
FFN_HALO = 8
LANES = 128
CONV_PHASES = 4
CONV_PHASE_VREGS = 4
CONV_SLABS = 2
CONF_BLOCK = 128
PROMPT_TILE_TARGET = 768
SAMPLE_SEQS = 32
VMEM_LIMIT = 56 * 1024 * 1024


def _cparams(sem):
    return pltpu.CompilerParams(dimension_semantics=sem, vmem_limit_bytes=VMEM_LIMIT)


def _const_spec(shape):
    nd = len(shape)
    return pl.BlockSpec(shape, lambda *_: (0,) * nd, pipeline_mode=pl.Buffered(1))


def _rmsnorm(x, g):
    return x * lax.rsqrt(jnp.mean(x * x, axis=-1, keepdims=True) + EPS) * g


def _dot(a, b):
    return jnp.dot(a, b, preferred_element_type=F32)


def _ssm_prep_kernel(lr_ref, li_ref, ls_ref, bre_ref, bim_ref, cre_ref, cim_ref,
                     tre_ref, tim_ref, kst_ref, bst_ref, mst_ref):
    lr = lr_ref[0]
    li = li_ref[0]
    dt = jnp.exp(ls_ref[0])
    decay = jnp.exp(lr * dt)
    a_re = decay * jnp.cos(li * dt)
    a_im = decay * jnp.sin(li * dt)
    den = lr * lr + li * li
    num_re = a_re - 1.0
    coef_re = (num_re * lr + a_im * li) / den
    coef_im = (a_im * lr - num_re * li) / den

    cmul = lambda x, y: (x[0] * y[0] - x[1] * y[1], x[0] * y[1] + x[1] * y[0])
    n = lr.shape[-1]
    pw = [(jnp.ones((1, n), F32), jnp.zeros((1, n), F32)), (a_re, a_im)]
    for _ in range(SUBLANES - 1):
        pw.append(cmul(pw[-1], (a_re, a_im)))
    apw = [pw[SUBLANES]]
    for _ in range(SUBLANES - 1):
        apw.append(cmul(apw[-1], pw[SUBLANES]))
    row = lax.broadcasted_iota(jnp.int32, (SUBLANES, n), 0)
    zero = jnp.zeros((SUBLANES, n), F32)
    for t, k in enumerate((1, 2, 4)):
        tre_ref[0, t] = jnp.where(row >= k, jnp.broadcast_to(apw[k - 1][0], (SUBLANES, n)), zero)
        tim_ref[0, t] = jnp.where(row >= k, jnp.broadcast_to(apw[k - 1][1], (SUBLANES, n)), zero)
    ap_re, ap_im = zero, zero
    for j in range(SUBLANES):
        ap_re = jnp.where(row == j, jnp.broadcast_to(apw[j][0], (SUBLANES, n)), ap_re)
        ap_im = jnp.where(row == j, jnp.broadcast_to(apw[j][1], (SUBLANES, n)), ap_im)
    tre_ref[0, 3] = ap_re
    tim_ref[0, 3] = ap_im

    bre = bre_ref[0]
    bim = bim_ref[0]
    bbar = (coef_re * bre - coef_im * bim, coef_re * bim + coef_im * bre)
    rows = SSM_BLK_GROUPS * SSM_GROUP
    r_i = lax.broadcasted_iota(jnp.int32, (rows, n), 0)
    c_i = lax.broadcasted_iota(jnp.int32, (rows, n), 1)
    sh = lambda v, d: lax.shift_right_logical(v, jnp.full(v.shape, int(math.log2(d)), jnp.int32))
    diag = sh(r_i, SSM_GROUP) == (sh(c_i, SSM_STATE) & (SSM_BLK_GROUPS - 1))
    zer = jnp.zeros((rows, n), F32)
    blockdiag = lambda v: jnp.where(diag, jnp.concatenate([v] * SSM_BLK_GROUPS, axis=0), zer)
    bb = (blockdiag(bbar[0]), blockdiag(bbar[1]))
    cc = (blockdiag(cre_ref[0]), blockdiag(cim_ref[0]))
    w = SSM_BLK_GROUPS * SSM_STATE
    nblk = n // w
    for d in range(SUBLANES):
        ba = cmul(bb, pw[d])
        ca = cmul(cc, pw[d + 1])
        for blk in range(nblk):
            sl = slice(blk * w, (blk + 1) * w)
            rs = slice(d * rows, (d + 1) * rows)
            bcat = jnp.concatenate([ba[0][:, sl], ba[1][:, sl]], axis=1)
            ccat = jnp.concatenate([cc[0][:, sl], -cc[1][:, sl]], axis=1)
            bst_ref[0, blk, rs, :] = bcat.astype(BF16)
            kst_ref[0, blk, rs, :] = lax.dot_general(
                bcat, ccat, (((1,), (1,)), ((), ())), precision=lax.Precision.HIGHEST,
                preferred_element_type=F32).astype(BF16)
            mcat = jnp.concatenate([ca[0][:, sl], -ca[1][:, sl]], axis=1)
            mst_ref[0, blk, :, rs] = mcat.T.astype(BF16)


def _ssm_prep(lam_re, lam_im, log_step, b_re, b_im, c_re, c_im):
    ne, g, p = lam_re.shape
    n = g * p
    assert g % SSM_BLK_GROUPS == 0 and p == SSM_STATE and b_re.shape[-1] == SSM_GROUP
    nblk = g // SSM_BLK_GROUPS
    rows = SSM_BLK_GROUPS * SSM_GROUP
    w = SSM_BLK_GROUPS * SSM_STATE
    flat = lambda a: a.reshape(ne, 1, n)
    ls = jnp.broadcast_to(log_step[:, :, None], (ne, g, p))
    bt = lambda a: jnp.transpose(a, (0, 3, 1, 2)).reshape(ne, SSM_GROUP, n)
    ct = lambda a: jnp.transpose(a, (0, 2, 1, 3)).reshape(ne, SSM_GROUP, n)
    vec = pl.BlockSpec((1, 1, n), lambda i: (i, 0, 0))
    mat = pl.BlockSpec((1, SSM_GROUP, n), lambda i: (i, 0, 0))
    tab = pl.BlockSpec((1, 4, SUBLANES, n), lambda i: (i, 0, 0, 0))
    stk = lambda cols: pl.BlockSpec((1, nblk, SUBLANES * rows, cols), lambda i: (i, 0, 0, 0))
    mspec = pl.BlockSpec((1, nblk, 2 * w, SUBLANES * rows), lambda i: (i, 0, 0, 0))
    return pl.pallas_call(
        _ssm_prep_kernel,
        grid=(ne,),
        in_specs=[vec, vec, vec, mat, mat, mat, mat],
        out_specs=[tab, tab, stk(rows), stk(2 * w), mspec],
        out_shape=[jax.ShapeDtypeStruct((ne, 4, SUBLANES, n), F32)] * 2
        + [jax.ShapeDtypeStruct((ne, nblk, SUBLANES * rows, rows), BF16),
           jax.ShapeDtypeStruct((ne, nblk, SUBLANES * rows, 2 * w), BF16),
           jax.ShapeDtypeStruct((ne, nblk, 2 * w, SUBLANES * rows), BF16)],
        compiler_params=pltpu.CompilerParams(vmem_limit_bytes=VMEM_LIMIT),
        name="ssm_prep",
    )(flat(lam_re), flat(lam_im), flat(ls), bt(b_re), bt(b_im), ct(c_re), ct(c_im))


def _bucket_table():
    i = np.arange(WINDOW)[:, None]
    j = np.arange(2 * WINDOW)[None, :]
    dist = WINDOW + i - j
    nn = np.maximum(dist, 0)
    max_exact = N_BUCKETS // 2
    nf = np.maximum(nn, max_exact).astype(np.float32)
    large = max_exact + (np.log(nf / np.float32(max_exact)) / np.float32(math.log(MAX_DISTANCE / max_exact))
                         * np.float32(N_BUCKETS - max_exact)).astype(np.int32)
    large = np.minimum(large, N_BUCKETS - 1)
    bucket = np.where(nn < max_exact, nn, large)
    valid = (dist >= 0) & (dist < WINDOW)
    return np.where(valid, bucket, -1).astype(np.int32)


def _bias_kernel(rb_ref, bucket_ref, out_ref):
    bucket = bucket_ref[...]
    nh = out_ref.shape[0]
    for h in range(nh):
        acc = jnp.full(bucket.shape, NEG_INF, F32)
        for b in range(N_BUCKETS):
            acc = jnp.where(bucket == b, rb_ref[b, h], acc)
        out_ref[h] = acc


def _bias_table(rel_bias):
    nh = rel_bias.shape[1]
    bucket = jnp.asarray(_bucket_table())
    return pl.pallas_call(
        _bias_kernel,
        in_specs=[pl.BlockSpec(memory_space=pltpu.SMEM),
                  pl.BlockSpec(bucket.shape, lambda: (0, 0))],
        out_specs=pl.BlockSpec((nh,) + bucket.shape, lambda: (0, 0, 0)),
        out_shape=jax.ShapeDtypeStruct((nh,) + bucket.shape, F32),
        name="rel_bias_table",
    )(rel_bias.astype(F32), bucket)


def _block_state_scan(sbuf, tre_ref, tim_ref, lane0, w, carry):
    lanes = pl.ds(lane0, w)
    first = lax.broadcasted_iota(jnp.int32, (SUBLANES, w), 0) == 0

    def body(b, carry):
        hr, hi = carry
        r0 = pl.multiple_of(b * SUBLANES, SUBLANES)
        xr = sbuf[pl.ds(r0, SUBLANES), 0:w]
        xi = sbuf[pl.ds(r0, SUBLANES), w:2 * w]
        for t, k in enumerate((1, 2, 4)):
            cr = tre_ref[0, t, :, lanes]
            ci = tim_ref[0, t, :, lanes]
            rr = pltpu.roll(xr, k, 0)
            ri = pltpu.roll(xi, k, 0)
            xr, xi = xr + cr * rr - ci * ri, xi + cr * ri + ci * rr
        pr = tre_ref[0, 3, :, lanes]
        pi = tim_ref[0, 3, :, lanes]
        xr, xi = xr + pr * hr - pi * hi, xi + pr * hi + pi * hr
        sbuf[pl.ds(r0, SUBLANES), 0:w] = jnp.where(first, hr, pltpu.roll(xr, 1, 0))
        sbuf[pl.ds(r0, SUBLANES), w:2 * w] = jnp.where(first, hi, pltpu.roll(xi, 1, 0))
        return (jnp.broadcast_to(xr[SUBLANES - 1:SUBLANES, :], (SUBLANES, w)),
                jnp.broadcast_to(xi[SUBLANES - 1:SUBLANES, :], (SUBLANES, w)))

    return lax.fori_loop(0, sbuf.shape[0] // SUBLANES, body, carry)


def _s5_mix(u, uext, yint, kst_ref, bst_ref, mst_ref, d, wglu_ref, bglu, advance):
    nblk = kst_ref.shape[1]
    cw = kst_ref.shape[3]
    rows = u.shape[0]
    nb = rows // SUBLANES
    j_i = lax.broadcasted_iota(jnp.int32, (1, SUBLANES, 1), 1)
    ys = []
    for blk in range(nblk):
        ub = u[:, blk * cw:(blk + 1) * cw]
        u3 = ub.reshape(nb, SUBLANES, cw)
        parts = [ub]
        for dd in range(1, SUBLANES):
            parts.append(jnp.where(j_i >= dd, pltpu.roll(u3, dd, 1), 0.0).reshape(rows, cw))
        for dd in range(SUBLANES):
            uext[dd] = parts[dd]
        y_intra = _dot(jnp.concatenate(parts, axis=1).astype(BF16), kst_ref[0, blk])
        last = [uext[dd, pl.ds(SUBLANES - 1, nb, stride=SUBLANES), :] for dd in range(SUBLANES)]
        s = _dot(jnp.concatenate(last, axis=1).astype(BF16), bst_ref[0, blk])
        hprev = advance(blk, s)
        yi = _dot(hprev.astype(BF16), mst_ref[0, blk])
        for j in range(SUBLANES):
            yint[pl.ds(j, nb, stride=SUBLANES), :] = yi[:, j * cw:(j + 1) * cw]
        ys.append(y_intra + yint[...])
    y = jnp.concatenate(ys, axis=1) + d * u
    g = jax.nn.gelu(y)
    return g * jax.nn.sigmoid(_dot(g.astype(BF16), wglu_ref[0]) + bglu)


def _mixer_prompt_kernel(*refs, n_pieces):
    if n_pieces:
        head_ref, piece_refs, refs = refs[0], refs[1:1 + n_pieces], refs[1 + n_pieces:]
    else:
        x_ref, refs = refs[0], refs[1:]
    (g_ref, wukv_ref, wqt_ref, wvt_ref, kst_ref, bst_ref, mst_ref, tre_ref, tim_ref,
     d_ref, wglu_ref, bglu_ref, bias_ref, sink_ref, wout_ref,
     o_ref, sre_ref, sim_ref, ko_ref, vo_ref,
     uext, yint, sbuf, hst, kbuf, vtbuf, mix, sc, pb) = refs
    t = pl.program_id(1)
    tm = o_ref.shape[1]
    d_ssm = d_ref.shape[-1]
    d_kv = N_KV_HEADS * HEAD_DIM
    gq = bias_ref.shape[3] // WINDOW
    nt_dims = (((1,), (1,)), ((), ()))

    @pl.when(t == 0)
    def _():
        hst[...] = jnp.zeros_like(hst)
        kbuf[0:WINDOW, :] = jnp.zeros((WINDOW, d_kv), F32)
        vtbuf[:, 0:WINDOW] = jnp.zeros((d_kv, WINDOW), F32)

    if n_pieces:
        first = jnp.where(t == 0, head_ref[...], piece_refs[0][0])
        x = jnp.concatenate([first] + [r[0] for r in piece_refs[1:]], axis=0)
    else:
        x = x_ref[0]
    h = _rmsnorm(x, g_ref[0]).astype(BF16)
    z = _dot(h, wukv_ref[0])
    u = z[:, :d_ssm]
    kbuf[WINDOW:, :] = z[:, d_ssm:d_ssm + d_kv]
    v = z[:, d_ssm + d_kv:]
    qt = lax.dot_general(wqt_ref[0], h, nt_dims, preferred_element_type=F32) * (HEAD_DIM ** -0.5)
    vtbuf[:, WINDOW:] = lax.dot_general(wvt_ref[0], h, nt_dims, preferred_element_type=F32)

    n_state = hst.shape[1] // 2

    w = sbuf.shape[1] // 2

    def advance(blk, s):
        lane0 = blk * w
        sbuf[...] = s
        init = (hst[:, lane0:lane0 + w], hst[:, n_state + lane0:n_state + lane0 + w])
        hr, hi = _block_state_scan(sbuf, tre_ref, tim_ref, lane0, w, init)
        hst[:, lane0:lane0 + w] = hr
        hst[:, n_state + lane0:n_state + lane0 + w] = hi
        return sbuf[...]

    ya = _s5_mix(u, uext, yint, kst_ref, bst_ref, mst_ref, d_ref[0], wglu_ref, bglu_ref[0], advance)
    mix[:, 0:d_ssm] = ya.astype(BF16)
    sre_ref[0] = hst[0:1, 0:n_state]
    sim_ref[0] = hst[0:1, n_state:]

    nqb = tm // WINDOW
    for qb in range(nqb):
        c0 = qb * WINDOW
        var = jnp.where(t == 0, qb + 1, 0) if qb + 1 < bias_ref.shape[0] else 0
        for j in range(N_KV_HEADS):
            kb = kbuf[c0:c0 + 2 * WINDOW, j * HEAD_DIM:(j + 1) * HEAD_DIM].astype(BF16)
            qc = jnp.concatenate(
                [qt[(j * gq + g) * HEAD_DIM:(j * gq + g + 1) * HEAD_DIM, c0:c0 + WINDOW] for g in range(gq)],
                axis=1).astype(BF16)
            sc[j, qb] = _dot(kb, qc) + bias_ref[var, j]
    dens = []
    for j in range(N_KV_HEADS):
        s = sc[j]
        sink = sink_ref[0, j][None]
        m = jnp.maximum(jnp.max(s, axis=1, keepdims=True), sink)
        p = jnp.exp(s - m)
        dens.append(jnp.sum(p, axis=1, keepdims=True) + jnp.exp(sink - m))
        pb[j] = p.astype(BF16)
    for qb in range(nqb):
        c0 = qb * WINDOW
        pieces = []
        for j in range(N_KV_HEADS):
            vb = vtbuf[j * HEAD_DIM:(j + 1) * HEAD_DIM, c0:c0 + 2 * WINDOW].astype(BF16)
            o = _dot(vb, pb[j, qb]) / dens[j][qb]
            pieces += [o[:, g * WINDOW:(g + 1) * WINDOW] for g in range(gq)]
        ot = jnp.concatenate(pieces, axis=0)
        mix[c0:c0 + WINDOW, d_ssm:] = ot.T.astype(BF16)

    kbuf[0:WINDOW, :] = kbuf[tm:tm + WINDOW, :]
    vtbuf[:, 0:WINDOW] = vtbuf[:, tm:tm + WINDOW]
    ko_ref[0] = kbuf[0:WINDOW, :]
    vo_ref[0] = v[tm - WINDOW:, :]
    o_ref[0] = x + _dot(mix[...], wout_ref[0])


def _mixer_prompt(x, head, idx, layer, wts, tm):
    nb, _, dm = x.shape
    tp = x.shape[1] + (0 if head is None else head.shape[0])
    nt = tp // tm
    n_state = wts["tre"].shape[-1]
    d_ssm = wts["ssm_d"].shape[-1]
    d_kv = N_KV_HEADS * HEAD_DIM
    w = wts["bst"].shape[-1] // 2
    cw = wts["kst"].shape[-1]
    layer_spec = lambda a, i: pl.BlockSpec((1,) + a.shape[1:], lambda b, t: (i,) + (0,) * (a.ndim - 1),
                                           pipeline_mode=pl.Buffered(1))
    if head is None:
        n_pieces = 0
        ins = [(x, pl.BlockSpec((1, tm, dm), lambda b, t: (b, t, 0)))]
    else:
        assert head.shape[0] == WINDOW
        n_pieces = tm // WINDOW
        piece = lambda i: pl.BlockSpec((1, WINDOW, dm), lambda b, t: (b, jnp.maximum(t * n_pieces + i - 1, 0), 0))
        ins = [(head, _const_spec(head.shape))] + [(x, piece(i)) for i in range(n_pieces)]
    ins += [
        (wts["g_mix"], layer_spec(wts["g_mix"], layer)),
        (wts["w_ukv"], layer_spec(wts["w_ukv"], idx)),
        (wts["w_qt"], layer_spec(wts["w_qt"], idx)),
        (wts["w_vt"], layer_spec(wts["w_vt"], idx)),
        (wts["kst"], layer_spec(wts["kst"], idx)),
        (wts["bst"], layer_spec(wts["bst"], idx)),
        (wts["mst"], layer_spec(wts["mst"], idx)),
        (wts["tre"], layer_spec(wts["tre"], idx)),
        (wts["tim"], layer_spec(wts["tim"], idx)),
        (wts["ssm_d"], layer_spec(wts["ssm_d"], idx)),
        (wts["w_glu"], layer_spec(wts["w_glu"], idx)),
        (wts["b_glu"], layer_spec(wts["b_glu"], idx)),
        (wts["bias_p"], _const_spec(wts["bias_p"].shape)),
        (wts["sink_p"], layer_spec(wts["sink_p"], idx)),
        (wts["w_out"], layer_spec(wts["w_out"], idx)),
    ]
    out_shape = [
        jax.ShapeDtypeStruct((nb, tp, dm), F32),
        jax.ShapeDtypeStruct((nb, 1, n_state), F32),
        jax.ShapeDtypeStruct((nb, 1, n_state), F32),
        jax.ShapeDtypeStruct((nb, WINDOW, d_kv), F32),
        jax.ShapeDtypeStruct((nb, WINDOW, d_kv), F32),
    ]
    out_specs = [
        pl.BlockSpec((1, tm, dm), lambda b, t: (b, t, 0)),
        pl.BlockSpec((1, 1, n_state), lambda b, t: (b, 0, 0)),
        pl.BlockSpec((1, 1, n_state), lambda b, t: (b, 0, 0)),
        pl.BlockSpec((1, WINDOW, d_kv), lambda b, t: (b, 0, 0)),
        pl.BlockSpec((1, WINDOW, d_kv), lambda b, t: (b, 0, 0)),
    ]
    scratch = [
        pltpu.VMEM((SUBLANES, tm, cw), F32),
        pltpu.VMEM((tm, cw), F32),
        pltpu.VMEM((tm // SUBLANES, 2 * w), F32),
        pltpu.VMEM((SUBLANES, 2 * n_state), F32),
        pltpu.VMEM((tm + WINDOW, d_kv), F32),
        pltpu.VMEM((d_kv, tm + WINDOW), F32),
        pltpu.VMEM((tm, dm), BF16),
        pltpu.VMEM((N_KV_HEADS, tm // WINDOW) + wts["bias_p"].shape[2:], F32),
        pltpu.VMEM((N_KV_HEADS, tm // WINDOW) + wts["bias_p"].shape[2:], BF16),
    ]
    assert tm // WINDOW >= wts["bias_p"].shape[0] - 1
    return pl.pallas_call(
        functools.partial(_mixer_prompt_kernel, n_pieces=n_pieces),
        grid=(nb, nt),
        in_specs=[s for _, s in ins],
        out_specs=out_specs,
        out_shape=out_shape,
        scratch_shapes=scratch,
        compiler_params=_cparams(("arbitrary", "arbitrary")),
        name=f"mixer_prompt_l{layer}",
    )(*[a for a, _ in ins])


def _mixer_sample_kernel(x_ref, h0r_ref, h0i_ref, kc_ref, vc_ref, g_ref, win_ref, kst_ref, bst_ref, mst_ref,
                         tre_ref, tim_ref, d_ref, wglu_ref, bglu_ref, bias_ref, sink_ref, wout_ref,
                         o_ref, sre_ref, sim_ref, ko_ref, vo_ref, uext, yint):
    rows = x_ref.shape[0]
    ns = kc_ref.shape[0]
    tq = rows // ns
    assert tq == SUBLANES
    w_rows = kc_ref.shape[1]
    d_ssm = d_ref.shape[-1]
    gq = bias_ref.shape[1] // tq
    d_q = N_KV_HEADS * gq * HEAD_DIM
    d_kv = N_KV_HEADS * HEAD_DIM

    x = x_ref[...]
    h = _rmsnorm(x, g_ref[0]).astype(BF16)
    z = _dot(h, win_ref[0])
    u = z[:, :d_ssm]
    q = (z[:, d_ssm:d_ssm + d_q] * (HEAD_DIM ** -0.5)).reshape(ns, tq, d_q)
    kn = z[:, d_ssm + d_q:d_ssm + d_q + d_kv].reshape(ns, tq, d_kv)
    vn = z[:, d_ssm + d_q + d_kv:].reshape(ns, tq, d_kv)

    w = bst_ref.shape[-1] // 2

    def advance(blk, s):
        lanes = slice(blk * w, (blk + 1) * w)
        hr = h0r_ref[:, lanes]
        hi = h0i_ref[:, lanes]
        ar = tre_ref[0, 0, SUBLANES - 1:SUBLANES, lanes]
        ai = tim_ref[0, 0, SUBLANES - 1:SUBLANES, lanes]
        sre_ref[:, lanes] = ar * hr - ai * hi + s[:, 0:w]
        sim_ref[:, lanes] = ar * hi + ai * hr + s[:, w:2 * w]
        return jnp.concatenate([hr, hi], axis=1)

    ya = _s5_mix(u, uext, yint, kst_ref, bst_ref, mst_ref, d_ref[0], wglu_ref, bglu_ref[0], advance)

    kc = jnp.concatenate([kc_ref[...], kn], axis=1)
    vc = jnp.concatenate([vc_ref[...], vn], axis=1)
    ko_ref[...] = kc[:, tq:, :]
    vo_ref[...] = vc[:, tq:, :]
    pieces = []
    for j in range(N_KV_HEADS):
        kb = kc[:, :, j * HEAD_DIM:(j + 1) * HEAD_DIM].astype(BF16)
        vb = vc[:, :, j * HEAD_DIM:(j + 1) * HEAD_DIM].astype(BF16)
        qs = jnp.concatenate(
            [q[:, :, (j * gq + g) * HEAD_DIM:(j * gq + g + 1) * HEAD_DIM] for g in range(gq)],
            axis=1).astype(BF16)
        s = jnp.einsum("nqd,nkd->nqk", qs, kb, preferred_element_type=F32) + bias_ref[j]
        sink = sink_ref[0, j]
        m = jnp.maximum(jnp.max(s, axis=-1, keepdims=True), sink)
        p = jnp.exp(s - m)
        l = jnp.sum(p, axis=-1, keepdims=True) + jnp.exp(sink - m)
        o = jnp.einsum("nqk,nkd->nqd", p.astype(BF16), vb, preferred_element_type=F32) / l
        pieces += [o[:, g * tq:(g + 1) * tq, :] for g in range(gq)]
    yb = jnp.concatenate(pieces, axis=2).reshape(rows, d_q)
    mixed = jnp.concatenate([ya, yb], axis=1).astype(BF16)
    o_ref[...] = x + _dot(mixed, wout_ref[0])


def _mixer_sample(x2, h0r, h0i, kc, vc, idx, layer, wts, ns):
    rows_all, dm = x2.shape
    _, n_seq, w_rows, d_kv = kc.shape
    tq = rows_all // n_seq
    n_state = h0r.shape[-1]
    cw = wts["kst"].shape[-1]
    layer_spec = lambda a, i: pl.BlockSpec((1,) + a.shape[1:], lambda s: (i,) + (0,) * (a.ndim - 1),
                                           pipeline_mode=pl.Buffered(1))
    row_spec = pl.BlockSpec((ns * tq, dm), lambda s: (s, 0))
    st_spec = pl.BlockSpec((ns, n_state), lambda s: (s, 0))
    kv_spec = pl.BlockSpec((ns, w_rows, d_kv), lambda s: (s, 0, 0))
    st_in = pl.BlockSpec((None, ns, n_state), lambda s: (idx, s, 0))
    kv_in = pl.BlockSpec((None, ns, w_rows, d_kv), lambda s: (idx, s, 0, 0))
    ins = [
        (x2, row_spec), (h0r, st_in), (h0i, st_in), (kc, kv_in), (vc, kv_in),
        (wts["g_mix"], layer_spec(wts["g_mix"], layer)),
        (wts["w_in"], layer_spec(wts["w_in"], idx)),
        (wts["kst"], layer_spec(wts["kst"], idx)),
        (wts["bst"], layer_spec(wts["bst"], idx)),
        (wts["mst"], layer_spec(wts["mst"], idx)),
        (wts["tre"], layer_spec(wts["tre"], idx)),
        (wts["tim"], layer_spec(wts["tim"], idx)),
        (wts["ssm_d"], layer_spec(wts["ssm_d"], idx)),
        (wts["w_glu"], layer_spec(wts["w_glu"], idx)),
        (wts["b_glu"], layer_spec(wts["b_glu"], idx)),
        (wts["bias_s"], _const_spec(wts["bias_s"].shape)),
        (wts["sink_s"], layer_spec(wts["sink_s"], idx)),
        (wts["w_out"], layer_spec(wts["w_out"], idx)),
    ]
    return pl.pallas_call(
        _mixer_sample_kernel,
        grid=(n_seq // ns,),
        in_specs=[s for _, s in ins],
        out_specs=[row_spec, st_spec, st_spec, kv_spec, kv_spec],
        out_shape=[jax.ShapeDtypeStruct(x2.shape, F32), jax.ShapeDtypeStruct(h0r.shape[1:], F32),
                   jax.ShapeDtypeStruct(h0i.shape[1:], F32), jax.ShapeDtypeStruct(kc.shape[1:], F32),
                   jax.ShapeDtypeStruct(vc.shape[1:], F32)],
        scratch_shapes=[pltpu.VMEM((SUBLANES, ns * tq, cw), F32), pltpu.VMEM((ns * tq, cw), F32)],
        compiler_params=_cparams(("arbitrary",)),
        name=f"mixer_sample_l{layer}",
    )(*[a for a, _ in ins])


def _conformer_head(x, g_ref, w1_ref, c):
    h = _rmsnorm(x, g_ref[0]).astype(BF16)
    z = _dot(h, w1_ref[0])
    return z[:, :c] * jax.nn.sigmoid(z[:, c:])


def _conformer_tail(y, bdw_ref, lng_ref, lnb_ref, w2_ref):
    y = y + bdw_ref[0]
    mu = jnp.mean(y, axis=-1, keepdims=True)
    yc = y - mu
    var = jnp.mean(yc * yc, axis=-1, keepdims=True)
    y = yc * lax.rsqrt(var + EPS) * lng_ref[0] + lnb_ref[0]
    return _dot(jax.nn.silu(y).astype(BF16), w2_ref[0])


def _conformer_prompt_kernel(x_ref, g_ref, w1_ref, wdw_ref, bdw_ref, lng_ref, lnb_ref, w2_ref, o_ref, st_ref,
                             ext, ybuf, *, pad_rows, width):
    t = pl.program_id(1)
    x = x_ref[0]
    nslab, le, lanes = ext.shape
    l = le - CONV_HALO
    c = nslab * lanes
    off = CONV_HALO - (width - 1)
    mrows = CONV_PHASE_VREGS * SUBLANES
    span = CONV_PHASES * mrows
    assert l % CONF_BLOCK == 0 and CONF_BLOCK % span == 0

    @pl.when(t == 0)
    def _():
        ext[:, 0:CONV_HALO, :] = jnp.zeros((nslab, CONV_HALO, lanes), F32)

    gl = _conformer_head(x, g_ref, w1_ref, c)
    for q in range(nslab):
        ext[q, CONV_HALO:, :] = gl[:, q * lanes:(q + 1) * lanes]
    st_ref[0] = jnp.concatenate([ext[q, l + off:, :] for q in range(nslab)], axis=1)

    def block(cb, _):
        r0 = pl.multiple_of(cb * CONF_BLOCK, CONF_BLOCK)
        for grp in range(CONF_BLOCK // span):
            g0 = r0 + grp * span
            for q0 in range(0, nslab, CONV_SLABS):
                slabs = range(q0, q0 + CONV_SLABS)
                acc = {(r, q): jnp.zeros((CONV_PHASE_VREGS, SUBLANES, lanes), F32)
                       for r in range(CONV_PHASES) for q in slabs}
                for s in range(width + CONV_PHASES - 1):
                    for q in slabs:
                        xs = ext[q, pl.ds(g0 + (off + s), mrows, stride=CONV_PHASES), :]
                        xs = xs.reshape(CONV_PHASE_VREGS, SUBLANES, lanes)
                        for r in range(CONV_PHASES):
                            k = s - r
                            if 0 <= k < width:
                                wk = wdw_ref[0, k, :, q * lanes:(q + 1) * lanes]
                                acc[r, q] = acc[r, q] + wk[None] * xs
                for r in range(CONV_PHASES):
                    for q in slabs:
                        ybuf[q, pl.ds(g0 + r, mrows, stride=CONV_PHASES), :] = acc[r, q].reshape(mrows, lanes)
        y = jnp.concatenate([ybuf[q, pl.ds(r0, CONF_BLOCK), :] for q in range(nslab)], axis=1)
        out = _conformer_tail(y, bdw_ref, lng_ref, lnb_ref, w2_ref)
        rid = t * l + r0 + lax.broadcasted_iota(jnp.int32, (CONF_BLOCK, 1), 0)
        o_ref[0, pl.ds(r0, CONF_BLOCK), :] = x_ref[0, pl.ds(r0, CONF_BLOCK), :] + jnp.where(rid >= pad_rows, out, 0.0)
        return 0

    lax.fori_loop(0, l // CONF_BLOCK, block, 0)
    ext[:, 0:CONV_HALO, :] = ext[:, l:l + CONV_HALO, :]


def _conformer_sample_kernel(x_ref, past_ref, g_ref, w1_ref, wdw_ref, bdw_ref, lng_ref, lnb_ref, w2_ref,
                             o_ref, st_ref, ext, ybuf, stage, *, width):
    x = x_ref[...]
    ns, le, c = ext.shape
    l = le - CONV_HALO
    off = CONV_HALO - (width - 1)
    ext[:, 0:off, :] = jnp.zeros((ns, off, c), F32)
    ext[:, off:CONV_HALO, :] = past_ref[...]
    ext[:, CONV_HALO:, :] = _conformer_head(x, g_ref, w1_ref, c).reshape(ns, l, c)
    st_ref[...] = ext[:, l + off:, :]

    def conv_unit(n, stg):
        win = ext[n]
        for s in range(1, SUBLANES):
            stg[s - 1] = win[s:s + l + CONV_HALO - SUBLANES]
        acc = jnp.zeros((l // SUBLANES, SUBLANES, c), F32)
        for k in range(width):
            a, s = divmod(off + k, SUBLANES)
            if s == 0:
                xk = ext[n, a * SUBLANES:a * SUBLANES + l, :]
            else:
                xk = stg[s - 1, a * SUBLANES:a * SUBLANES + l, :]
            acc = acc + wdw_ref[0, k][None] * xk.reshape(l // SUBLANES, SUBLANES, c)
        return acc.reshape(l, c)

    def seq(i, _):
        for half in range(2):
            n = 2 * i + half
            ybuf[pl.ds(pl.multiple_of(n * l, l), l), :] = conv_unit(n, stage.at[half])
        return 0
    assert ns % 2 == 0 and l % SUBLANES == 0
    lax.fori_loop(0, ns // 2, seq, 0)
    o_ref[...] = x + _conformer_tail(ybuf[...], bdw_ref, lng_ref, lnb_ref, w2_ref)


def _conformer(x, past, idx, layer, wts, tile, pad_rows):
    carry = past is None
    width = wts["w_dw"].shape[1]
    c = wts["w_dw"].shape[3]
    if carry:
        nb, tp, dm = x.shape
        grid = (nb, tp // tile)
        lspec = lambda a, i: pl.BlockSpec((1,) + a.shape[1:], lambda b, t: (i,) + (0,) * (a.ndim - 1),
                                          pipeline_mode=pl.Buffered(1))
        xspec = pl.BlockSpec((1, tile, dm), lambda b, t: (b, t, 0))
        ins = [(x, xspec)]
        st_shape = (nb, width - 1, c)
        st_spec = pl.BlockSpec((1, width - 1, c), lambda b, t: (b, 0, 0))
        ns, l = 1, tile
        sem = ("arbitrary", "arbitrary")
    else:
        rows_all, dm = x.shape
        n_seq = past.shape[1]
        l = rows_all // n_seq
        ns = tile
        grid = (n_seq // ns,)
        lspec = lambda a, i: pl.BlockSpec((1,) + a.shape[1:], lambda s: (i,) + (0,) * (a.ndim - 1),
                                          pipeline_mode=pl.Buffered(1))
        xspec = pl.BlockSpec((ns * l, dm), lambda s: (s, 0))
        ins = [(x, xspec), (past, pl.BlockSpec((None, ns, width - 1, c), lambda s: (idx, s, 0, 0)))]
        st_shape = (n_seq, width - 1, c)
        st_spec = pl.BlockSpec((ns, width - 1, c), lambda s: (s, 0, 0))
        sem = ("arbitrary",)
    ins += [
        (wts["g_mix"], lspec(wts["g_mix"], layer)),
        (wts["w_pw1"], lspec(wts["w_pw1"], idx)),
        (wts["w_dw"], lspec(wts["w_dw"], idx)),
        (wts["b_dw"], lspec(wts["b_dw"], idx)),
        (wts["ln_g"], lspec(wts["ln_g"], idx)),
        (wts["ln_b"], lspec(wts["ln_b"], idx)),
        (wts["w_pw2"], lspec(wts["w_pw2"], idx)),
    ]
    if carry:
        body = functools.partial(_conformer_prompt_kernel, pad_rows=pad_rows, width=width)
        scratch = [pltpu.VMEM((c // LANES, l + CONV_HALO, LANES), F32), pltpu.VMEM((c // LANES, l, LANES), F32)]
    else:
        body = functools.partial(_conformer_sample_kernel, width=width)
        scratch = [pltpu.VMEM((ns, l + CONV_HALO, c), F32), pltpu.VMEM((ns * l, c), F32),
                   pltpu.VMEM((2, SUBLANES - 1, l + CONV_HALO - SUBLANES, c), F32)]
    return pl.pallas_call(
        body,
        grid=grid,
        in_specs=[s for _, s in ins],
        out_specs=[xspec, st_spec],
        out_shape=[jax.ShapeDtypeStruct(x.shape, F32), jax.ShapeDtypeStruct(st_shape, F32)],
        scratch_shapes=scratch,
        compiler_params=_cparams(sem),
        name=f"conformer_{'prompt' if carry else 'sample'}_l{layer}",
    )(*[a for a, _ in ins])


def _ffn_kernel(*refs, carry, final_norm, n_pieces):
    if n_pieces:
        halo_ref, piece_refs, refs = refs[0], refs[1:1 + n_pieces], refs[1 + n_pieces:]
        x = jnp.concatenate([r[0] for r in piece_refs], axis=0)
    elif carry:
        x_ref, refs = refs[0], refs[1:]
        x = x_ref[0]
    else:
        x_ref, past_ref, refs = refs[0], refs[1], refs[2:]
        x = x_ref[...]
    (g_ref, wup_ref, wcv_ref, bcv_ref, wdn_ref, gf_ref, o_ref, st_ref, ext) = refs
    ns, le, dff = ext.shape
    l = le - FFN_HALO
    rows = ns * l
    kw = wcv_ref.shape[1]

    if carry:
        @pl.when(pl.program_id(1) == 0)
        def _():
            if n_pieces:
                hh = _rmsnorm(halo_ref[0, WINDOW - FFN_HALO:, :], g_ref[0]).astype(BF16)
                ext[:, 0:FFN_HALO, :] = _dot(hh, wup_ref[0, :, 0:dff]).reshape(ns, FFN_HALO, dff)
            else:
                ext[:, 0:FFN_HALO, :] = jnp.zeros((ns, FFN_HALO, dff), F32)
    else:
        ext[:, FFN_HALO - (kw - 1):FFN_HALO, :] = past_ref[...]

    h = _rmsnorm(x, g_ref[0]).astype(BF16)
    gate = _dot(h, wup_ref[0, :, 0:dff])
    up = _dot(h, wup_ref[0, :, dff:])
    ext[:, FFN_HALO:, :] = gate.reshape(ns, l, dff)
    gc = bcv_ref[0]
    for k in range(kw):
        o = FFN_HALO - (kw - 1) + k
        gc = gc + wcv_ref[0, k:k + 1, :] * ext[:, o:o + l, :]
    y = (jax.nn.gelu(gc).reshape(rows, dff) * up).astype(BF16)
    acc = x + _dot(y, wdn_ref[0])
    st_ref[...] = ext[:, le - (kw - 1):, :].reshape(st_ref.shape)
    if final_norm:
        acc = _rmsnorm(acc, gf_ref[...])
    if carry:
        o_ref[0] = acc
        ext[:, 0:FFN_HALO, :] = ext[:, l:l + FFN_HALO, :]
    else:
        o_ref[...] = acc


def _ffn(x, past, layer, wts, tile, final_norm, skip_rows=0):
    carry = past is None
    dff = wts["w_dn"].shape[1]
    kw = wts["w_cv"].shape[1]
    n_pieces = 0
    if carry:
        nb, tp, dm = x.shape
        out_rows = tp - skip_rows
        grid = (nb, out_rows // tile)
        lspec = lambda a, i: pl.BlockSpec((1,) + a.shape[1:], lambda b, t: (i,) + (0,) * (a.ndim - 1),
                                          pipeline_mode=pl.Buffered(1))
        xspec = pl.BlockSpec((1, tile, dm), lambda b, t: (b, t, 0))
        if skip_rows:
            assert skip_rows % WINDOW == 0 and tile % WINDOW == 0 and out_rows % tile == 0
            n_pieces, skip = tile // WINDOW, skip_rows // WINDOW
            piece = lambda i: pl.BlockSpec((1, WINDOW, dm), lambda b, t: (b, skip + t * n_pieces + i, 0))
            ins = [(x, piece(-1))] + [(x, piece(i)) for i in range(n_pieces)]
        else:
            ins = [(x, xspec)]
        out_struct = jax.ShapeDtypeStruct((nb, out_rows, dm), F32)
        st_shape = (nb, kw - 1, dff)
        st_spec = pl.BlockSpec((1, kw - 1, dff), lambda b, t: (b, 0, 0))
        ns, l = 1, tile
        sem = ("arbitrary", "arbitrary")
    else:
        out_struct = jax.ShapeDtypeStruct(x.shape, F32)
        rows_all, dm = x.shape
        n_seq = past.shape[1]
        l = rows_all // n_seq
        ns = tile
        grid = (n_seq // ns,)
        lspec = lambda a, i: pl.BlockSpec((1,) + a.shape[1:], lambda s: (i,) + (0,) * (a.ndim - 1),
                                          pipeline_mode=pl.Buffered(1))
        xspec = pl.BlockSpec((ns * l, dm), lambda s: (s, 0))
        ins = [(x, xspec), (past, pl.BlockSpec((None, ns, kw - 1, dff), lambda s: (layer, s, 0, 0)))]
        st_shape = (n_seq, kw - 1, dff)
        st_spec = pl.BlockSpec((ns, kw - 1, dff), lambda s: (s, 0, 0))
        sem = ("arbitrary",)
    ins += [
        (wts["g_ffn"], lspec(wts["g_ffn"], layer)),
        (wts["w_up"], lspec(wts["w_up"], layer)),
        (wts["w_cv"], lspec(wts["w_cv"], layer)),
        (wts["b_cv"], lspec(wts["b_cv"], layer)),
        (wts["w_dn"], lspec(wts["w_dn"], layer)),
        (wts["g_final"], _const_spec(wts["g_final"].shape)),
    ]
    return pl.pallas_call(
        functools.partial(_ffn_kernel, carry=carry, final_norm=final_norm, n_pieces=n_pieces),
        grid=grid,
        in_specs=[s for _, s in ins],
        out_specs=[xspec, st_spec],
        out_shape=[out_struct, jax.ShapeDtypeStruct(st_shape, F32)],
        scratch_shapes=[pltpu.VMEM((ns, l + FFN_HALO, dff), F32)],
        compiler_params=_cparams(sem),
        name=f"ffn_{'prompt' if carry else 'sample'}_l{layer}",
    )(*[a for a, _ in ins])


def _pick_tile(total, target):
    best = WINDOW
    for m in range(1, total // WINDOW + 1):
        if total % (m * WINDOW) == 0 and m * WINDOW <= target:
            best = m * WINDOW
    return best


def kernel(x_prompt, x_sample, state_ssm_re, state_ssm_im, cache_swa_k, cache_swa_v, state_conv, state_ffn,
           meta_tokens, g_mix, g_ffn, g_final, w_in_mix, ssm_lambda_re, ssm_lambda_im, ssm_log_step,
           ssm_b_re, ssm_b_im, ssm_c_re, ssm_c_im, ssm_d, ssm_w_glu, ssm_b_glu, rel_bias, attn_sinks,
           w_out_mix, conv_w_pw1, conv_w_dw, conv_b_dw, conv_ln_g, conv_ln_b, conv_w_pw2,
           ffn_w_up, ffn_w_conv, ffn_b_conv, ffn_w_down):
    nb, seq, dm = x_prompt.shape
    n_seq, tq, _ = x_sample.shape
    depth = g_mix.shape[0]
    n_meta = meta_tokens.shape[0]
    n_even, g_ssm, p_ssm = ssm_lambda_re.shape
    n_heads = rel_bias.shape[1]
    gq = n_heads // N_KV_HEADS
    w_rows = cache_swa_k.shape[2]
    d_kv = N_KV_HEADS * HEAD_DIM
    conv_w = conv_w_dw.shape[1]
    ffn_w = ffn_w_conv.shape[1]
    assert tq == SUBLANES and w_rows == WINDOW and seq % WINDOW == 0 and n_meta <= WINDOW
    assert conv_w - 1 <= CONV_HALO and ffn_w - 1 <= FFN_HALO

    tre, tim, kst, bst, mst = _ssm_prep(ssm_lambda_re, ssm_lambda_im, ssm_log_step,
                                        ssm_b_re, ssm_b_im, ssm_c_re, ssm_c_im)
    bias = _bias_table(rel_bias)
    row3 = lambda a: a.reshape(a.shape[0], 1, a.shape[-1]).astype(F32)
    sinks = attn_sinks.astype(F32).reshape(n_even, N_KV_HEADS, gq, 1)
    pad_rows = WINDOW - n_meta
    bias_t = jnp.transpose(bias.reshape(N_KV_HEADS, gq, WINDOW, 2 * WINDOW), (0, 3, 1, 2)
                           ).reshape(N_KV_HEADS, 2 * WINDOW, gq * WINDOW)
    key_i = jnp.arange(2 * WINDOW)[None, :, None]
    n_var = 1 + -(-(pad_rows + WINDOW) // WINDOW)
    bias_p = jnp.stack([bias_t] + [jnp.where(key_i >= pad_rows + WINDOW - qb * WINDOW, bias_t, NEG_INF)
                                   for qb in range(n_var - 1)])
    d_ssm = ssm_d.shape[-1]
    d_q = n_heads * HEAD_DIM
    w_in_bf = w_in_mix.astype(BF16)
    wts = {
        "g_mix": row3(g_mix), "g_ffn": row3(g_ffn), "g_final": g_final.reshape(1, dm).astype(F32),
        "w_in": w_in_bf, "kst": kst, "bst": bst, "mst": mst, "tre": tre, "tim": tim,
        "w_ukv": jnp.concatenate([w_in_bf[:, :, :d_ssm], w_in_bf[:, :, d_ssm + d_q:]], axis=-1),
        "w_qt": jnp.transpose(w_in_bf[:, :, d_ssm:d_ssm + d_q], (0, 2, 1)),
        "w_vt": jnp.transpose(w_in_bf[:, :, d_ssm + d_q + d_kv:], (0, 2, 1)),
        "ssm_d": row3(ssm_d), "w_glu": ssm_w_glu.astype(BF16), "b_glu": row3(ssm_b_glu),
        "bias_p": bias_p,
        "bias_s": bias[:, :tq, :w_rows + tq].reshape(N_KV_HEADS, gq * tq, w_rows + tq),
        "sink_p": jnp.broadcast_to(sinks[:, :, :, None, :], (n_even, N_KV_HEADS, gq, WINDOW, 1)
                                   ).reshape(n_even, N_KV_HEADS, 1, gq * WINDOW),
        "sink_s": jnp.broadcast_to(sinks[:, :, :, None, :], (n_even, N_KV_HEADS, gq, tq, 1)
                                   ).reshape(n_even, N_KV_HEADS, gq * tq, 1),
        "w_out": w_out_mix.astype(BF16),
        "w_pw1": conv_w_pw1.astype(BF16), "w_dw": jnp.broadcast_to(conv_w_dw.astype(F32)[:, :, None, :],
                                 conv_w_dw.shape[:2] + (SUBLANES, conv_w_dw.shape[2])), "b_dw": row3(conv_b_dw),
        "ln_g": row3(conv_ln_g), "ln_b": row3(conv_ln_b), "w_pw2": conv_w_pw2.astype(BF16),
        "w_up": ffn_w_up.astype(BF16), "w_cv": ffn_w_conv.astype(F32), "b_cv": row3(ffn_b_conv),
        "w_dn": ffn_w_down.astype(BF16),
    }

    tp = pad_rows + n_meta + seq
    tm = _pick_tile(tp, PROMPT_TILE_TARGET)
    xp = x_prompt.astype(F32)
    head = jnp.concatenate([jnp.zeros((pad_rows, dm), F32), meta_tokens.astype(F32)], axis=0)
    xs = x_sample.astype(F32).reshape(n_seq * tq, dm)
    ns = min(SAMPLE_SEQS, n_seq)
    assert n_seq % ns == 0

    h0r_all = state_ssm_re.astype(F32).reshape(n_even, n_seq, g_ssm * p_ssm)
    h0i_all = state_ssm_im.astype(F32).reshape(n_even, n_seq, g_ssm * p_ssm)
    kc_all = cache_swa_k.astype(F32).reshape(n_even, n_seq, w_rows, d_kv)
    vc_all = cache_swa_v.astype(F32).reshape(n_even, n_seq, w_rows, d_kv)
    conv_all = state_conv.astype(F32)
    ffn_all = state_ffn.astype(F32)

    sr_p, si_p, k_p, v_p, c_p, f_p = [], [], [], [], [], []
    sr_s, si_s, k_s, v_s, c_s, f_s = [], [], [], [], [], []
    for layer in range(depth):
        idx = layer // 2
        if layer % 2 == 0:
            xp, sre, sim, ko, vo = _mixer_prompt(xp, head if layer == 0 else None, idx, layer, wts, tm)
            sr_p.append(sre.reshape(nb, g_ssm, p_ssm))
            si_p.append(sim.reshape(nb, g_ssm, p_ssm))
            k_p.append(ko.reshape(nb, WINDOW, N_KV_HEADS, HEAD_DIM))
            v_p.append(vo.reshape(nb, WINDOW, N_KV_HEADS, HEAD_DIM))
            xs, sre, sim, ko, vo = _mixer_sample(xs, h0r_all, h0i_all, kc_all, vc_all, idx, layer, wts, ns)
            sr_s.append(sre.reshape(n_seq, g_ssm, p_ssm))
            si_s.append(sim.reshape(n_seq, g_ssm, p_ssm))
            k_s.append(ko.reshape(n_seq, w_rows, N_KV_HEADS, HEAD_DIM))
            v_s.append(vo.reshape(n_seq, w_rows, N_KV_HEADS, HEAD_DIM))
        else:
            xp, st = _conformer(xp, None, idx, layer, wts, tm, pad_rows)
            c_p.append(st)
            xs, st = _conformer(xs, conv_all, idx, layer, wts, ns, 0)
            c_s.append(st)
        last = layer == depth - 1
        if last:
            xp, st = _ffn(xp, None, layer, wts, _pick_tile(seq, PROMPT_TILE_TARGET), True,
                          skip_rows=pad_rows + n_meta)
        else:
            xp, st = _ffn(xp, None, layer, wts, tm, False)
        f_p.append(st)
        xs, st = _ffn(xs, ffn_all, layer, wts, ns, last)
        f_s.append(st)

    yp = xp
    ys = xs.reshape(n_seq, tq, dm)
    st = jnp.stack
    return (yp, ys, st(sr_p), st(si_p), st(k_p), st(v_p), st(c_p), st(f_p),
            st(sr_s), st(si_s), st(k_s), st(v_s), st(c_s), st(f_s))
```

```python
import functools
import math

import numpy as np
import jax
import jax.numpy as jnp
from jax import lax
from jax.experimental import pallas as pl
from jax.experimental.pallas import tpu as pltpu

F32 = jnp.float32
BF16 = jnp.bfloat16

EPS = 1e-6
NEG_INF = -1e30
WINDOW = 128
HEAD_DIM = 64
N_KV_HEADS = 2
N_BUCKETS = 32
MAX_DISTANCE = 128
SSM_GROUP = 16
SSM_STATE = 64
SSM_BLK_GROUPS = 8
SUBLANES = 8
CONV_HALO = 32
FFN_HALO = 8
LANES = 128
CONV_PHASES = 4
CONV_PHASE_VREGS = 4
CONV_SLABS = 2
CONF_BLOCK = 128
PROMPT_TILE_TARGET = 768
SAMPLE_SEQS = 32
VMEM_LIMIT = 56 * 1024 * 1024


def _cparams(sem):
    return pltpu.CompilerParams(dimension_semantics=sem, vmem_limit_bytes=VMEM_LIMIT)


def _const_spec(shape):
    nd = len(shape)
    return pl.BlockSpec(shape, lambda *_: (0,) * nd, pipeline_mode=pl.Buffered(1))


def _rmsnorm(x, g):
    return x * lax.rsqrt(jnp.mean(x * x, axis=-1, keepdims=True) + EPS) * g


def _dot(a, b):
    return jnp.dot(a, b, preferred_element_type=F32)


def _ssm_prep_kernel(lr_ref, li_ref, ls_ref, bre_ref, bim_ref, cre_ref, cim_ref,
                     tre_ref, tim_ref, kst_ref, bst_ref, mst_ref):
    lr = lr_ref[0]
    li = li_ref[0]
    dt = jnp.exp(ls_ref[0])
    decay = jnp.exp(lr * dt)
    a_re = decay * jnp.cos(li * dt)
    a_im = decay * jnp.sin(li * dt)
    den = lr * lr + li * li
    num_re = a_re - 1.0
    coef_re = (num_re * lr + a_im * li) / den
    coef_im = (a_im * lr - num_re * li) / den

    cmul = lambda x, y: (x[0] * y[0] - x[1] * y[1], x[0] * y[1] + x[1] * y[0])
    n = lr.shape[-1]
    pw = [(jnp.ones((1, n), F32), jnp.zeros((1, n), F32)), (a_re, a_im)]
    for _ in range(SUBLANES - 1):
        pw.append(cmul(pw[-1], (a_re, a_im)))
    apw = [pw[SUBLANES]]
    for _ in range(SUBLANES - 1):
        apw.append(cmul(apw[-1], pw[SUBLANES]))
    row = lax.broadcasted_iota(jnp.int32, (SUBLANES, n), 0)
    zero = jnp.zeros((SUBLANES, n), F32)
    for t, k in enumerate((1, 2, 4)):
        tre_ref[0, t] = jnp.where(row >= k, jnp.broadcast_to(apw[k - 1][0], (SUBLANES, n)), zero)
        tim_ref[0, t] = jnp.where(row >= k, jnp.broadcast_to(apw[k - 1][1], (SUBLANES, n)), zero)
    ap_re, ap_im = zero, zero
    for j in range(SUBLANES):
        ap_re = jnp.where(row == j, jnp.broadcast_to(apw[j][0], (SUBLANES, n)), ap_re)
        ap_im = jnp.where(row == j, jnp.broadcast_to(apw[j][1], (SUBLANES, n)), ap_im)
    tre_ref[0, 3] = ap_re
    tim_ref[0, 3] = ap_im

    bre = bre_ref[0]
    bim = bim_ref[0]
    bbar = (coef_re * bre - coef_im * bim, coef_re * bim + coef_im * bre)
    rows = SSM_BLK_GROUPS * SSM_GROUP
    r_i = lax.broadcasted_iota(jnp.int32, (rows, n), 0)
    c_i = lax.broadcasted_iota(jnp.int32, (rows, n), 1)
    sh = lambda v, d: lax.shift_right_logical(v, jnp.full(v.shape, int(math.log2(d)), jnp.int32))
    diag = sh(r_i, SSM_GROUP) == (sh(c_i, SSM_STATE) & (SSM_BLK_GROUPS - 1))
    zer = jnp.zeros((rows, n), F32)
    blockdiag = lambda v: jnp.where(diag, jnp.concatenate([v] * SSM_BLK_GROUPS, axis=0), zer)
    bb = (blockdiag(bbar[0]), blockdiag(bbar[1]))
    cc = (blockdiag(cre_ref[0]), blockdiag(cim_ref[0]))
    w = SSM_BLK_GROUPS * SSM_STATE
    nblk = n // w
    for blk in range(nblk):
        sl = slice(blk * w, (blk + 1) * w)
        bbs = (bb[0][:, sl], bb[1][:, sl])
        ccs = (cc[0][:, sl], cc[1][:, sl])
        bcats = []
        for d in range(SUBLANES):
            rs = slice(d * rows, (d + 1) * rows)
            ba = cmul(bbs, (pw[d][0][:, sl], pw[d][1][:, sl]))
            ca = cmul(ccs, (pw[d + 1][0][:, sl], pw[d + 1][1][:, sl]))
            bcats.append(jnp.concatenate([ba[0], ba[1]], axis=1))
            bst_ref[0, blk, rs, :] = bcats[-1].astype(BF16)
            mcat = jnp.concatenate([ca[0], -ca[1]], axis=1)
            mst_ref[0, blk, :, rs] = mcat.T.astype(BF16)
        ccat = jnp.concatenate([ccs[0], -ccs[1]], axis=1)
        split = lambda v: (v.astype(BF16), (v - v.astype(BF16).astype(F32)).astype(BF16))
        (bh, bl), (ch, cl) = split(jnp.concatenate(bcats, axis=0)), split(ccat)
        nt = lambda p, q: lax.dot_general(p, q, (((1,), (1,)), ((), ())), preferred_element_type=F32)
        kst_ref[0, blk] = (nt(bh, ch) + nt(bh, cl) + nt(bl, ch)).astype(BF16)


def _ssm_prep(lam_re, lam_im, log_step, b_re, b_im, c_re, c_im):
    ne, g, p = lam_re.shape
    n = g * p
    assert g % SSM_BLK_GROUPS == 0 and p == SSM_STATE and b_re.shape[-1] == SSM_GROUP
    nblk = g // SSM_BLK_GROUPS
    rows = SSM_BLK_GROUPS * SSM_GROUP
    w = SSM_BLK_GROUPS * SSM_STATE
    flat = lambda a: a.reshape(ne, 1, n)
    ls = jnp.broadcast_to(log_step[:, :, None], (ne, g, p))
    bt = lambda a: jnp.transpose(a, (0, 3, 1, 2)).reshape(ne, SSM_GROUP, n)
    ct = lambda a: jnp.transpose(a, (0, 2, 1, 3)).reshape(ne, SSM_GROUP, n)
    vec = pl.BlockSpec((1, 1, n), lambda i: (i, 0, 0))
    mat = pl.BlockSpec((1, SSM_GROUP, n), lambda i: (i, 0, 0))
    tab = pl.BlockSpec((1, 4, SUBLANES, n), lambda i: (i, 0, 0, 0))
    stk = lambda cols: pl.BlockSpec((1, nblk, SUBLANES * rows, cols), lambda i: (i, 0, 0, 0))
    mspec = pl.BlockSpec((1, nblk, 2 * w, SUBLANES * rows), lambda i: (i, 0, 0, 0))
    return pl.pallas_call(
        _ssm_prep_kernel,
        grid=(ne,),
        in_specs=[vec, vec, vec, mat, mat, mat, mat],
        out_specs=[tab, tab, stk(rows), stk(2 * w), mspec],
        out_shape=[jax.ShapeDtypeStruct((ne, 4, SUBLANES, n), F32)] * 2
        + [jax.ShapeDtypeStruct((ne, nblk, SUBLANES * rows, rows), BF16),
           jax.ShapeDtypeStruct((ne, nblk, SUBLANES * rows, 2 * w), BF16),
           jax.ShapeDtypeStruct((ne, nblk, 2 * w, SUBLANES * rows), BF16)],
        compiler_params=pltpu.CompilerParams(vmem_limit_bytes=VMEM_LIMIT),
        name="ssm_prep",
    )(flat(lam_re), flat(lam_im), flat(ls), bt(b_re), bt(b_im), ct(c_re), ct(c_im))


def _bucket_table():
    i = np.arange(WINDOW)[:, None]
    j = np.arange(2 * WINDOW)[None, :]
    dist = WINDOW + i - j
    nn = np.maximum(dist, 0)
    max_exact = N_BUCKETS // 2
    nf = np.maximum(nn, max_exact).astype(np.float32)
    large = max_exact + (np.log(nf / np.float32(max_exact)) / np.float32(math.log(MAX_DISTANCE / max_exact))
                         * np.float32(N_BUCKETS - max_exact)).astype(np.int32)
    large = np.minimum(large, N_BUCKETS - 1)
    bucket = np.where(nn < max_exact, nn, large)
    valid = (dist >= 0) & (dist < WINDOW)
    return np.where(valid, bucket, -1).astype(np.int32)


def _bias_kernel(rb_ref, bucket_ref, out_ref):
    bucket = bucket_ref[...]
    nh = out_ref.shape[0]
    for h in range(nh):
        acc = jnp.full(bucket.shape, NEG_INF, F32)
        for b in range(N_BUCKETS):
            acc = jnp.where(bucket == b, rb_ref[b, h], acc)
        out_ref[h] = acc


def _bias_table(rel_bias):
    nh = rel_bias.shape[1]
    bucket = jnp.asarray(_bucket_table())
    return pl.pallas_call(
        _bias_kernel,
        in_specs=[pl.BlockSpec(memory_space=pltpu.SMEM),
                  pl.BlockSpec(bucket.shape, lambda: (0, 0))],
        out_specs=pl.BlockSpec((nh,) + bucket.shape, lambda: (0, 0, 0)),
        out_shape=jax.ShapeDtypeStruct((nh,) + bucket.shape, F32),
        name="rel_bias_table",
    )(rel_bias.astype(F32), bucket)


def _block_state_scan(sbuf, tre_ref, tim_ref, lane0, w, carry):
    lanes = pl.ds(lane0, w)
    first = lax.broadcasted_iota(jnp.int32, (SUBLANES, w), 0) == 0

    def body(b, carry):
        hr, hi = carry
        r0 = pl.multiple_of(b * SUBLANES, SUBLANES)
        xr = sbuf[pl.ds(r0, SUBLANES), 0:w]
        xi = sbuf[pl.ds(r0, SUBLANES), w:2 * w]
        for t, k in enumerate((1, 2, 4)):
            cr = tre_ref[0, t, :, lanes]
            ci = tim_ref[0, t, :, lanes]
            rr = pltpu.roll(xr, k, 0)
            ri = pltpu.roll(xi, k, 0)
            xr, xi = xr + cr * rr - ci * ri, xi + cr * ri + ci * rr
        pr = tre_ref[0, 3, :, lanes]
        pi = tim_ref[0, 3, :, lanes]
        xr, xi = xr + pr * hr - pi * hi, xi + pr * hi + pi * hr
        sbuf[pl.ds(r0, SUBLANES), 0:w] = jnp.where(first, hr, pltpu.roll(xr, 1, 0))
        sbuf[pl.ds(r0, SUBLANES), w:2 * w] = jnp.where(first, hi, pltpu.roll(xi, 1, 0))
        return (jnp.broadcast_to(xr[SUBLANES - 1:SUBLANES, :], (SUBLANES, w)),
                jnp.broadcast_to(xi[SUBLANES - 1:SUBLANES, :], (SUBLANES, w)))

    return lax.fori_loop(0, sbuf.shape[0] // SUBLANES, body, carry)


def _s5_mix(u, uext, yint, kst_ref, bst_ref, mst_ref, d, wglu_ref, bglu, advance):
    nblk = kst_ref.shape[1]
    cw = kst_ref.shape[3]
    rows = u.shape[0]
    nb = rows // SUBLANES
    j_i = lax.broadcasted_iota(jnp.int32, (1, SUBLANES, 1), 1)
    ys = []
    for blk in range(nblk):
        ub = u[:, blk * cw:(blk + 1) * cw]
        u3 = ub.reshape(nb, SUBLANES, cw)
        parts = [ub]
        for dd in range(1, SUBLANES):
            parts.append(jnp.where(j_i >= dd, pltpu.roll(u3, dd, 1), 0.0).reshape(rows, cw))
        for dd in range(SUBLANES):
            uext[dd] = parts[dd]
        y_intra = _dot(jnp.concatenate(parts, axis=1).astype(BF16), kst_ref[0, blk])
        last = [uext[dd, pl.ds(SUBLANES - 1, nb, stride=SUBLANES), :] for dd in range(SUBLANES)]
        s = _dot(jnp.concatenate(last, axis=1).astype(BF16), bst_ref[0, blk])
        hprev = advance(blk, s)
        yi = _dot(hprev.astype(BF16), mst_ref[0, blk])
        for j in range(SUBLANES):
            yint[pl.ds(j, nb, stride=SUBLANES), :] = yi[:, j * cw:(j + 1) * cw]
        ys.append(y_intra + yint[...])
    y = jnp.concatenate(ys, axis=1) + d * u
    g = jax.nn.gelu(y)
    return g * jax.nn.sigmoid(_dot(g.astype(BF16), wglu_ref[0]) + bglu)


def _mixer_prompt_kernel(*refs, n_pieces):
    if n_pieces:
        head_ref, piece_refs, refs = refs[0], refs[1:1 + n_pieces], refs[1 + n_pieces:]
    else:
        x_ref, refs = refs[0], refs[1:]
    (g_ref, wukv_ref, wqt_ref, wvt_ref, kst_ref, bst_ref, mst_ref, tre_ref, tim_ref,
     d_ref, wglu_ref, bglu_ref, bias_ref, sink_ref, wout_ref,
     o_ref, sre_ref, sim_ref, ko_ref, vo_ref,
     uext, yint, sbuf, hst, kbuf, vtbuf, mix, sc, pb) = refs
    t = pl.program_id(1)
    tm = o_ref.shape[1]
    d_ssm = d_ref.shape[-1]
    d_kv = N_KV_HEADS * HEAD_DIM
    gq = bias_ref.shape[3] // WINDOW
    nt_dims = (((1,), (1,)), ((), ()))

    @pl.when(t == 0)
    def _():
        hst[...] = jnp.zeros_like(hst)
        kbuf[0:WINDOW, :] = jnp.zeros((WINDOW, d_kv), F32)
        vtbuf[:, 0:WINDOW] = jnp.zeros((d_kv, WINDOW), F32)

    if n_pieces:
        first = jnp.where(t == 0, head_ref[...], piece_refs[0][0])
        x = jnp.concatenate([first] + [r[0] for r in piece_refs[1:]], axis=0)
    else:
        x = x_ref[0]
    h = _rmsnorm(x, g_ref[0]).astype(BF16)
    z = _dot(h, wukv_ref[0])
    u = z[:, :d_ssm]
    kbuf[WINDOW:, :] = z[:, d_ssm:d_ssm + d_kv]
    v = z[:, d_ssm + d_kv:]
    qt = lax.dot_general(wqt_ref[0], h, nt_dims, preferred_element_type=F32) * (HEAD_DIM ** -0.5)
    vtbuf[:, WINDOW:] = lax.dot_general(wvt_ref[0], h, nt_dims, preferred_element_type=F32)

    n_state = hst.shape[1] // 2

    w = sbuf.shape[1] // 2

    def advance(blk, s):
        lane0 = blk * w
        sbuf[...] = s
        init = (hst[:, lane0:lane0 + w], hst[:, n_state + lane0:n_state + lane0 + w])
        hr, hi = _block_state_scan(sbuf, tre_ref, tim_ref, lane0, w, init)
        hst[:, lane0:lane0 + w] = hr
        hst[:, n_state + lane0:n_state + lane0 + w] = hi
        return sbuf[...]

    ya = _s5_mix(u, uext, yint, kst_ref, bst_ref, mst_ref, d_ref[0], wglu_ref, bglu_ref[0], advance)
    mix[:, 0:d_ssm] = ya.astype(BF16)
    sre_ref[0] = hst[0:1, 0:n_state]
    sim_ref[0] = hst[0:1, n_state:]

    nqb = tm // WINDOW
    for qb in range(nqb):
        c0 = qb * WINDOW
        var = jnp.where(t == 0, qb + 1, 0) if qb + 1 < bias_ref.shape[0] else 0
        for j in range(N_KV_HEADS):
            kb = kbuf[c0:c0 + 2 * WINDOW, j * HEAD_DIM:(j + 1) * HEAD_DIM].astype(BF16)
            qc = jnp.concatenate(
                [qt[(j * gq + g) * HEAD_DIM:(j * gq + g + 1) * HEAD_DIM, c0:c0 + WINDOW] for g in range(gq)],
                axis=1).astype(BF16)
            sc[j, qb] = _dot(kb, qc) + bias_ref[var, j]
    dens = []
    for j in range(N_KV_HEADS):
        s = sc[j]
        sink = sink_ref[0, j][None]
        m = jnp.maximum(jnp.max(s, axis=1, keepdims=True), sink)
        p = jnp.exp(s - m)
        dens.append(jnp.sum(p, axis=1, keepdims=True) + jnp.exp(sink - m))
        pb[j] = p.astype(BF16)
    for qb in range(nqb):
        c0 = qb * WINDOW
        pieces = []
        for j in range(N_KV_HEADS):
            vb = vtbuf[j * HEAD_DIM:(j + 1) * HEAD_DIM, c0:c0 + 2 * WINDOW].astype(BF16)
            o = _dot(vb, pb[j, qb]) / dens[j][qb]
            pieces += [o[:, g * WINDOW:(g + 1) * WINDOW] for g in range(gq)]
        ot = jnp.concatenate(pieces, axis=0)
        mix[c0:c0 + WINDOW, d_ssm:] = ot.T.astype(BF16)

    kbuf[0:WINDOW, :] = kbuf[tm:tm + WINDOW, :]
    vtbuf[:, 0:WINDOW] = vtbuf[:, tm:tm + WINDOW]
    ko_ref[0] = kbuf[0:WINDOW, :]
    vo_ref[0] = v[tm - WINDOW:, :]
    o_ref[0] = x + _dot(mix[...], wout_ref[0])


def _mixer_prompt(x, head, idx, layer, wts, tm):
    nb, _, dm = x.shape
    tp = x.shape[1] + (0 if head is None else head.shape[0])
    nt = tp // tm
    n_state = wts["tre"].shape[-1]
    d_ssm = wts["ssm_d"].shape[-1]
    d_kv = N_KV_HEADS * HEAD_DIM
    w = wts["bst"].shape[-1] // 2
    cw = wts["kst"].shape[-1]
    layer_spec = lambda a, i: pl.BlockSpec((1,) + a.shape[1:], lambda b, t: (i,) + (0,) * (a.ndim - 1),
                                           pipeline_mode=pl.Buffered(1))
    if head is None:
        n_pieces = 0
        ins = [(x, pl.BlockSpec((1, tm, dm), lambda b, t: (b, t, 0)))]
    else:
        assert head.shape[0] == WINDOW
        n_pieces = tm // WINDOW
        piece = lambda i: pl.BlockSpec((1, WINDOW, dm), lambda b, t: (b, jnp.maximum(t * n_pieces + i - 1, 0), 0))
        ins = [(head, _const_spec(head.shape))] + [(x, piece(i)) for i in range(n_pieces)]
    ins += [
        (wts["g_mix"], layer_spec(wts["g_mix"], layer)),
        (wts["w_ukv"], layer_spec(wts["w_ukv"], idx)),
        (wts["w_qt"], layer_spec(wts["w_qt"], idx)),
        (wts["w_vt"], layer_spec(wts["w_vt"], idx)),
        (wts["kst"], layer_spec(wts["kst"], idx)),
        (wts["bst"], layer_spec(wts["bst"], idx)),
        (wts["mst"], layer_spec(wts["mst"], idx)),
        (wts["tre"], layer_spec(wts["tre"], idx)),
        (wts["tim"], layer_spec(wts["tim"], idx)),
        (wts["ssm_d"], layer_spec(wts["ssm_d"], idx)),
        (wts["w_glu"], layer_spec(wts["w_glu"], idx)),
        (wts["b_glu"], layer_spec(wts["b_glu"], idx)),
        (wts["bias_p"], _const_spec(wts["bias_p"].shape)),
        (wts["sink_p"], layer_spec(wts["sink_p"], idx)),
        (wts["w_out"], layer_spec(wts["w_out"], idx)),
    ]
    out_shape = [
        jax.ShapeDtypeStruct((nb, tp, dm), F32),
        jax.ShapeDtypeStruct((nb, 1, n_state), F32),
        jax.ShapeDtypeStruct((nb, 1, n_state), F32),
        jax.ShapeDtypeStruct((nb, WINDOW, d_kv), F32),
        jax.ShapeDtypeStruct((nb, WINDOW, d_kv), F32),
    ]
    out_specs = [
        pl.BlockSpec((1, tm, dm), lambda b, t: (b, t, 0)),
        pl.BlockSpec((1, 1, n_state), lambda b, t: (b, 0, 0)),
        pl.BlockSpec((1, 1, n_state), lambda b, t: (b, 0, 0)),
        pl.BlockSpec((1, WINDOW, d_kv), lambda b, t: (b, 0, 0)),
        pl.BlockSpec((1, WINDOW, d_kv), lambda b, t: (b, 0, 0)),
    ]
    scratch = [
        pltpu.VMEM((SUBLANES, tm, cw), F32),
        pltpu.VMEM((tm, cw), F32),
        pltpu.VMEM((tm // SUBLANES, 2 * w), F32),
        pltpu.VMEM((SUBLANES, 2 * n_state), F32),
        pltpu.VMEM((tm + WINDOW, d_kv), F32),
        pltpu.VMEM((d_kv, tm + WINDOW), F32),
        pltpu.VMEM((tm, dm), BF16),
        pltpu.VMEM((N_KV_HEADS, tm // WINDOW) + wts["bias_p"].shape[2:], F32),
        pltpu.VMEM((N_KV_HEADS, tm // WINDOW) + wts["bias_p"].shape[2:], BF16),
    ]
    assert tm // WINDOW >= wts["bias_p"].shape[0] - 1
    return pl.pallas_call(
        functools.partial(_mixer_prompt_kernel, n_pieces=n_pieces),
        grid=(nb, nt),
        in_specs=[s for _, s in ins],
        out_specs=out_specs,
        out_shape=out_shape,
        scratch_shapes=scratch,
        compiler_params=_cparams(("arbitrary", "arbitrary")),
        name=f"mixer_prompt_l{layer}",
    )(*[a for a, _ in ins])


def _mixer_sample_kernel(x_ref, h0r_ref, h0i_ref, kc_ref, vc_ref, g_ref, win_ref, kst_ref, bst_ref, mst_ref,
                         tre_ref, tim_ref, d_ref, wglu_ref, bglu_ref, bias_ref, sink_ref, wout_ref,
                         o_ref, sre_ref, sim_ref, ko_ref, vo_ref, uext, yint):
    rows = x_ref.shape[0]
    ns = kc_ref.shape[0]
    tq = rows // ns
    assert tq == SUBLANES
    w_rows = kc_ref.shape[1]
    d_ssm = d_ref.shape[-1]
    gq = bias_ref.shape[1] // tq
    d_q = N_KV_HEADS * gq * HEAD_DIM
    d_kv = N_KV_HEADS * HEAD_DIM

    x = x_ref[...]
    h = _rmsnorm(x, g_ref[0]).astype(BF16)
    z = _dot(h, win_ref[0])
    u = z[:, :d_ssm]
    q = (z[:, d_ssm:d_ssm + d_q] * (HEAD_DIM ** -0.5)).reshape(ns, tq, d_q)
    kn = z[:, d_ssm + d_q:d_ssm + d_q + d_kv].reshape(ns, tq, d_kv)
    vn = z[:, d_ssm + d_q + d_kv:].reshape(ns, tq, d_kv)

    w = bst_ref.shape[-1] // 2

    def advance(blk, s):
        lanes = slice(blk * w, (blk + 1) * w)
        hr = h0r_ref[:, lanes]
        hi = h0i_ref[:, lanes]
        ar = tre_ref[0, 0, SUBLANES - 1:SUBLANES, lanes]
        ai = tim_ref[0, 0, SUBLANES - 1:SUBLANES, lanes]
        sre_ref[:, lanes] = ar * hr - ai * hi + s[:, 0:w]
        sim_ref[:, lanes] = ar * hi + ai * hr + s[:, w:2 * w]
        return jnp.concatenate([hr, hi], axis=1)

    ya = _s5_mix(u, uext, yint, kst_ref, bst_ref, mst_ref, d_ref[0], wglu_ref, bglu_ref[0], advance)

    kc = jnp.concatenate([kc_ref[...], kn], axis=1)
    vc = jnp.concatenate([vc_ref[...], vn], axis=1)
    ko_ref[...] = kc[:, tq:, :]
    vo_ref[...] = vc[:, tq:, :]
    pieces = []
    for j in range(N_KV_HEADS):
        kb = kc[:, :, j * HEAD_DIM:(j + 1) * HEAD_DIM].astype(BF16)
        vb = vc[:, :, j * HEAD_DIM:(j + 1) * HEAD_DIM].astype(BF16)
        qs = jnp.concatenate(
            [q[:, :, (j * gq + g) * HEAD_DIM:(j * gq + g + 1) * HEAD_DIM] for g in range(gq)],
            axis=1).astype(BF16)
        s = jnp.einsum("nqd,nkd->nqk", qs, kb, preferred_element_type=F32) + bias_ref[j]
        sink = sink_ref[0, j]
        m = jnp.maximum(jnp.max(s, axis=-1, keepdims=True), sink)
        p = jnp.exp(s - m)
        l = jnp.sum(p, axis=-1, keepdims=True) + jnp.exp(sink - m)
        o = jnp.einsum("nqk,nkd->nqd", p.astype(BF16), vb, preferred_element_type=F32) / l
        pieces += [o[:, g * tq:(g + 1) * tq, :] for g in range(gq)]
    yb = jnp.concatenate(pieces, axis=2).reshape(rows, d_q)
    mixed = jnp.concatenate([ya, yb], axis=1).astype(BF16)
    o_ref[...] = x + _dot(mixed, wout_ref[0])


def _mixer_sample(x2, h0r, h0i, kc, vc, idx, layer, wts, ns):
    rows_all, dm = x2.shape
    _, n_seq, w_rows, d_kv = kc.shape
    tq = rows_all // n_seq
    n_state = h0r.shape[-1]
    cw = wts["kst"].shape[-1]
    layer_spec = lambda a, i: pl.BlockSpec((1,) + a.shape[1:], lambda s: (i,) + (0,) * (a.ndim - 1),
                                           pipeline_mode=pl.Buffered(1))
    row_spec = pl.BlockSpec((ns * tq, dm), lambda s: (s, 0))
    st_spec = pl.BlockSpec((ns, n_state), lambda s: (s, 0))
    kv_spec = pl.BlockSpec((ns, w_rows, d_kv), lambda s: (s, 0, 0))
    st_in = pl.BlockSpec((None, ns, n_state), lambda s: (idx, s, 0))
    kv_in = pl.BlockSpec((None, ns, w_rows, d_kv), lambda s: (idx, s, 0, 0))
    ins = [
        (x2, row_spec), (h0r, st_in), (h0i, st_in), (kc, kv_in), (vc, kv_in),
        (wts["g_mix"], layer_spec(wts["g_mix"], layer)),
        (wts["w_in"], layer_spec(wts["w_in"], idx)),
        (wts["kst"], layer_spec(wts["kst"], idx)),
        (wts["bst"], layer_spec(wts["bst"], idx)),
        (wts["mst"], layer_spec(wts["mst"], idx)),
        (wts["tre"], layer_spec(wts["tre"], idx)),
        (wts["tim"], layer_spec(wts["tim"], idx)),
        (wts["ssm_d"], layer_spec(wts["ssm_d"], idx)),
        (wts["w_glu"], layer_spec(wts["w_glu"], idx)),
        (wts["b_glu"], layer_spec(wts["b_glu"], idx)),
        (wts["bias_s"], _const_spec(wts["bias_s"].shape)),
        (wts["sink_s"], layer_spec(wts["sink_s"], idx)),
        (wts["w_out"], layer_spec(wts["w_out"], idx)),
    ]
    return pl.pallas_call(
        _mixer_sample_kernel,
        grid=(n_seq // ns,),
        in_specs=[s for _, s in ins],
        out_specs=[row_spec, st_spec, st_spec, kv_spec, kv_spec],
        out_shape=[jax.ShapeDtypeStruct(x2.shape, F32), jax.ShapeDtypeStruct(h0r.shape[1:], F32),
                   jax.ShapeDtypeStruct(h0i.shape[1:], F32), jax.ShapeDtypeStruct(kc.shape[1:], F32),
                   jax.ShapeDtypeStruct(vc.shape[1:], F32)],
        scratch_shapes=[pltpu.VMEM((SUBLANES, ns * tq, cw), F32), pltpu.VMEM((ns * tq, cw), F32)],
        compiler_params=_cparams(("arbitrary",)),
        name=f"mixer_sample_l{layer}",
    )(*[a for a, _ in ins])


def _conformer_head(x, g_ref, w1_ref, c):
    h = _rmsnorm(x, g_ref[0]).astype(BF16)
    z = _dot(h, w1_ref[0])
    return z[:, :c] * jax.nn.sigmoid(z[:, c:])


def _conformer_tail(y, bdw_ref, lng_ref, lnb_ref, w2_ref):
    y = y + bdw_ref[0]
    mu = jnp.mean(y, axis=-1, keepdims=True)
    yc = y - mu
    var = jnp.mean(yc * yc, axis=-1, keepdims=True)
    y = yc * lax.rsqrt(var + EPS) * lng_ref[0] + lnb_ref[0]
    return _dot(jax.nn.silu(y).astype(BF16), w2_ref[0])


def _conformer_prompt_kernel(x_ref, g_ref, w1_ref, wdw_ref, bdw_ref, lng_ref, lnb_ref, w2_ref, o_ref, st_ref,
                             ext, ybuf, *, pad_rows, width):
    t = pl.program_id(1)
    x = x_ref[0]
    nslab, le, lanes = ext.shape
    l = le - CONV_HALO
    c = nslab * lanes
    off = CONV_HALO - (width - 1)
    mrows = CONV_PHASE_VREGS * SUBLANES
    span = CONV_PHASES * mrows
    assert l % CONF_BLOCK == 0 and CONF_BLOCK % span == 0

    @pl.when(t == 0)
    def _():
        ext[:, 0:CONV_HALO, :] = jnp.zeros((nslab, CONV_HALO, lanes), F32)

    gl = _conformer_head(x, g_ref, w1_ref, c)
    for q in range(nslab):
        ext[q, CONV_HALO:, :] = gl[:, q * lanes:(q + 1) * lanes]
    st_ref[0] = jnp.concatenate([ext[q, l + off:, :] for q in range(nslab)], axis=1)

    def block(cb, _):
        r0 = pl.multiple_of(cb * CONF_BLOCK, CONF_BLOCK)
        for grp in range(CONF_BLOCK // span):
            g0 = r0 + grp * span
            for q0 in range(0, nslab, CONV_SLABS):
                slabs = range(q0, q0 + CONV_SLABS)
                acc = {(r, q): jnp.zeros((CONV_PHASE_VREGS, SUBLANES, lanes), F32)
                       for r in range(CONV_PHASES) for q in slabs}
                for s in range(width + CONV_PHASES - 1):
                    for q in slabs:
                        xs = ext[q, pl.ds(g0 + (off + s), mrows, stride=CONV_PHASES), :]
                        xs = xs.reshape(CONV_PHASE_VREGS, SUBLANES, lanes)
                        for r in range(CONV_PHASES):
                            k = s - r
                            if 0 <= k < width:
                                wk = wdw_ref[0, k, :, q * lanes:(q + 1) * lanes]
                                acc[r, q] = acc[r, q] + wk[None] * xs
                for r in range(CONV_PHASES):
                    for q in slabs:
                        ybuf[q, pl.ds(g0 + r, mrows, stride=CONV_PHASES), :] = acc[r, q].reshape(mrows, lanes)
        y = jnp.concatenate([ybuf[q, pl.ds(r0, CONF_BLOCK), :] for q in range(nslab)], axis=1)
        out = _conformer_tail(y, bdw_ref, lng_ref, lnb_ref, w2_ref)
        rid = t * l + r0 + lax.broadcasted_iota(jnp.int32, (CONF_BLOCK, 1), 0)
        o_ref[0, pl.ds(r0, CONF_BLOCK), :] = x_ref[0, pl.ds(r0, CONF_BLOCK), :] + jnp.where(rid >= pad_rows, out, 0.0)
        return 0

    lax.fori_loop(0, l // CONF_BLOCK, block, 0)
    ext[:, 0:CONV_HALO, :] = ext[:, l:l + CONV_HALO, :]


def _conformer_sample_kernel(x_ref, past_ref, g_ref, w1_ref, wdw_ref, bdw_ref, lng_ref, lnb_ref, w2_ref,
                             o_ref, st_ref, ext, ybuf, stage, *, width):
    x = x_ref[...]
    ns, le, c = ext.shape
    l = le - CONV_HALO
    off = CONV_HALO - (width - 1)
    ext[:, 0:off, :] = jnp.zeros((ns, off, c), F32)
    ext[:, off:CONV_HALO, :] = past_ref[...]
    ext[:, CONV_HALO:, :] = _conformer_head(x, g_ref, w1_ref, c).reshape(ns, l, c)
    st_ref[...] = ext[:, l + off:, :]

    def conv_unit(n, stg):
        win = ext[n]
        for s in range(1, SUBLANES):
            stg[s - 1] = win[s:s + l + CONV_HALO - SUBLANES]
        acc = jnp.zeros((l // SUBLANES, SUBLANES, c), F32)
        for k in range(width):
            a, s = divmod(off + k, SUBLANES)
            if s == 0:
                xk = ext[n, a * SUBLANES:a * SUBLANES + l, :]
            else:
                xk = stg[s - 1, a * SUBLANES:a * SUBLANES + l, :]
            acc = acc + wdw_ref[0, k][None] * xk.reshape(l // SUBLANES, SUBLANES, c)
        return acc.reshape(l, c)

    def seq(i, _):
        for half in range(2):
            n = 2 * i + half
            ybuf[pl.ds(pl.multiple_of(n * l, l), l), :] = conv_unit(n, stage.at[half])
        return 0
    assert ns % 2 == 0 and l % SUBLANES == 0
    lax.fori_loop(0, ns // 2, seq, 0)
    o_ref[...] = x + _conformer_tail(ybuf[...], bdw_ref, lng_ref, lnb_ref, w2_ref)


def _conformer(x, past, idx, layer, wts, tile, pad_rows):
    carry = past is None
    width = wts["w_dw"].shape[1]
    c = wts["w_dw"].shape[3]
    if carry:
        nb, tp, dm = x.shape
        grid = (nb, tp // tile)
        lspec = lambda a, i: pl.BlockSpec((1,) + a.shape[1:], lambda b, t: (i,) + (0,) * (a.ndim - 1),
                                          pipeline_mode=pl.Buffered(1))
        xspec = pl.BlockSpec((1, tile, dm), lambda b, t: (b, t, 0))
        ins = [(x, xspec)]
        st_shape = (nb, width - 1, c)
        st_spec = pl.BlockSpec((1, width - 1, c), lambda b, t: (b, 0, 0))
        ns, l = 1, tile
        sem = ("arbitrary", "arbitrary")
    else:
        rows_all, dm = x.shape
        n_seq = past.shape[1]
        l = rows_all // n_seq
        ns = tile
        grid = (n_seq // ns,)
        lspec = lambda a, i: pl.BlockSpec((1,) + a.shape[1:], lambda s: (i,) + (0,) * (a.ndim - 1),
                                          pipeline_mode=pl.Buffered(1))
        xspec = pl.BlockSpec((ns * l, dm), lambda s: (s, 0))
        ins = [(x, xspec), (past, pl.BlockSpec((None, ns, width - 1, c), lambda s: (idx, s, 0, 0)))]
        st_shape = (n_seq, width - 1, c)
        st_spec = pl.BlockSpec((ns, width - 1, c), lambda s: (s, 0, 0))
        sem = ("arbitrary",)
    ins += [
        (wts["g_mix"], lspec(wts["g_mix"], layer)),
        (wts["w_pw1"], lspec(wts["w_pw1"], idx)),
        (wts["w_dw"], lspec(wts["w_dw"], idx)),
        (wts["b_dw"], lspec(wts["b_dw"], idx)),
        (wts["ln_g"], lspec(wts["ln_g"], idx)),
        (wts["ln_b"], lspec(wts["ln_b"], idx)),
        (wts["w_pw2"], lspec(wts["w_pw2"], idx)),
    ]
    if carry:
        body = functools.partial(_conformer_prompt_kernel, pad_rows=pad_rows, width=width)
        scratch = [pltpu.VMEM((c // LANES, l + CONV_HALO, LANES), F32), pltpu.VMEM((c // LANES, l, LANES), F32)]
    else:
        body = functools.partial(_conformer_sample_kernel, width=width)
        scratch = [pltpu.VMEM((ns, l + CONV_HALO, c), F32), pltpu.VMEM((ns * l, c), F32),
                   pltpu.VMEM((2, SUBLANES - 1, l + CONV_HALO - SUBLANES, c), F32)]
    return pl.pallas_call(
        body,
        grid=grid,
        in_specs=[s for _, s in ins],
        out_specs=[xspec, st_spec],
        out_shape=[jax.ShapeDtypeStruct(x.shape, F32), jax.ShapeDtypeStruct(st_shape, F32)],
        scratch_shapes=scratch,
        compiler_params=_cparams(sem),
        name=f"conformer_{'prompt' if carry else 'sample'}_l{layer}",
    )(*[a for a, _ in ins])


def _ffn_kernel(*refs, carry, final_norm, n_pieces):
    if n_pieces:
        halo_ref, piece_refs, refs = refs[0], refs[1:1 + n_pieces], refs[1 + n_pieces:]
        x = jnp.concatenate([r[0] for r in piece_refs], axis=0)
    elif carry:
        x_ref, refs = refs[0], refs[1:]
        x = x_ref[0]
    else:
        x_ref, past_ref, refs = refs[0], refs[1], refs[2:]
        x = x_ref[...]
    (g_ref, wup_ref, wcv_ref, bcv_ref, wdn_ref, gf_ref, o_ref, st_ref, ext) = refs
    ns, le, dff = ext.shape
    l = le - FFN_HALO
    rows = ns * l
    kw = wcv_ref.shape[1]

    if carry:
        @pl.when(pl.program_id(1) == 0)
        def _():
            if n_pieces:
                hh = _rmsnorm(halo_ref[0, WINDOW - FFN_HALO:, :], g_ref[0]).astype(BF16)
                ext[:, 0:FFN_HALO, :] = _dot(hh, wup_ref[0, :, 0:dff]).reshape(ns, FFN_HALO, dff)
            else:
                ext[:, 0:FFN_HALO, :] = jnp.zeros((ns, FFN_HALO, dff), F32)
    else:
        ext[:, FFN_HALO - (kw - 1):FFN_HALO, :] = past_ref[...]

    h = _rmsnorm(x, g_ref[0]).astype(BF16)
    gate = _dot(h, wup_ref[0, :, 0:dff])
    up = _dot(h, wup_ref[0, :, dff:])
    ext[:, FFN_HALO:, :] = gate.reshape(ns, l, dff)
    gc = bcv_ref[0]
    for k in range(kw):
        o = FFN_HALO - (kw - 1) + k
        gc = gc + wcv_ref[0, k:k + 1, :] * ext[:, o:o + l, :]
    y = (jax.nn.gelu(gc).reshape(rows, dff) * up).astype(BF16)
    acc = x + _dot(y, wdn_ref[0])
    st_ref[...] = ext[:, le - (kw - 1):, :].reshape(st_ref.shape)
    if final_norm:
        acc = _rmsnorm(acc, gf_ref[...])
    if carry:
        o_ref[0] = acc
        ext[:, 0:FFN_HALO, :] = ext[:, l:l + FFN_HALO, :]
    else:
        o_ref[...] = acc


def _ffn(x, past, layer, wts, tile, final_norm, skip_rows=0):
    carry = past is None
    dff = wts["w_dn"].shape[1]
    kw = wts["w_cv"].shape[1]
    n_pieces = 0
    if carry:
        nb, tp, dm = x.shape
        out_rows = tp - skip_rows
        grid = (nb, out_rows // tile)
        lspec = lambda a, i: pl.BlockSpec((1,) + a.shape[1:], lambda b, t: (i,) + (0,) * (a.ndim - 1),
                                          pipeline_mode=pl.Buffered(1))
        xspec = pl.BlockSpec((1, tile, dm), lambda b, t: (b, t, 0))
        if skip_rows:
            assert skip_rows % WINDOW == 0 and tile % WINDOW == 0 and out_rows % tile == 0
            n_pieces, skip = tile // WINDOW, skip_rows // WINDOW
            piece = lambda i: pl.BlockSpec((1, WINDOW, dm), lambda b, t: (b, skip + t * n_pieces + i, 0))
            ins = [(x, piece(-1))] + [(x, piece(i)) for i in range(n_pieces)]
        else:
            ins = [(x, xspec)]
        out_struct = jax.ShapeDtypeStruct((nb, out_rows, dm), F32)
        st_shape = (nb, kw - 1, dff)
        st_spec = pl.BlockSpec((1, kw - 1, dff), lambda b, t: (b, 0, 0))
        ns, l = 1, tile
        sem = ("arbitrary", "arbitrary")
    else:
        out_struct = jax.ShapeDtypeStruct(x.shape, F32)
        rows_all, dm = x.shape
        n_seq = past.shape[1]
        l = rows_all // n_seq
        ns = tile
        grid = (n_seq // ns,)
        lspec = lambda a, i: pl.BlockSpec((1,) + a.shape[1:], lambda s: (i,) + (0,) * (a.ndim - 1),
                                          pipeline_mode=pl.Buffered(1))
        xspec = pl.BlockSpec((ns * l, dm), lambda s: (s, 0))
        ins = [(x, xspec), (past, pl.BlockSpec((None, ns, kw - 1, dff), lambda s: (layer, s, 0, 0)))]
        st_shape = (n_seq, kw - 1, dff)
        st_spec = pl.BlockSpec((ns, kw - 1, dff), lambda s: (s, 0, 0))
        sem = ("arbitrary",)
    ins += [
        (wts["g_ffn"], lspec(wts["g_ffn"], layer)),
        (wts["w_up"], lspec(wts["w_up"], layer)),
        (wts["w_cv"], lspec(wts["w_cv"], layer)),
        (wts["b_cv"], lspec(wts["b_cv"], layer)),
        (wts["w_dn"], lspec(wts["w_dn"], layer)),
        (wts["g_final"], _const_spec(wts["g_final"].shape)),
    ]
    return pl.pallas_call(
        functools.partial(_ffn_kernel, carry=carry, final_norm=final_norm, n_pieces=n_pieces),
        grid=grid,
        in_specs=[s for _, s in ins],
        out_specs=[xspec, st_spec],
        out_shape=[out_struct, jax.ShapeDtypeStruct(st_shape, F32)],
        scratch_shapes=[pltpu.VMEM((ns, l + FFN_HALO, dff), F32)],
        compiler_params=_cparams(sem),
        name=f"ffn_{'prompt' if carry else 'sample'}_l{layer}",
    )(*[a for a, _ in ins])


def _pick_tile(total, target):
    best = WINDOW
    for m in range(1, total // WINDOW + 1):
        if total % (m * WINDOW) == 0 and m * WINDOW <= target:
            best = m * WINDOW
    return best


def kernel(x_prompt, x_sample, state_ssm_re, state_ssm_im, cache_swa_k, cache_swa_v, state_conv, state_ffn,
           meta_tokens, g_mix, g_ffn, g_final, w_in_mix, ssm_lambda_re, ssm_lambda_im, ssm_log_step,
           ssm_b_re, ssm_b_im, ssm_c_re, ssm_c_im, ssm_d, ssm_w_glu, ssm_b_glu, rel_bias, attn_sinks,
           w_out_mix, conv_w_pw1, conv_w_dw, conv_b_dw, conv_ln_g, conv_ln_b, conv_w_pw2,
           ffn_w_up, ffn_w_conv, ffn_b_conv, ffn_w_down):
    nb, seq, dm = x_prompt.shape
    n_seq, tq, _ = x_sample.shape
    depth = g_mix.shape[0]
    n_meta = meta_tokens.shape[0]
    n_even, g_ssm, p_ssm = ssm_lambda_re.shape
    n_heads = rel_bias.shape[1]
    gq = n_heads // N_KV_HEADS
    w_rows = cache_swa_k.shape[2]
    d_kv = N_KV_HEADS * HEAD_DIM
    conv_w = conv_w_dw.shape[1]
    ffn_w = ffn_w_conv.shape[1]
    assert tq == SUBLANES and w_rows == WINDOW and seq % WINDOW == 0 and n_meta <= WINDOW
    assert conv_w - 1 <= CONV_HALO and ffn_w - 1 <= FFN_HALO

    tre, tim, kst, bst, mst = _ssm_prep(ssm_lambda_re, ssm_lambda_im, ssm_log_step,
                                        ssm_b_re, ssm_b_im, ssm_c_re, ssm_c_im)
    bias = _bias_table(rel_bias)
    row3 = lambda a: a.reshape(a.shape[0], 1, a.shape[-1]).astype(F32)
    sinks = attn_sinks.astype(F32).reshape(n_even, N_KV_HEADS, gq, 1)
    pad_rows = WINDOW - n_meta
    bias_t = jnp.transpose(bias.reshape(N_KV_HEADS, gq, WINDOW, 2 * WINDOW), (0, 3, 1, 2)
                           ).reshape(N_KV_HEADS, 2 * WINDOW, gq * WINDOW)
    key_i = jnp.arange(2 * WINDOW)[None, :, None]
    n_var = 1 + -(-(pad_rows + WINDOW) // WINDOW)
    bias_p = jnp.stack([bias_t] + [jnp.where(key_i >= pad_rows + WINDOW - qb * WINDOW, bias_t, NEG_INF)
                                   for qb in range(n_var - 1)])
    d_ssm = ssm_d.shape[-1]
    d_q = n_heads * HEAD_DIM
    w_in_bf = w_in_mix.astype(BF16)
    wts = {
        "g_mix": row3(g_mix), "g_ffn": row3(g_ffn), "g_final": g_final.reshape(1, dm).astype(F32),
        "w_in": w_in_bf, "kst": kst, "bst": bst, "mst": mst, "tre": tre, "tim": tim,
        "w_ukv": jnp.concatenate([w_in_bf[:, :, :d_ssm], w_in_bf[:, :, d_ssm + d_q:]], axis=-1),
        "w_qt": jnp.transpose(w_in_bf[:, :, d_ssm:d_ssm + d_q], (0, 2, 1)),
        "w_vt": jnp.transpose(w_in_bf[:, :, d_ssm + d_q + d_kv:], (0, 2, 1)),
        "ssm_d": row3(ssm_d), "w_glu": ssm_w_glu.astype(BF16), "b_glu": row3(ssm_b_glu),
        "bias_p": bias_p,
        "bias_s": bias[:, :tq, :w_rows + tq].reshape(N_KV_HEADS, gq * tq, w_rows + tq),
        "sink_p": jnp.broadcast_to(sinks[:, :, :, None, :], (n_even, N_KV_HEADS, gq, WINDOW, 1)
                                   ).reshape(n_even, N_KV_HEADS, 1, gq * WINDOW),
        "sink_s": jnp.broadcast_to(sinks[:, :, :, None, :], (n_even, N_KV_HEADS, gq, tq, 1)
                                   ).reshape(n_even, N_KV_HEADS, gq * tq, 1),
        "w_out": w_out_mix.astype(BF16),
        "w_pw1": conv_w_pw1.astype(BF16), "w_dw": jnp.broadcast_to(conv_w_dw.astype(F32)[:, :, None, :],
                                 conv_w_dw.shape[:2] + (SUBLANES, conv_w_dw.shape[2])), "b_dw": row3(conv_b_dw),
        "ln_g": row3(conv_ln_g), "ln_b": row3(conv_ln_b), "w_pw2": conv_w_pw2.astype(BF16),
        "w_up": ffn_w_up.astype(BF16), "w_cv": ffn_w_conv.astype(F32), "b_cv": row3(ffn_b_conv),
        "w_dn": ffn_w_down.astype(BF16),
    }

    tp = pad_rows + n_meta + seq
    tm = _pick_tile(tp, PROMPT_TILE_TARGET)
    xp = x_prompt.astype(F32)
    head = jnp.concatenate([jnp.zeros((pad_rows, dm), F32), meta_tokens.astype(F32)], axis=0)
    xs = x_sample.astype(F32).reshape(n_seq * tq, dm)
    ns = min(SAMPLE_SEQS, n_seq)
    assert n_seq % ns == 0

    h0r_all = state_ssm_re.astype(F32).reshape(n_even, n_seq, g_ssm * p_ssm)
    h0i_all = state_ssm_im.astype(F32).reshape(n_even, n_seq, g_ssm * p_ssm)
    kc_all = cache_swa_k.astype(F32).reshape(n_even, n_seq, w_rows, d_kv)
    vc_all = cache_swa_v.astype(F32).reshape(n_even, n_seq, w_rows, d_kv)
    conv_all = state_conv.astype(F32)
    ffn_all = state_ffn.astype(F32)

    sr_p, si_p, k_p, v_p, c_p, f_p = [], [], [], [], [], []
    sr_s, si_s, k_s, v_s, c_s, f_s = [], [], [], [], [], []
    for layer in range(depth):
        idx = layer // 2
        if layer % 2 == 0:
            xp, sre, sim, ko, vo = _mixer_prompt(xp, head if layer == 0 else None, idx, layer, wts, tm)
            sr_p.append(sre.reshape(nb, g_ssm, p_ssm))
            si_p.append(sim.reshape(nb, g_ssm, p_ssm))
            k_p.append(ko.reshape(nb, WINDOW, N_KV_HEADS, HEAD_DIM))
            v_p.append(vo.reshape(nb, WINDOW, N_KV_HEADS, HEAD_DIM))
            xs, sre, sim, ko, vo = _mixer_sample(xs, h0r_all, h0i_all, kc_all, vc_all, idx, layer, wts, ns)
            sr_s.append(sre.reshape(n_seq, g_ssm, p_ssm))
            si_s.append(sim.reshape(n_seq, g_ssm, p_ssm))
            k_s.append(ko.reshape(n_seq, w_rows, N_KV_HEADS, HEAD_DIM))
            v_s.append(vo.reshape(n_seq, w_rows, N_KV_HEADS, HEAD_DIM))
        else:
            xp, st = _conformer(xp, None, idx, layer, wts, tm, pad_rows)
            c_p.append(st)
            xs, st = _conformer(xs, conv_all, idx, layer, wts, ns, 0)
            c_s.append(st)
        last = layer == depth - 1
        if last:
            xp, st = _ffn(xp, None, layer, wts, _pick_tile(seq, PROMPT_TILE_TARGET), True,
                          skip_rows=pad_rows + n_meta)
        else:
            xp, st = _ffn(xp, None, layer, wts, tm, False)
        f_p.append(st)
        xs, st = _ffn(xs, ffn_all, layer, wts, ns, last)
        f_s.append(st)

    yp = xp
    ys = xs.reshape(n_seq, tq, dm)
    st = jnp.stack
    return (yp, ys, st(sr_p), st(si_p), st(k_p), st(v_p), st(c_p), st(f_p),
            st(sr_s), st(si_s), st(k_s), st(v_s), st(c_s), st(f_s))
```

```python
import functools
import math

import numpy as np
import jax
import jax.numpy as jnp
from jax import lax
from jax.experimental import pallas as pl
from jax.experimental.pallas import tpu as pltpu

F32 = jnp.float32
BF16 = jnp.bfloat16

EPS = 1e-6
NEG_INF = -1e30
WINDOW = 128
HEAD_DIM = 64
N_KV_HEADS = 2
N_BUCKETS = 32
MAX_DISTANCE = 128
SSM_GROUP = 16
SSM_STATE = 64
SSM_BLK_GROUPS = 8
SSM_SUB_GROUPS = 4
SUBLANES = 8
CONV_HALO = 32
FFN_HALO = 8
LANES = 128
CONV_PHASES = 4
CONV_PHASE_VREGS = 4
CONV_SLABS = 2
CONF_BLOCK = 128
PROMPT_TILE_TARGET = 768
SAMPLE_SEQS = 32
VMEM_LIMIT = 56 * 1024 * 1024


def _cparams(sem):
    return pltpu.CompilerParams(dimension_semantics=sem, vmem_limit_bytes=VMEM_LIMIT)


def _const_spec(shape):
    nd = len(shape)
    return pl.BlockSpec(shape, lambda *_: (0,) * nd, pipeline_mode=pl.Buffered(1))


def _rmsnorm(x, g):
    return x * lax.rsqrt(jnp.mean(x * x, axis=-1, keepdims=True) + EPS) * g


def _dot(a, b):
    return jnp.dot(a, b, preferred_element_type=F32)


def _ssm_prep_kernel(lr_ref, li_ref, ls_ref, bre_ref, bim_ref, cre_ref, cim_ref,
                     tre_ref, tim_ref, kst_ref, bst_ref, mst_ref):
    lr = lr_ref[0]
    li = li_ref[0]
    dt = jnp.exp(ls_ref[0])
    decay = jnp.exp(lr * dt)
    a_re = decay * jnp.cos(li * dt)
    a_im = decay * jnp.sin(li * dt)
    den = lr * lr + li * li
    num_re = a_re - 1.0
    coef_re = (num_re * lr + a_im * li) / den
    coef_im = (a_im * lr - num_re * li) / den

    cmul = lambda x, y: (x[0] * y[0] - x[1] * y[1], x[0] * y[1] + x[1] * y[0])
    n = lr.shape[-1]
    pw = [(jnp.ones((1, n), F32), jnp.zeros((1, n), F32)), (a_re, a_im)]
    for _ in range(SUBLANES - 1):
        pw.append(cmul(pw[-1], (a_re, a_im)))
    apw = [pw[SUBLANES]]
    for _ in range(SUBLANES - 1):
        apw.append(cmul(apw[-1], pw[SUBLANES]))
    row = lax.broadcasted_iota(jnp.int32, (SUBLANES, n), 0)
    zero = jnp.zeros((SUBLANES, n), F32)
    for t, k in enumerate((1, 2, 4)):
        tre_ref[0, t] = jnp.where(row >= k, jnp.broadcast_to(apw[k - 1][0], (SUBLANES, n)), zero)
        tim_ref[0, t] = jnp.where(row >= k, jnp.broadcast_to(apw[k - 1][1], (SUBLANES, n)), zero)
    ap_re, ap_im = zero, zero
    for j in range(SUBLANES):
        ap_re = jnp.where(row == j, jnp.broadcast_to(apw[j][0], (SUBLANES, n)), ap_re)
        ap_im = jnp.where(row == j, jnp.broadcast_to(apw[j][1], (SUBLANES, n)), ap_im)
    tre_ref[0, 3] = ap_re
    tim_ref[0, 3] = ap_im

    bre = bre_ref[0]
    bim = bim_ref[0]
    bbar = (coef_re * bre - coef_im * bim, coef_re * bim + coef_im * bre)
    rows = SSM_BLK_GROUPS * SSM_GROUP
    r_i = lax.broadcasted_iota(jnp.int32, (rows, n), 0)
    c_i = lax.broadcasted_iota(jnp.int32, (rows, n), 1)
    sh = lambda v, d: lax.shift_right_logical(v, jnp.full(v.shape, int(math.log2(d)), jnp.int32))
    diag = sh(r_i, SSM_GROUP) == (sh(c_i, SSM_STATE) & (SSM_BLK_GROUPS - 1))
    zer = jnp.zeros((rows, n), F32)
    blockdiag = lambda v: jnp.where(diag, jnp.concatenate([v] * SSM_BLK_GROUPS, axis=0), zer)
    bb = (blockdiag(bbar[0]), blockdiag(bbar[1]))
    cc = (blockdiag(cre_ref[0]), blockdiag(cim_ref[0]))
    w = SSM_BLK_GROUPS * SSM_STATE
    nblk = n // w
    for blk in range(nblk):
        sl = slice(blk * w, (blk + 1) * w)
        bbs = (bb[0][:, sl], bb[1][:, sl])
        ccs = (cc[0][:, sl], cc[1][:, sl])
        bcats = []
        nsub = SSM_BLK_GROUPS // SSM_SUB_GROUPS
        lw, sw = SSM_SUB_GROUPS * SSM_GROUP, SSM_SUB_GROUPS * SSM_STATE
        bsub = [[] for _ in range(nsub)]
        msub = [[] for _ in range(nsub)]
        for d in range(SUBLANES):
            ba = cmul(bbs, (pw[d][0][:, sl], pw[d][1][:, sl]))
            ca = cmul(ccs, (pw[d + 1][0][:, sl], pw[d + 1][1][:, sl]))
            bcats.append(jnp.concatenate([ba[0], ba[1]], axis=1))
            for sb in range(nsub):
                rr, cs = slice(sb * lw, (sb + 1) * lw), slice(sb * sw, (sb + 1) * sw)
                bsub[sb].append(jnp.concatenate([ba[0][rr, cs], ba[1][rr, cs]], axis=1))
                msub[sb].append(jnp.concatenate([ca[0][rr, cs], -ca[1][rr, cs]], axis=1))
        for sb in range(nsub):
            bst_ref[0, blk, sb] = jnp.concatenate(bsub[sb], axis=0).astype(BF16)
            mst_ref[0, blk, sb] = jnp.concatenate(msub[sb], axis=0).T.astype(BF16)
        ccat = jnp.concatenate([ccs[0], -ccs[1]], axis=1)
        split = lambda v: (v.astype(BF16), (v - v.astype(BF16).astype(F32)).astype(BF16))
        (bh, bl), (ch, cl) = split(jnp.concatenate(bcats, axis=0)), split(ccat)
        nt = lambda p, q: lax.dot_general(p, q, (((1,), (1,)), ((), ())), preferred_element_type=F32)
        kst_ref[0, blk] = (nt(bh, ch) + nt(bh, cl) + nt(bl, ch)).astype(BF16)


def _ssm_prep(lam_re, lam_im, log_step, b_re, b_im, c_re, c_im):
    ne, g, p = lam_re.shape
    n = g * p
    assert g % SSM_BLK_GROUPS == 0 and p == SSM_STATE and b_re.shape[-1] == SSM_GROUP
    nblk = g // SSM_BLK_GROUPS
    rows = SSM_BLK_GROUPS * SSM_GROUP
    w = SSM_BLK_GROUPS * SSM_STATE
    flat = lambda a: a.reshape(ne, 1, n)
    ls = jnp.broadcast_to(log_step[:, :, None], (ne, g, p))
    bt = lambda a: jnp.transpose(a, (0, 3, 1, 2)).reshape(ne, SSM_GROUP, n)
    ct = lambda a: jnp.transpose(a, (0, 2, 1, 3)).reshape(ne, SSM_GROUP, n)
    vec = pl.BlockSpec((1, 1, n), lambda i: (i, 0, 0))
    mat = pl.BlockSpec((1, SSM_GROUP, n), lambda i: (i, 0, 0))
    tab = pl.BlockSpec((1, 4, SUBLANES, n), lambda i: (i, 0, 0, 0))
    assert SSM_BLK_GROUPS % SSM_SUB_GROUPS == 0
    nsub = SSM_BLK_GROUPS // SSM_SUB_GROUPS
    sub_in = SUBLANES * SSM_SUB_GROUPS * SSM_GROUP
    sub_st = 2 * SSM_SUB_GROUPS * SSM_STATE
    kspec = pl.BlockSpec((1, nblk, SUBLANES * rows, rows), lambda i: (i, 0, 0, 0))
    bspec = pl.BlockSpec((1, nblk, nsub, sub_in, sub_st), lambda i: (i, 0, 0, 0, 0))
    mspec = pl.BlockSpec((1, nblk, nsub, sub_st, sub_in), lambda i: (i, 0, 0, 0, 0))
    return pl.pallas_call(
        _ssm_prep_kernel,
        grid=(ne,),
        in_specs=[vec, vec, vec, mat, mat, mat, mat],
        out_specs=[tab, tab, kspec, bspec, mspec],
        out_shape=[jax.ShapeDtypeStruct((ne, 4, SUBLANES, n), F32)] * 2
        + [jax.ShapeDtypeStruct((ne, nblk, SUBLANES * rows, rows), BF16),
           jax.ShapeDtypeStruct((ne, nblk, nsub, sub_in, sub_st), BF16),
           jax.ShapeDtypeStruct((ne, nblk, nsub, sub_st, sub_in), BF16)],
        compiler_params=pltpu.CompilerParams(vmem_limit_bytes=VMEM_LIMIT),
        name="ssm_prep",
    )(flat(lam_re), flat(lam_im), flat(ls), bt(b_re), bt(b_im), ct(c_re), ct(c_im))


def _bucket_table():
    i = np.arange(WINDOW)[:, None]
    j = np.arange(2 * WINDOW)[None, :]
    dist = WINDOW + i - j
    nn = np.maximum(dist, 0)
    max_exact = N_BUCKETS // 2
    nf = np.maximum(nn, max_exact).astype(np.float32)
    large = max_exact + (np.log(nf / np.float32(max_exact)) / np.float32(math.log(MAX_DISTANCE / max_exact))
                         * np.float32(N_BUCKETS - max_exact)).astype(np.int32)
    large = np.minimum(large, N_BUCKETS - 1)
    bucket = np.where(nn < max_exact, nn, large)
    valid = (dist >= 0) & (dist < WINDOW)
    return np.where(valid, bucket, -1).astype(np.int32)


def _bias_kernel(rb_ref, bucket_ref, out_ref):
    bucket = bucket_ref[...]
    nh = out_ref.shape[0]
    for h in range(nh):
        acc = jnp.full(bucket.shape, NEG_INF, F32)
        for b in range(N_BUCKETS):
            acc = jnp.where(bucket == b, rb_ref[b, h], acc)
        out_ref[h] = acc


def _bias_table(rel_bias):
    nh = rel_bias.shape[1]
    bucket = jnp.asarray(_bucket_table())
    return pl.pallas_call(
        _bias_kernel,
        in_specs=[pl.BlockSpec(memory_space=pltpu.SMEM),
                  pl.BlockSpec(bucket.shape, lambda: (0, 0))],
        out_specs=pl.BlockSpec((nh,) + bucket.shape, lambda: (0, 0, 0)),
        out_shape=jax.ShapeDtypeStruct((nh,) + bucket.shape, F32),
        name="rel_bias_table",
    )(rel_bias.astype(F32), bucket)


def _block_state_scan(sbuf, hst, tre_ref, tim_ref):
    nblk, nb, w2 = sbuf.shape
    w = w2 // 2
    n_state = hst.shape[1] // 2
    first = lax.broadcasted_iota(jnp.int32, (SUBLANES, w), 0) == 0

    def body(b, _):
        r0 = pl.multiple_of(b * SUBLANES, SUBLANES)
        for blk in range(nblk):
            lanes = pl.ds(blk * w, w)
            hr = hst[:, blk * w:(blk + 1) * w]
            hi = hst[:, n_state + blk * w:n_state + (blk + 1) * w]
            xr = sbuf[blk, pl.ds(r0, SUBLANES), 0:w]
            xi = sbuf[blk, pl.ds(r0, SUBLANES), w:2 * w]
            for t, k in enumerate((1, 2, 4)):
                cr = tre_ref[0, t, :, lanes]
                ci = tim_ref[0, t, :, lanes]
                rr = pltpu.roll(xr, k, 0)
                ri = pltpu.roll(xi, k, 0)
                xr, xi = xr + cr * rr - ci * ri, xi + cr * ri + ci * rr
            pr = tre_ref[0, 3, :, lanes]
            pi = tim_ref[0, 3, :, lanes]
            xr, xi = xr + pr * hr - pi * hi, xi + pr * hi + pi * hr
            sbuf[blk, pl.ds(r0, SUBLANES), 0:w] = jnp.where(first, hr, pltpu.roll(xr, 1, 0))
            sbuf[blk, pl.ds(r0, SUBLANES), w:2 * w] = jnp.where(first, hi, pltpu.roll(xi, 1, 0))
            hst[:, blk * w:(blk + 1) * w] = jnp.broadcast_to(xr[SUBLANES - 1:SUBLANES, :], (SUBLANES, w))
            hst[:, n_state + blk * w:n_state + (blk + 1) * w] = jnp.broadcast_to(
                xi[SUBLANES - 1:SUBLANES, :], (SUBLANES, w))
        return 0

    lax.fori_loop(0, nb // SUBLANES, body, 0)


def _s5_mix(u, uext, yint, kst_ref, bst_ref, mst_ref, d, wglu_ref, bglu, advance):
    nblk = kst_ref.shape[1]
    cw = kst_ref.shape[3]
    nsub = bst_ref.shape[2]
    lw, sw = cw // nsub, bst_ref.shape[4] // 2
    rows = u.shape[0]
    nb = rows // SUBLANES
    j_i = lax.broadcasted_iota(jnp.int32, (1, SUBLANES, 1), 1)
    y_intra, s_all = [], []
    for blk in range(nblk):
        ub = u[:, blk * cw:(blk + 1) * cw]
        u3 = ub.reshape(nb, SUBLANES, cw)
        parts = [ub]
        for dd in range(1, SUBLANES):
            parts.append(jnp.where(j_i >= dd, pltpu.roll(u3, dd, 1), 0.0).reshape(rows, cw))
        for dd in range(SUBLANES):
            uext[dd] = parts[dd]
        y_intra.append(_dot(jnp.concatenate(parts, axis=1).astype(BF16), kst_ref[0, blk]))
        last = [uext[dd, pl.ds(SUBLANES - 1, nb, stride=SUBLANES), :] for dd in range(SUBLANES)]
        s_sub = []
        for sb in range(nsub):
            ul = jnp.concatenate([v[:, sb * lw:(sb + 1) * lw] for v in last], axis=1)
            s_sub.append(_dot(ul.astype(BF16), bst_ref[0, blk, sb]))
        s_all.append(jnp.concatenate([v[:, :sw] for v in s_sub] + [v[:, sw:] for v in s_sub], axis=1))
    h_all = advance(s_all)
    w = nsub * sw
    ys = []
    for blk in range(nblk):
        hprev = h_all[blk]
        yi = [_dot(jnp.concatenate([hprev[:, sb * sw:(sb + 1) * sw], hprev[:, w + sb * sw:w + (sb + 1) * sw]],
                                   axis=1).astype(BF16), mst_ref[0, blk, sb]) for sb in range(nsub)]
        for j in range(SUBLANES):
            yint[pl.ds(j, nb, stride=SUBLANES), :] = jnp.concatenate(
                [v[:, j * lw:(j + 1) * lw] for v in yi], axis=1)
        ys.append(y_intra[blk] + yint[...])
    y = jnp.concatenate(ys, axis=1) + d * u
    g = jax.nn.gelu(y)
    return g * jax.nn.sigmoid(_dot(g.astype(BF16), wglu_ref[0]) + bglu)


def _mixer_prompt_kernel(*refs, n_pieces, pad_rows):
    if n_pieces:
        head_ref, piece_refs, refs = refs[0], refs[1:1 + n_pieces], refs[1 + n_pieces:]
    else:
        x_ref, refs = refs[0], refs[1:]
    (g_ref, wukv_ref, wqt_ref, wvt_ref, kst_ref, bst_ref, mst_ref, tre_ref, tim_ref,
     d_ref, wglu_ref, bglu_ref, bias_ref, sink_ref, wout_ref,
     o_ref, sre_ref, sim_ref, ko_ref, vo_ref,
     uext, yint, sbuf, hst, kbuf, vtbuf, mix, sc, pb) = refs
    t = pl.program_id(1)
    tm = o_ref.shape[1]
    d_ssm = d_ref.shape[-1]
    d_kv = N_KV_HEADS * HEAD_DIM
    gq = bias_ref.shape[3] // WINDOW
    nt_dims = (((1,), (1,)), ((), ()))

    @pl.when(t == 0)
    def _():
        hst[...] = jnp.zeros_like(hst)
        kbuf[0:WINDOW, :] = jnp.zeros((WINDOW, d_kv), F32)
        vtbuf[:, 0:WINDOW] = jnp.zeros((d_kv, WINDOW), F32)

    if n_pieces:
        first = jnp.where(t == 0, head_ref[...], piece_refs[0][0])
        x = jnp.concatenate([first] + [r[0] for r in piece_refs[1:]], axis=0)
    else:
        x = x_ref[0]
    h = _rmsnorm(x, g_ref[0]).astype(BF16)
    z = _dot(h, wukv_ref[0])
    u = z[:, :d_ssm]
    kbuf[WINDOW:, :] = z[:, d_ssm:d_ssm + d_kv]
    v = z[:, d_ssm + d_kv:]
    qt = lax.dot_general(wqt_ref[0], h, nt_dims, preferred_element_type=F32)
    vtbuf[:, WINDOW:] = lax.dot_general(wvt_ref[0], h, nt_dims, preferred_element_type=F32)

    n_state = hst.shape[1] // 2

    def advance(s_list):
        for blk, s in enumerate(s_list):
            sbuf[blk] = s
        _block_state_scan(sbuf, hst, tre_ref, tim_ref)
        return [sbuf[blk] for blk in range(len(s_list))]

    nqb = tm // WINDOW
    for qb in range(nqb):
        c0 = qb * WINDOW
        var = jnp.where(t == 0, qb + 1, 0) if qb + 1 < bias_ref.shape[0] else 0
        for j in range(N_KV_HEADS):
            kb = kbuf[c0:c0 + 2 * WINDOW, j * HEAD_DIM:(j + 1) * HEAD_DIM].astype(BF16)
            qc = jnp.concatenate(
                [qt[(j * gq + g) * HEAD_DIM:(j * gq + g + 1) * HEAD_DIM, c0:c0 + WINDOW] for g in range(gq)],
                axis=1).astype(BF16)
            sc[j, qb] = _dot(kb, qc) + bias_ref[var, j]
    dens = []
    for j in range(N_KV_HEADS):
        s = sc[j]
        sink = sink_ref[0, j][None]
        m = jnp.maximum(jnp.max(s, axis=1, keepdims=True), sink)
        p = jnp.exp(s - m)
        dens.append(jnp.sum(p, axis=1, keepdims=True) + jnp.exp(sink - m))
        pb[j] = p.astype(BF16)
    for qb in range(nqb):
        c0 = qb * WINDOW
        pieces = []
        for j in range(N_KV_HEADS):
            vb = vtbuf[j * HEAD_DIM:(j + 1) * HEAD_DIM, c0:c0 + 2 * WINDOW].astype(BF16)
            o = _dot(vb, pb[j, qb]) / dens[j][qb]
            pieces += [o[:, g * WINDOW:(g + 1) * WINDOW] for g in range(gq)]
        ot = jnp.concatenate(pieces, axis=0)
        yb = ot.T
        if qb == 0:
            rid = lax.broadcasted_iota(jnp.int32, (WINDOW, 1), 0)
            yb = jnp.where((rid >= pad_rows) | (t > 0), yb, 0.0)
        mix[c0:c0 + WINDOW, d_ssm:] = yb.astype(BF16)

    ya = _s5_mix(u, uext, yint, kst_ref, bst_ref, mst_ref, d_ref[0], wglu_ref, bglu_ref[0], advance)
    mix[:, 0:d_ssm] = ya.astype(BF16)
    sre_ref[0] = hst[0:1, 0:n_state]
    sim_ref[0] = hst[0:1, n_state:]

    kbuf[0:WINDOW, :] = kbuf[tm:tm + WINDOW, :]
    vtbuf[:, 0:WINDOW] = vtbuf[:, tm:tm + WINDOW]
    ko_ref[0] = kbuf[0:WINDOW, :]
    vo_ref[0] = v[tm - WINDOW:, :]
    o_ref[0] = x + _dot(mix[...], wout_ref[0])


def _mixer_prompt(x, head, idx, layer, wts, tm, pad_rows):
    nb, _, dm = x.shape
    tp = x.shape[1] + (0 if head is None else head.shape[0])
    nt = tp // tm
    n_state = wts["tre"].shape[-1]
    d_ssm = wts["ssm_d"].shape[-1]
    d_kv = N_KV_HEADS * HEAD_DIM
    w = n_state // wts["kst"].shape[1]
    cw = wts["kst"].shape[-1]
    layer_spec = lambda a, i: pl.BlockSpec((1,) + a.shape[1:], lambda b, t: (i,) + (0,) * (a.ndim - 1),
                                           pipeline_mode=pl.Buffered(1))
    if head is None:
        n_pieces = 0
        ins = [(x, pl.BlockSpec((1, tm, dm), lambda b, t: (b, t, 0)))]
    else:
        assert head.shape[0] == WINDOW
        n_pieces = tm // WINDOW
        piece = lambda i: pl.BlockSpec((1, WINDOW, dm), lambda b, t: (b, jnp.maximum(t * n_pieces + i - 1, 0), 0))
        ins = [(head, _const_spec(head.shape))] + [(x, piece(i)) for i in range(n_pieces)]
    ins += [
        (wts["g_mix"], layer_spec(wts["g_mix"], layer)),
        (wts["w_ukv"], layer_spec(wts["w_ukv"], idx)),
        (wts["w_qt"], layer_spec(wts["w_qt"], idx)),
        (wts["w_vt"], layer_spec(wts["w_vt"], idx)),
        (wts["kst"], layer_spec(wts["kst"], idx)),
        (wts["bst"], layer_spec(wts["bst"], idx)),
        (wts["mst"], layer_spec(wts["mst"], idx)),
        (wts["tre"], layer_spec(wts["tre"], idx)),
        (wts["tim"], layer_spec(wts["tim"], idx)),
        (wts["ssm_d"], layer_spec(wts["ssm_d"], idx)),
        (wts["w_glu"], layer_spec(wts["w_glu"], idx)),
        (wts["b_glu"], layer_spec(wts["b_glu"], idx)),
        (wts["bias_p"], _const_spec(wts["bias_p"].shape)),
        (wts["sink_p"], layer_spec(wts["sink_p"], idx)),
        (wts["w_out"], layer_spec(wts["w_out"], idx)),
    ]
    out_shape = [
        jax.ShapeDtypeStruct((nb, tp, dm), F32),
        jax.ShapeDtypeStruct((nb, 1, n_state), F32),
        jax.ShapeDtypeStruct((nb, 1, n_state), F32),
        jax.ShapeDtypeStruct((nb, WINDOW, d_kv), F32),
        jax.ShapeDtypeStruct((nb, WINDOW, d_kv), F32),
    ]
    out_specs = [
        pl.BlockSpec((1, tm, dm), lambda b, t: (b, t, 0)),
        pl.BlockSpec((1, 1, n_state), lambda b, t: (b, 0, 0)),
        pl.BlockSpec((1, 1, n_state), lambda b, t: (b, 0, 0)),
        pl.BlockSpec((1, WINDOW, d_kv), lambda b, t: (b, 0, 0)),
        pl.BlockSpec((1, WINDOW, d_kv), lambda b, t: (b, 0, 0)),
    ]
    scratch = [
        pltpu.VMEM((SUBLANES, tm, cw), F32),
        pltpu.VMEM((tm, cw), F32),
        pltpu.VMEM((wts["kst"].shape[1], tm // SUBLANES, 2 * w), F32),
        pltpu.VMEM((SUBLANES, 2 * n_state), F32),
        pltpu.VMEM((tm + WINDOW, d_kv), F32),
        pltpu.VMEM((d_kv, tm + WINDOW), F32),
        pltpu.VMEM((tm, dm), BF16),
        pltpu.VMEM((N_KV_HEADS, tm // WINDOW) + wts["bias_p"].shape[2:], F32),
        pltpu.VMEM((N_KV_HEADS, tm // WINDOW) + wts["bias_p"].shape[2:], BF16),
    ]
    assert tm // WINDOW >= wts["bias_p"].shape[0] - 1
    return pl.pallas_call(
        functools.partial(_mixer_prompt_kernel, n_pieces=n_pieces, pad_rows=pad_rows),
        grid=(nb, nt),
        in_specs=[s for _, s in ins],
        out_specs=out_specs,
        out_shape=out_shape,
        scratch_shapes=scratch,
        compiler_params=_cparams(("arbitrary", "arbitrary")),
        name=f"mixer_prompt_l{layer}",
    )(*[a for a, _ in ins])


def _mixer_sample_kernel(x_ref, h0r_ref, h0i_ref, kc_ref, vc_ref, g_ref, win_ref, kst_ref, bst_ref, mst_ref,
                         tre_ref, tim_ref, d_ref, wglu_ref, bglu_ref, bias_ref, sink_ref, wout_ref,
                         o_ref, sre_ref, sim_ref, ko_ref, vo_ref, uext, yint):
    rows = x_ref.shape[0]
    ns = kc_ref.shape[0]
    tq = rows // ns
    assert tq == SUBLANES
    w_rows = kc_ref.shape[1]
    d_ssm = d_ref.shape[-1]
    gq = bias_ref.shape[1] // tq
    d_q = N_KV_HEADS * gq * HEAD_DIM
    d_kv = N_KV_HEADS * HEAD_DIM

    x = x_ref[...]
    h = _rmsnorm(x, g_ref[0]).astype(BF16)
    z = _dot(h, win_ref[0])
    u = z[:, :d_ssm]
    q = (z[:, d_ssm:d_ssm + d_q] * (HEAD_DIM ** -0.5)).reshape(ns, tq, d_q)
    kn = z[:, d_ssm + d_q:d_ssm + d_q + d_kv].reshape(ns, tq, d_kv)
    vn = z[:, d_ssm + d_q + d_kv:].reshape(ns, tq, d_kv)

    w = h0r_ref.shape[-1] // kst_ref.shape[1]

    def advance(s_list):
        entering = []
        for blk, s in enumerate(s_list):
            lanes = slice(blk * w, (blk + 1) * w)
            hr = h0r_ref[:, lanes]
            hi = h0i_ref[:, lanes]
            ar = tre_ref[0, 0, SUBLANES - 1:SUBLANES, lanes]
            ai = tim_ref[0, 0, SUBLANES - 1:SUBLANES, lanes]
            sre_ref[:, lanes] = ar * hr - ai * hi + s[:, 0:w]
            sim_ref[:, lanes] = ar * hi + ai * hr + s[:, w:2 * w]
            entering.append(jnp.concatenate([hr, hi], axis=1))
        return entering

    ya = _s5_mix(u, uext, yint, kst_ref, bst_ref, mst_ref, d_ref[0], wglu_ref, bglu_ref[0], advance)

    kc = jnp.concatenate([kc_ref[...], kn], axis=1)
    vc = jnp.concatenate([vc_ref[...], vn], axis=1)
    ko_ref[...] = kc[:, tq:, :]
    vo_ref[...] = vc[:, tq:, :]
    pieces = []
    for j in range(N_KV_HEADS):
        kb = kc[:, :, j * HEAD_DIM:(j + 1) * HEAD_DIM].astype(BF16)
        vb = vc[:, :, j * HEAD_DIM:(j + 1) * HEAD_DIM].astype(BF16)
        qs = jnp.concatenate(
            [q[:, :, (j * gq + g) * HEAD_DIM:(j * gq + g + 1) * HEAD_DIM] for g in range(gq)],
            axis=1).astype(BF16)
        s = jnp.einsum("nqd,nkd->nqk", qs, kb, preferred_element_type=F32) + bias_ref[j]
        sink = sink_ref[0, j]
        m = jnp.maximum(jnp.max(s, axis=-1, keepdims=True), sink)
        p = jnp.exp(s - m)
        l = jnp.sum(p, axis=-1, keepdims=True) + jnp.exp(sink - m)
        o = jnp.einsum("nqk,nkd->nqd", p.astype(BF16), vb, preferred_element_type=F32) / l
        pieces += [o[:, g * tq:(g + 1) * tq, :] for g in range(gq)]
    yb = jnp.concatenate(pieces, axis=2).reshape(rows, d_q)
    mixed = jnp.concatenate([ya, yb], axis=1).astype(BF16)
    o_ref[...] = x + _dot(mixed, wout_ref[0])


def _mixer_sample(x2, h0r, h0i, kc, vc, idx, layer, wts, ns):
    rows_all, dm = x2.shape
    _, n_seq, w_rows, d_kv = kc.shape
    tq = rows_all // n_seq
    n_state = h0r.shape[-1]
    cw = wts["kst"].shape[-1]
    layer_spec = lambda a, i: pl.BlockSpec((1,) + a.shape[1:], lambda s: (i,) + (0,) * (a.ndim - 1),
                                           pipeline_mode=pl.Buffered(1))
    row_spec = pl.BlockSpec((ns * tq, dm), lambda s: (s, 0))
    st_spec = pl.BlockSpec((ns, n_state), lambda s: (s, 0))
    kv_spec = pl.BlockSpec((ns, w_rows, d_kv), lambda s: (s, 0, 0))
    st_in = pl.BlockSpec((None, ns, n_state), lambda s: (idx, s, 0))
    kv_in = pl.BlockSpec((None, ns, w_rows, d_kv), lambda s: (idx, s, 0, 0))
    ins = [
        (x2, row_spec), (h0r, st_in), (h0i, st_in), (kc, kv_in), (vc, kv_in),
        (wts["g_mix"], layer_spec(wts["g_mix"], layer)),
        (wts["w_in"], layer_spec(wts["w_in"], idx)),
        (wts["kst"], layer_spec(wts["kst"], idx)),
        (wts["bst"], layer_spec(wts["bst"], idx)),
        (wts["mst"], layer_spec(wts["mst"], idx)),
        (wts["tre"], layer_spec(wts["tre"], idx)),
        (wts["tim"], layer_spec(wts["tim"], idx)),
        (wts["ssm_d"], layer_spec(wts["ssm_d"], idx)),
        (wts["w_glu"], layer_spec(wts["w_glu"], idx)),
        (wts["b_glu"], layer_spec(wts["b_glu"], idx)),
        (wts["bias_s"], _const_spec(wts["bias_s"].shape)),
        (wts["sink_s"], layer_spec(wts["sink_s"], idx)),
        (wts["w_out"], layer_spec(wts["w_out"], idx)),
    ]
    return pl.pallas_call(
        _mixer_sample_kernel,
        grid=(n_seq // ns,),
        in_specs=[s for _, s in ins],
        out_specs=[row_spec, st_spec, st_spec, kv_spec, kv_spec],
        out_shape=[jax.ShapeDtypeStruct(x2.shape, F32), jax.ShapeDtypeStruct(h0r.shape[1:], F32),
                   jax.ShapeDtypeStruct(h0i.shape[1:], F32), jax.ShapeDtypeStruct(kc.shape[1:], F32),
                   jax.ShapeDtypeStruct(vc.shape[1:], F32)],
        scratch_shapes=[pltpu.VMEM((SUBLANES, ns * tq, cw), F32), pltpu.VMEM((ns * tq, cw), F32)],
        compiler_params=_cparams(("arbitrary",)),
        name=f"mixer_sample_l{layer}",
    )(*[a for a, _ in ins])


def _conformer_head(x, g_ref, w1_ref, c):
    h = _rmsnorm(x, g_ref[0]).astype(BF16)
    z = _dot(h, w1_ref[0])
    return z[:, :c] * jax.nn.sigmoid(z[:, c:])


def _conformer_tail(y, bdw_ref, lng_ref, lnb_ref, w2_ref):
    y = y + bdw_ref[0]
    mu = jnp.mean(y, axis=-1, keepdims=True)
    yc = y - mu
    var = jnp.mean(yc * yc, axis=-1, keepdims=True)
    y = yc * lax.rsqrt(var + EPS) * lng_ref[0] + lnb_ref[0]
    return _dot(jax.nn.silu(y).astype(BF16), w2_ref[0])


def _conformer_prompt_kernel(x_ref, g_ref, w1_ref, wdw_ref, bdw_ref, lng_ref, lnb_ref, w2_ref, o_ref, st_ref,
                             ext, ybuf, *, pad_rows, width):
    t = pl.program_id(1)
    x = x_ref[0]
    nslab, le, lanes = ext.shape
    l = le - CONV_HALO
    c = nslab * lanes
    off = CONV_HALO - (width - 1)
    mrows = CONV_PHASE_VREGS * SUBLANES
    span = CONV_PHASES * mrows
    assert l % CONF_BLOCK == 0 and CONF_BLOCK % span == 0

    @pl.when(t == 0)
    def _():
        ext[:, 0:CONV_HALO, :] = jnp.zeros((nslab, CONV_HALO, lanes), F32)

    gl = _conformer_head(x, g_ref, w1_ref, c)
    for q in range(nslab):
        ext[q, CONV_HALO:, :] = gl[:, q * lanes:(q + 1) * lanes]
    st_ref[0] = jnp.concatenate([ext[q, l + off:, :] for q in range(nslab)], axis=1)

    def block(cb, _):
        r0 = pl.multiple_of(cb * CONF_BLOCK, CONF_BLOCK)
        for grp in range(CONF_BLOCK // span):
            g0 = r0 + grp * span
            for q0 in range(0, nslab, CONV_SLABS):
                slabs = range(q0, q0 + CONV_SLABS)
                acc = {(r, q): jnp.zeros((CONV_PHASE_VREGS, SUBLANES, lanes), F32)
                       for r in range(CONV_PHASES) for q in slabs}
                for s in range(width + CONV_PHASES - 1):
                    for q in slabs:
                        xs = ext[q, pl.ds(g0 + (off + s), mrows, stride=CONV_PHASES), :]
                        xs = xs.reshape(CONV_PHASE_VREGS, SUBLANES, lanes)
                        for r in range(CONV_PHASES):
                            k = s - r
                            if 0 <= k < width:
                                wk = wdw_ref[0, k, :, q * lanes:(q + 1) * lanes]
                                acc[r, q] = acc[r, q] + wk[None] * xs
                for r in range(CONV_PHASES):
                    for q in slabs:
                        ybuf[q, pl.ds(g0 + r, mrows, stride=CONV_PHASES), :] = acc[r, q].reshape(mrows, lanes)
        y = jnp.concatenate([ybuf[q, pl.ds(r0, CONF_BLOCK), :] for q in range(nslab)], axis=1)
        out = _conformer_tail(y, bdw_ref, lng_ref, lnb_ref, w2_ref)
        rid = t * l + r0 + lax.broadcasted_iota(jnp.int32, (CONF_BLOCK, 1), 0)
        o_ref[0, pl.ds(r0, CONF_BLOCK), :] = x_ref[0, pl.ds(r0, CONF_BLOCK), :] + jnp.where(rid >= pad_rows, out, 0.0)
        return 0

    lax.fori_loop(0, l // CONF_BLOCK, block, 0)
    ext[:, 0:CONV_HALO, :] = ext[:, l:l + CONV_HALO, :]


def _conformer_sample_kernel(x_ref, past_ref, g_ref, w1_ref, wdw_ref, bdw_ref, lng_ref, lnb_ref, w2_ref,
                             o_ref, st_ref, ext, ybuf, stage, *, width):
    x = x_ref[...]
    ns, le, c = ext.shape
    l = le - CONV_HALO
    off = CONV_HALO - (width - 1)
    ext[:, 0:off, :] = jnp.zeros((ns, off, c), F32)
    ext[:, off:CONV_HALO, :] = past_ref[...]
    ext[:, CONV_HALO:, :] = _conformer_head(x, g_ref, w1_ref, c).reshape(ns, l, c)
    st_ref[...] = ext[:, l + off:, :]

    def conv_unit(n, stg):
        win = ext[n]
        for s in range(1, SUBLANES):
            stg[s - 1] = win[s:s + l + CONV_HALO - SUBLANES]
        acc = jnp.zeros((l // SUBLANES, SUBLANES, c), F32)
        for k in range(width):
            a, s = divmod(off + k, SUBLANES)
            if s == 0:
                xk = ext[n, a * SUBLANES:a * SUBLANES + l, :]
            else:
                xk = stg[s - 1, a * SUBLANES:a * SUBLANES + l, :]
            acc = acc + wdw_ref[0, k][None] * xk.reshape(l // SUBLANES, SUBLANES, c)
        return acc.reshape(l, c)

    def seq(i, _):
        for half in range(2):
            n = 2 * i + half
            ybuf[pl.ds(pl.multiple_of(n * l, l), l), :] = conv_unit(n, stage.at[half])
        return 0
    assert ns % 2 == 0 and l % SUBLANES == 0
    lax.fori_loop(0, ns // 2, seq, 0)
    o_ref[...] = x + _conformer_tail(ybuf[...], bdw_ref, lng_ref, lnb_ref, w2_ref)


def _conformer(x, past, idx, layer, wts, tile, pad_rows):
    carry = past is None
    width = wts["w_dw"].shape[1]
    c = wts["w_dw"].shape[3]
    if carry:
        nb, tp, dm = x.shape
        grid = (nb, tp // tile)
        lspec = lambda a, i: pl.BlockSpec((1,) + a.shape[1:], lambda b, t: (i,) + (0,) * (a.ndim - 1),
                                          pipeline_mode=pl.Buffered(1))
        xspec = pl.BlockSpec((1, tile, dm), lambda b, t: (b, t, 0))
        ins = [(x, xspec)]
        st_shape = (nb, width - 1, c)
        st_spec = pl.BlockSpec((1, width - 1, c), lambda b, t: (b, 0, 0))
        ns, l = 1, tile
        sem = ("arbitrary", "arbitrary")
    else:
        rows_all, dm = x.shape
        n_seq = past.shape[1]
        l = rows_all // n_seq
        ns = tile
        grid = (n_seq // ns,)
        lspec = lambda a, i: pl.BlockSpec((1,) + a.shape[1:], lambda s: (i,) + (0,) * (a.ndim - 1),
                                          pipeline_mode=pl.Buffered(1))
        xspec = pl.BlockSpec((ns * l, dm), lambda s: (s, 0))
        ins = [(x, xspec), (past, pl.BlockSpec((None, ns, width - 1, c), lambda s: (idx, s, 0, 0)))]
        st_shape = (n_seq, width - 1, c)
        st_spec = pl.BlockSpec((ns, width - 1, c), lambda s: (s, 0, 0))
        sem = ("arbitrary",)
    ins += [
        (wts["g_mix"], lspec(wts["g_mix"], layer)),
        (wts["w_pw1"], lspec(wts["w_pw1"], idx)),
        (wts["w_dw"], lspec(wts["w_dw"], idx)),
        (wts["b_dw"], lspec(wts["b_dw"], idx)),
        (wts["ln_g"], lspec(wts["ln_g"], idx)),
        (wts["ln_b"], lspec(wts["ln_b"], idx)),
        (wts["w_pw2"], lspec(wts["w_pw2"], idx)),
    ]
    if carry:
        body = functools.partial(_conformer_prompt_kernel, pad_rows=pad_rows, width=width)
        scratch = [pltpu.VMEM((c // LANES, l + CONV_HALO, LANES), F32), pltpu.VMEM((c // LANES, l, LANES), F32)]
    else:
        body = functools.partial(_conformer_sample_kernel, width=width)
        scratch = [pltpu.VMEM((ns, l + CONV_HALO, c), F32), pltpu.VMEM((ns * l, c), F32),
                   pltpu.VMEM((2, SUBLANES - 1, l + CONV_HALO - SUBLANES, c), F32)]
    return pl.pallas_call(
        body,
        grid=grid,
        in_specs=[s for _, s in ins],
        out_specs=[xspec, st_spec],
        out_shape=[jax.ShapeDtypeStruct(x.shape, F32), jax.ShapeDtypeStruct(st_shape, F32)],
        scratch_shapes=scratch,
        compiler_params=_cparams(sem),
        name=f"conformer_{'prompt' if carry else 'sample'}_l{layer}",
    )(*[a for a, _ in ins])


def _ffn_kernel(*refs, carry, final_norm, n_pieces):
    if n_pieces:
        halo_ref, piece_refs, refs = refs[0], refs[1:1 + n_pieces], refs[1 + n_pieces:]
        x = jnp.concatenate([r[0] for r in piece_refs], axis=0)
    elif carry:
        x_ref, refs = refs[0], refs[1:]
        x = x_ref[0]
    else:
        x_ref, past_ref, refs = refs[0], refs[1], refs[2:]
        x = x_ref[...]
    (g_ref, wup_ref, wcv_ref, bcv_ref, wdn_ref, gf_ref, o_ref, st_ref, ext) = refs
    ns, le, dff = ext.shape
    l = le - FFN_HALO
    rows = ns * l
    kw = wcv_ref.shape[1]

    if carry:
        @pl.when(pl.program_id(1) == 0)
        def _():
            if n_pieces:
                hh = _rmsnorm(halo_ref[0, WINDOW - FFN_HALO:, :], g_ref[0]).astype(BF16)
                ext[:, 0:FFN_HALO, :] = _dot(hh, wup_ref[0, :, 0:dff]).reshape(ns, FFN_HALO, dff)
            else:
                ext[:, 0:FFN_HALO, :] = jnp.zeros((ns, FFN_HALO, dff), F32)
    else:
        ext[:, FFN_HALO - (kw - 1):FFN_HALO, :] = past_ref[...]

    h = _rmsnorm(x, g_ref[0]).astype(BF16)
    gate = _dot(h, wup_ref[0, :, 0:dff])
    up = _dot(h, wup_ref[0, :, dff:])
    ext[:, FFN_HALO:, :] = gate.reshape(ns, l, dff)
    gc = bcv_ref[0]
    for k in range(kw):
        o = FFN_HALO - (kw - 1) + k
        gc = gc + wcv_ref[0, k:k + 1, :] * ext[:, o:o + l, :]
    y = (jax.nn.gelu(gc).reshape(rows, dff) * up).astype(BF16)
    acc = x + _dot(y, wdn_ref[0])
    st_ref[...] = ext[:, le - (kw - 1):, :].reshape(st_ref.shape)
    if final_norm:
        acc = _rmsnorm(acc, gf_ref[...])
    if carry:
        o_ref[0] = acc
        ext[:, 0:FFN_HALO, :] = ext[:, l:l + FFN_HALO, :]
    else:
        o_ref[...] = acc


def _ffn(x, past, layer, wts, tile, final_norm, skip_rows=0):
    carry = past is None
    dff = wts["w_dn"].shape[1]
    kw = wts["w_cv"].shape[1]
    n_pieces = 0
    if carry:
        nb, tp, dm = x.shape
        out_rows = tp - skip_rows
        grid = (nb, out_rows // tile)
        lspec = lambda a, i: pl.BlockSpec((1,) + a.shape[1:], lambda b, t: (i,) + (0,) * (a.ndim - 1),
                                          pipeline_mode=pl.Buffered(1))
        xspec = pl.BlockSpec((1, tile, dm), lambda b, t: (b, t, 0))
        if skip_rows:
            assert skip_rows % WINDOW == 0 and tile % WINDOW == 0 and out_rows % tile == 0
            n_pieces, skip = tile // WINDOW, skip_rows // WINDOW
            piece = lambda i: pl.BlockSpec((1, WINDOW, dm), lambda b, t: (b, skip + t * n_pieces + i, 0))
            ins = [(x, piece(-1))] + [(x, piece(i)) for i in range(n_pieces)]
        else:
            ins = [(x, xspec)]
        out_struct = jax.ShapeDtypeStruct((nb, out_rows, dm), F32)
        st_shape = (nb, kw - 1, dff)
        st_spec = pl.BlockSpec((1, kw - 1, dff), lambda b, t: (b, 0, 0))
        ns, l = 1, tile
        sem = ("arbitrary", "arbitrary")
    else:
        out_struct = jax.ShapeDtypeStruct(x.shape, F32)
        rows_all, dm = x.shape
        n_seq = past.shape[1]
        l = rows_all // n_seq
        ns = tile
        grid = (n_seq // ns,)
        lspec = lambda a, i: pl.BlockSpec((1,) + a.shape[1:], lambda s: (i,) + (0,) * (a.ndim - 1),
                                          pipeline_mode=pl.Buffered(1))
        xspec = pl.BlockSpec((ns * l, dm), lambda s: (s, 0))
        ins = [(x, xspec), (past, pl.BlockSpec((None, ns, kw - 1, dff), lambda s: (layer, s, 0, 0)))]
        st_shape = (n_seq, kw - 1, dff)
        st_spec = pl.BlockSpec((ns, kw - 1, dff), lambda s: (s, 0, 0))
        sem = ("arbitrary",)
    ins += [
        (wts["g_ffn"], lspec(wts["g_ffn"], layer)),
        (wts["w_up"], lspec(wts["w_up"], layer)),
        (wts["w_cv"], lspec(wts["w_cv"], layer)),
        (wts["b_cv"], lspec(wts["b_cv"], layer)),
        (wts["w_dn"], lspec(wts["w_dn"], layer)),
        (wts["g_final"], _const_spec(wts["g_final"].shape)),
    ]
    return pl.pallas_call(
        functools.partial(_ffn_kernel, carry=carry, final_norm=final_norm, n_pieces=n_pieces),
        grid=grid,
        in_specs=[s for _, s in ins],
        out_specs=[xspec, st_spec],
        out_shape=[out_struct, jax.ShapeDtypeStruct(st_shape, F32)],
        scratch_shapes=[pltpu.VMEM((ns, l + FFN_HALO, dff), F32)],
        compiler_params=_cparams(sem),
        name=f"ffn_{'prompt' if carry else 'sample'}_l{layer}",
    )(*[a for a, _ in ins])


def _pick_tile(total, target):
    best = WINDOW
    for m in range(1, total // WINDOW + 1):
        if total % (m * WINDOW) == 0 and m * WINDOW <= target:
            best = m * WINDOW
    return best


def kernel(x_prompt, x_sample, state_ssm_re, state_ssm_im, cache_swa_k, cache_swa_v, state_conv, state_ffn,
           meta_tokens, g_mix, g_ffn, g_final, w_in_mix, ssm_lambda_re, ssm_lambda_im, ssm_log_step,
           ssm_b_re, ssm_b_im, ssm_c_re, ssm_c_im, ssm_d, ssm_w_glu, ssm_b_glu, rel_bias, attn_sinks,
           w_out_mix, conv_w_pw1, conv_w_dw, conv_b_dw, conv_ln_g, conv_ln_b, conv_w_pw2,
           ffn_w_up, ffn_w_conv, ffn_b_conv, ffn_w_down):
    nb, seq, dm = x_prompt.shape
    n_seq, tq, _ = x_sample.shape
    depth = g_mix.shape[0]
    n_meta = meta_tokens.shape[0]
    n_even, g_ssm, p_ssm = ssm_lambda_re.shape
    n_heads = rel_bias.shape[1]
    gq = n_heads // N_KV_HEADS
    w_rows = cache_swa_k.shape[2]
    d_kv = N_KV_HEADS * HEAD_DIM
    conv_w = conv_w_dw.shape[1]
    ffn_w = ffn_w_conv.shape[1]
    assert tq == SUBLANES and w_rows == WINDOW and seq % WINDOW == 0 and n_meta <= WINDOW
    assert conv_w - 1 <= CONV_HALO and ffn_w - 1 <= FFN_HALO

    tre, tim, kst, bst, mst = _ssm_prep(ssm_lambda_re, ssm_lambda_im, ssm_log_step,
                                        ssm_b_re, ssm_b_im, ssm_c_re, ssm_c_im)
    bias = _bias_table(rel_bias)
    row3 = lambda a: a.reshape(a.shape[0], 1, a.shape[-1]).astype(F32)
    sinks = attn_sinks.astype(F32).reshape(n_even, N_KV_HEADS, gq, 1)
    pad_rows = WINDOW - n_meta
    bias_t = jnp.transpose(bias.reshape(N_KV_HEADS, gq, WINDOW, 2 * WINDOW), (0, 3, 1, 2)
                           ).reshape(N_KV_HEADS, 2 * WINDOW, gq * WINDOW)
    key_i = jnp.arange(2 * WINDOW)[None, :, None]
    n_var = 1 + -(-(pad_rows + WINDOW) // WINDOW)
    bias_p = jnp.stack([bias_t] + [jnp.where(key_i >= pad_rows + WINDOW - qb * WINDOW, bias_t, NEG_INF)
                                   for qb in range(n_var - 1)])
    d_ssm = ssm_d.shape[-1]
    d_q = n_heads * HEAD_DIM
    w_in_bf = w_in_mix.astype(BF16)
    wts = {
        "g_mix": row3(g_mix), "g_ffn": row3(g_ffn), "g_final": g_final.reshape(1, dm).astype(F32),
        "w_in": w_in_bf, "kst": kst, "bst": bst, "mst": mst, "tre": tre, "tim": tim,
        "w_ukv": jnp.concatenate([w_in_bf[:, :, :d_ssm], w_in_bf[:, :, d_ssm + d_q:]], axis=-1),
        "w_qt": jnp.transpose(w_in_bf[:, :, d_ssm:d_ssm + d_q], (0, 2, 1)) * (HEAD_DIM ** -0.5),
        "w_vt": jnp.transpose(w_in_bf[:, :, d_ssm + d_q + d_kv:], (0, 2, 1)),
        "ssm_d": row3(ssm_d), "w_glu": ssm_w_glu.astype(BF16), "b_glu": row3(ssm_b_glu),
        "bias_p": bias_p,
        "bias_s": bias[:, :tq, :w_rows + tq].reshape(N_KV_HEADS, gq * tq, w_rows + tq),
        "sink_p": jnp.broadcast_to(sinks[:, :, :, None, :], (n_even, N_KV_HEADS, gq, WINDOW, 1)
                                   ).reshape(n_even, N_KV_HEADS, 1, gq * WINDOW),
        "sink_s": jnp.broadcast_to(sinks[:, :, :, None, :], (n_even, N_KV_HEADS, gq, tq, 1)
                                   ).reshape(n_even, N_KV_HEADS, gq * tq, 1),
        "w_out": w_out_mix.astype(BF16),
        "w_pw1": conv_w_pw1.astype(BF16), "w_dw": jnp.broadcast_to(conv_w_dw.astype(F32)[:, :, None, :],
                                 conv_w_dw.shape[:2] + (SUBLANES, conv_w_dw.shape[2])), "b_dw": row3(conv_b_dw),
        "ln_g": row3(conv_ln_g), "ln_b": row3(conv_ln_b), "w_pw2": conv_w_pw2.astype(BF16),
        "w_up": ffn_w_up.astype(BF16), "w_cv": ffn_w_conv.astype(F32), "b_cv": row3(ffn_b_conv),
        "w_dn": ffn_w_down.astype(BF16),
    }

    tp = pad_rows + n_meta + seq
    tm = _pick_tile(tp, PROMPT_TILE_TARGET)
    xp = x_prompt.astype(F32)
    head = jnp.concatenate([jnp.zeros((pad_rows, dm), F32), meta_tokens.astype(F32)], axis=0)
    xs = x_sample.astype(F32).reshape(n_seq * tq, dm)
    ns = min(SAMPLE_SEQS, n_seq)
    assert n_seq % ns == 0

    h0r_all = state_ssm_re.astype(F32).reshape(n_even, n_seq, g_ssm * p_ssm)
    h0i_all = state_ssm_im.astype(F32).reshape(n_even, n_seq, g_ssm * p_ssm)
    kc_all = cache_swa_k.astype(F32).reshape(n_even, n_seq, w_rows, d_kv)
    vc_all = cache_swa_v.astype(F32).reshape(n_even, n_seq, w_rows, d_kv)
    conv_all = state_conv.astype(F32)
    ffn_all = state_ffn.astype(F32)

    sr_p, si_p, k_p, v_p, c_p, f_p = [], [], [], [], [], []
    sr_s, si_s, k_s, v_s, c_s, f_s = [], [], [], [], [], []
    for layer in range(depth):
        idx = layer // 2
        if layer % 2 == 0:
            xp, sre, sim, ko, vo = _mixer_prompt(xp, head if layer == 0 else None, idx, layer, wts, tm, pad_rows)
            sr_p.append(sre.reshape(nb, g_ssm, p_ssm))
            si_p.append(sim.reshape(nb, g_ssm, p_ssm))
            k_p.append(ko.reshape(nb, WINDOW, N_KV_HEADS, HEAD_DIM))
            v_p.append(vo.reshape(nb, WINDOW, N_KV_HEADS, HEAD_DIM))
            xs, sre, sim, ko, vo = _mixer_sample(xs, h0r_all, h0i_all, kc_all, vc_all, idx, layer, wts, ns)
            sr_s.append(sre.reshape(n_seq, g_ssm, p_ssm))
            si_s.append(sim.reshape(n_seq, g_ssm, p_ssm))
            k_s.append(ko.reshape(n_seq, w_rows, N_KV_HEADS, HEAD_DIM))
            v_s.append(vo.reshape(n_seq, w_rows, N_KV_HEADS, HEAD_DIM))
        else:
            xp, st = _conformer(xp, None, idx, layer, wts, tm, pad_rows)
            c_p.append(st)
            xs, st = _conformer(xs, conv_all, idx, layer, wts, ns, 0)
            c_s.append(st)
        last = layer == depth - 1
        if last:
            xp, st = _ffn(xp, None, layer, wts, _pick_tile(seq, PROMPT_TILE_TARGET), True,
                          skip_rows=pad_rows + n_meta)
        else:
            xp, st = _ffn(xp, None, layer, wts, tm, False)
        f_p.append(st)
        xs, st = _ffn(xs, ffn_all, layer, wts, ns, last)
        f_s.append(st)

    yp = xp
    ys = xs.reshape(n_seq, tq, dm)
    st = jnp.stack
    return (yp, ys, st(sr_p), st(si_p), st(k_p), st(v_p), st(c_p), st(f_p),
            st(sr_s), st(si_s), st(k_s), st(v_s), st(c_s), st(f_s))
```

```python
import functools
import math

import numpy as np
import jax
import jax.numpy as jnp
from jax import lax
from jax.experimental import pallas as pl
from jax.experimental.pallas import tpu as pltpu

F32 = jnp.float32
BF16 = jnp.bfloat16

EPS = 1e-6
NEG_INF = -1e30
WINDOW = 128
HEAD_DIM = 64
N_KV_HEADS = 2
N_BUCKETS = 32
MAX_DISTANCE = 128
SSM_GROUP = 16
SSM_STATE = 64
SSM_BLK_GROUPS = 8
SSM_SUB_GROUPS = 4
SUBLANES = 8
CONV_HALO = 32
FFN_HALO = 8
LANES = 128
CONV_PHASES = 4
CONV_PHASE_VREGS = 4
CONV_SLABS = 2
CONF_BLOCK = 128
PROMPT_TILE_TARGET = 768
SAMPLE_SEQS = 32
VMEM_LIMIT = 56 * 1024 * 1024


def _cparams(sem):
    return pltpu.CompilerParams(dimension_semantics=sem, vmem_limit_bytes=VMEM_LIMIT)


def _const_spec(shape):
    nd = len(shape)
    return pl.BlockSpec(shape, lambda *_: (0,) * nd, pipeline_mode=pl.Buffered(1))


def _rmsnorm(x, g):
    return x * lax.rsqrt(jnp.mean(x * x, axis=-1, keepdims=True) + EPS) * g


def _dot(a, b):
    return jnp.dot(a, b, preferred_element_type=F32)


def _ssm_prep_kernel(lr_ref, li_ref, ls_ref, bre_ref, bim_ref, cre_ref, cim_ref,
                     tre_ref, tim_ref, kst_ref, bst_ref, mst_ref):
    lr = lr_ref[0]
    li = li_ref[0]
    dt = jnp.exp(ls_ref[0])
    decay = jnp.exp(lr * dt)
    a_re = decay * jnp.cos(li * dt)
    a_im = decay * jnp.sin(li * dt)
    den = lr * lr + li * li
    num_re = a_re - 1.0
    coef_re = (num_re * lr + a_im * li) / den
    coef_im = (a_im * lr - num_re * li) / den

    cmul = lambda x, y: (x[0] * y[0] - x[1] * y[1], x[0] * y[1] + x[1] * y[0])
    n = lr.shape[-1]
    pw = [(jnp.ones((1, n), F32), jnp.zeros((1, n), F32)), (a_re, a_im)]
    for _ in range(SUBLANES - 1):
        pw.append(cmul(pw[-1], (a_re, a_im)))
    apw = [pw[SUBLANES]]
    for _ in range(SUBLANES - 1):
        apw.append(cmul(apw[-1], pw[SUBLANES]))
    row = lax.broadcasted_iota(jnp.int32, (SUBLANES, n), 0)
    zero = jnp.zeros((SUBLANES, n), F32)
    for t, k in enumerate((1, 2, 4)):
        tre_ref[0, t] = jnp.where(row >= k, jnp.broadcast_to(apw[k - 1][0], (SUBLANES, n)), zero)
        tim_ref[0, t] = jnp.where(row >= k, jnp.broadcast_to(apw[k - 1][1], (SUBLANES, n)), zero)
    ap_re, ap_im = zero, zero
    for j in range(SUBLANES):
        ap_re = jnp.where(row == j, jnp.broadcast_to(apw[j][0], (SUBLANES, n)), ap_re)
        ap_im = jnp.where(row == j, jnp.broadcast_to(apw[j][1], (SUBLANES, n)), ap_im)
    tre_ref[0, 3] = ap_re
    tim_ref[0, 3] = ap_im

    bre = bre_ref[0]
    bim = bim_ref[0]
    bbar = (coef_re * bre - coef_im * bim, coef_re * bim + coef_im * bre)
    rows = SSM_BLK_GROUPS * SSM_GROUP
    r_i = lax.broadcasted_iota(jnp.int32, (rows, n), 0)
    c_i = lax.broadcasted_iota(jnp.int32, (rows, n), 1)
    sh = lambda v, d: lax.shift_right_logical(v, jnp.full(v.shape, int(math.log2(d)), jnp.int32))
    diag = sh(r_i, SSM_GROUP) == (sh(c_i, SSM_STATE) & (SSM_BLK_GROUPS - 1))
    zer = jnp.zeros((rows, n), F32)
    blockdiag = lambda v: jnp.where(diag, jnp.concatenate([v] * SSM_BLK_GROUPS, axis=0), zer)
    bb = (blockdiag(bbar[0]), blockdiag(bbar[1]))
    cc = (blockdiag(cre_ref[0]), blockdiag(cim_ref[0]))
    w = SSM_BLK_GROUPS * SSM_STATE
    nblk = n // w
    for blk in range(nblk):
        sl = slice(blk * w, (blk + 1) * w)
        bbs = (bb[0][:, sl], bb[1][:, sl])
        ccs = (cc[0][:, sl], cc[1][:, sl])
        bcats = []
        nsub = SSM_BLK_GROUPS // SSM_SUB_GROUPS
        lw, sw = SSM_SUB_GROUPS * SSM_GROUP, SSM_SUB_GROUPS * SSM_STATE
        bsub = [[] for _ in range(nsub)]
        msub = [[] for _ in range(nsub)]
        for d in range(SUBLANES):
            ba = cmul(bbs, (pw[d][0][:, sl], pw[d][1][:, sl]))
            ca = cmul(ccs, (pw[d + 1][0][:, sl], pw[d + 1][1][:, sl]))
            bcats.append(jnp.concatenate([ba[0], ba[1]], axis=1))
            for sb in range(nsub):
                rr, cs = slice(sb * lw, (sb + 1) * lw), slice(sb * sw, (sb + 1) * sw)
                bsub[sb].append(jnp.concatenate([ba[0][rr, cs], ba[1][rr, cs]], axis=1))
                msub[sb].append(jnp.concatenate([ca[0][rr, cs], -ca[1][rr, cs]], axis=1))
        for sb in range(nsub):
            bst_ref[0, blk, sb] = jnp.concatenate(bsub[sb], axis=0).astype(BF16)
            mst_ref[0, blk, sb] = jnp.concatenate(msub[sb], axis=0).T.astype(BF16)
        ccat = jnp.concatenate([ccs[0], -ccs[1]], axis=1)
        split = lambda v: (v.astype(BF16), (v - v.astype(BF16).astype(F32)).astype(BF16))
        (bh, bl), (ch, cl) = split(jnp.concatenate(bcats, axis=0)), split(ccat)
        nt = lambda p, q: lax.dot_general(p, q, (((1,), (1,)), ((), ())), preferred_element_type=F32)
        kst_ref[0, blk] = (nt(bh, ch) + nt(bh, cl) + nt(bl, ch)).astype(BF16)


def _ssm_prep(lam_re, lam_im, log_step, b_re, b_im, c_re, c_im):
    ne, g, p = lam_re.shape
    n = g * p
    assert g % SSM_BLK_GROUPS == 0 and p == SSM_STATE and b_re.shape[-1] == SSM_GROUP
    nblk = g // SSM_BLK_GROUPS
    rows = SSM_BLK_GROUPS * SSM_GROUP
    w = SSM_BLK_GROUPS * SSM_STATE
    flat = lambda a: a.reshape(ne, 1, n)
    ls = jnp.broadcast_to(log_step[:, :, None], (ne, g, p))
    bt = lambda a: jnp.transpose(a, (0, 3, 1, 2)).reshape(ne, SSM_GROUP, n)
    ct = lambda a: jnp.transpose(a, (0, 2, 1, 3)).reshape(ne, SSM_GROUP, n)
    vec = pl.BlockSpec((1, 1, n), lambda i: (i, 0, 0))
    mat = pl.BlockSpec((1, SSM_GROUP, n), lambda i: (i, 0, 0))
    tab = pl.BlockSpec((1, 4, SUBLANES, n), lambda i: (i, 0, 0, 0))
    assert SSM_BLK_GROUPS % SSM_SUB_GROUPS == 0
    nsub = SSM_BLK_GROUPS // SSM_SUB_GROUPS
    sub_in = SUBLANES * SSM_SUB_GROUPS * SSM_GROUP
    sub_st = 2 * SSM_SUB_GROUPS * SSM_STATE
    kspec = pl.BlockSpec((1, nblk, SUBLANES * rows, rows), lambda i: (i, 0, 0, 0))
    bspec = pl.BlockSpec((1, nblk, nsub, sub_in, sub_st), lambda i: (i, 0, 0, 0, 0))
    mspec = pl.BlockSpec((1, nblk, nsub, sub_st, sub_in), lambda i: (i, 0, 0, 0, 0))
    return pl.pallas_call(
        _ssm_prep_kernel,
        grid=(ne,),
        in_specs=[vec, vec, vec, mat, mat, mat, mat],
        out_specs=[tab, tab, kspec, bspec, mspec],
        out_shape=[jax.ShapeDtypeStruct((ne, 4, SUBLANES, n), F32)] * 2
        + [jax.ShapeDtypeStruct((ne, nblk, SUBLANES * rows, rows), BF16),
           jax.ShapeDtypeStruct((ne, nblk, nsub, sub_in, sub_st), BF16),
           jax.ShapeDtypeStruct((ne, nblk, nsub, sub_st, sub_in), BF16)],
        compiler_params=pltpu.CompilerParams(vmem_limit_bytes=VMEM_LIMIT),
        name="ssm_prep",
    )(flat(lam_re), flat(lam_im), flat(ls), bt(b_re), bt(b_im), ct(c_re), ct(c_im))


def _bucket_table():
    i = np.arange(WINDOW)[:, None]
    j = np.arange(2 * WINDOW)[None, :]
    dist = WINDOW + i - j
    nn = np.maximum(dist, 0)
    max_exact = N_BUCKETS // 2
    nf = np.maximum(nn, max_exact).astype(np.float32)
    large = max_exact + (np.log(nf / np.float32(max_exact)) / np.float32(math.log(MAX_DISTANCE / max_exact))
                         * np.float32(N_BUCKETS - max_exact)).astype(np.int32)
    large = np.minimum(large, N_BUCKETS - 1)
    bucket = np.where(nn < max_exact, nn, large)
    valid = (dist >= 0) & (dist < WINDOW)
    return np.where(valid, bucket, -1).astype(np.int32)


def _bias_kernel(rb_ref, bucket_ref, out_ref):
    bucket = bucket_ref[...]
    nh = out_ref.shape[0]
    for h in range(nh):
        acc = jnp.full(bucket.shape, NEG_INF, F32)
        for b in range(N_BUCKETS):
            acc = jnp.where(bucket == b, rb_ref[b, h], acc)
        out_ref[h] = acc


def _bias_table(rel_bias):
    nh = rel_bias.shape[1]
    bucket = jnp.asarray(_bucket_table())
    return pl.pallas_call(
        _bias_kernel,
        in_specs=[pl.BlockSpec(memory_space=pltpu.SMEM),
                  pl.BlockSpec(bucket.shape, lambda: (0, 0))],
        out_specs=pl.BlockSpec((nh,) + bucket.shape, lambda: (0, 0, 0)),
        out_shape=jax.ShapeDtypeStruct((nh,) + bucket.shape, F32),
        name="rel_bias_table",
    )(rel_bias.astype(F32), bucket)


def _block_state_scan(sbuf, hst, tre_ref, tim_ref):
    nblk, nb, w2 = sbuf.shape
    w = w2 // 2
    n_state = hst.shape[1] // 2
    first = lax.broadcasted_iota(jnp.int32, (SUBLANES, w), 0) == 0

    def body(b, _):
        r0 = pl.multiple_of(b * SUBLANES, SUBLANES)
        for blk in range(nblk):
            lanes = pl.ds(blk * w, w)
            hr = hst[:, blk * w:(blk + 1) * w]
            hi = hst[:, n_state + blk * w:n_state + (blk + 1) * w]
            xr = sbuf[blk, pl.ds(r0, SUBLANES), 0:w]
            xi = sbuf[blk, pl.ds(r0, SUBLANES), w:2 * w]
            for t, k in enumerate((1, 2, 4)):
                cr = tre_ref[0, t, :, lanes]
                ci = tim_ref[0, t, :, lanes]
                rr = pltpu.roll(xr, k, 0)
                ri = pltpu.roll(xi, k, 0)
                xr, xi = xr + cr * rr - ci * ri, xi + cr * ri + ci * rr
            pr = tre_ref[0, 3, :, lanes]
            pi = tim_ref[0, 3, :, lanes]
            xr, xi = xr + pr * hr - pi * hi, xi + pr * hi + pi * hr
            sbuf[blk, pl.ds(r0, SUBLANES), 0:w] = jnp.where(first, hr, pltpu.roll(xr, 1, 0))
            sbuf[blk, pl.ds(r0, SUBLANES), w:2 * w] = jnp.where(first, hi, pltpu.roll(xi, 1, 0))
            hst[:, blk * w:(blk + 1) * w] = jnp.broadcast_to(xr[SUBLANES - 1:SUBLANES, :], (SUBLANES, w))
            hst[:, n_state + blk * w:n_state + (blk + 1) * w] = jnp.broadcast_to(
                xi[SUBLANES - 1:SUBLANES, :], (SUBLANES, w))
        return 0

    lax.fori_loop(0, nb // SUBLANES, body, 0)


def _s5_mix(u, uext, yint, kst_ref, bst_ref, mst_ref, d, wglu_ref, bglu, advance):
    nblk = kst_ref.shape[1]
    cw = kst_ref.shape[3]
    nsub = bst_ref.shape[2]
    lw, sw = cw // nsub, bst_ref.shape[4] // 2
    rows = u.shape[0]
    nb = rows // SUBLANES
    j_i = lax.broadcasted_iota(jnp.int32, (1, SUBLANES, 1), 1)
    y_intra, s_all = [], []
    for blk in range(nblk):
        ub = u[:, blk * cw:(blk + 1) * cw]
        u3 = ub.reshape(nb, SUBLANES, cw)
        parts = [ub]
        for dd in range(1, SUBLANES):
            parts.append(jnp.where(j_i >= dd, pltpu.roll(u3, dd, 1), 0.0).reshape(rows, cw))
        for dd in range(SUBLANES):
            uext[dd] = parts[dd]
        y_intra.append(_dot(jnp.concatenate(parts, axis=1).astype(BF16), kst_ref[0, blk]))
        last = [uext[dd, pl.ds(SUBLANES - 1, nb, stride=SUBLANES), :] for dd in range(SUBLANES)]
        s_sub = []
        for sb in range(nsub):
            ul = jnp.concatenate([v[:, sb * lw:(sb + 1) * lw] for v in last], axis=1)
            s_sub.append(_dot(ul.astype(BF16), bst_ref[0, blk, sb]))
        s_all.append(jnp.concatenate([v[:, :sw] for v in s_sub] + [v[:, sw:] for v in s_sub], axis=1))
    h_all = advance(s_all)
    w = nsub * sw
    ys = []
    for blk in range(nblk):
        hprev = h_all[blk]
        yi = [_dot(jnp.concatenate([hprev[:, sb * sw:(sb + 1) * sw], hprev[:, w + sb * sw:w + (sb + 1) * sw]],
                                   axis=1).astype(BF16), mst_ref[0, blk, sb]) for sb in range(nsub)]
        for j in range(SUBLANES):
            yint[pl.ds(j, nb, stride=SUBLANES), :] = jnp.concatenate(
                [v[:, j * lw:(j + 1) * lw] for v in yi], axis=1)
        ys.append(y_intra[blk] + yint[...])
    y = jnp.concatenate(ys, axis=1) + d * u
    g = jax.nn.gelu(y)
    return g * jax.nn.sigmoid(_dot(g.astype(BF16), wglu_ref[0]) + bglu)


def _mixer_prompt_kernel(*refs, n_pieces, pad_rows):
    if n_pieces:
        head_ref, piece_refs, refs = refs[0], refs[1:1 + n_pieces], refs[1 + n_pieces:]
    else:
        x_ref, refs = refs[0], refs[1:]
    (g_ref, wukv_ref, wqt_ref, wvt_ref, kst_ref, bst_ref, mst_ref, tre_ref, tim_ref,
     d_ref, wglu_ref, bglu_ref, bias_ref, sink_ref, wout_ref,
     o_ref, sre_ref, sim_ref, ko_ref, vo_ref,
     uext, yint, sbuf, hst, kbuf, vtbuf, mix, sc, pb) = refs
    t = pl.program_id(1)
    tm = o_ref.shape[1]
    d_ssm = d_ref.shape[-1]
    d_kv = N_KV_HEADS * HEAD_DIM
    gq = bias_ref.shape[3] // WINDOW
    nt_dims = (((1,), (1,)), ((), ()))

    @pl.when(t == 0)
    def _():
        hst[...] = jnp.zeros_like(hst)
        kbuf[0:WINDOW, :] = jnp.zeros((WINDOW, d_kv), F32)
        vtbuf[:, 0:WINDOW] = jnp.zeros((d_kv, WINDOW), F32)

    if n_pieces:
        first = jnp.where(t == 0, head_ref[...], piece_refs[0][0])
        x = jnp.concatenate([first] + [r[0] for r in piece_refs[1:]], axis=0)
    else:
        x = x_ref[0]
    h = _rmsnorm(x, g_ref[0]).astype(BF16)
    z = _dot(h, wukv_ref[0])
    u = z[:, :d_ssm]
    kbuf[WINDOW:, :] = z[:, d_ssm:d_ssm + d_kv]
    v = z[:, d_ssm + d_kv:]
    qt = lax.dot_general(wqt_ref[0], h, nt_dims, preferred_element_type=F32)
    vtbuf[:, WINDOW:] = lax.dot_general(wvt_ref[0], h, nt_dims, preferred_element_type=F32)

    n_state = hst.shape[1] // 2

    def advance(s_list):
        for blk, s in enumerate(s_list):
            sbuf[blk] = s
        _block_state_scan(sbuf, hst, tre_ref, tim_ref)
        return [sbuf[blk] for blk in range(len(s_list))]

    nqb = tm // WINDOW
    for qb in range(nqb):
        c0 = qb * WINDOW
        var = jnp.where(t == 0, qb + 1, 0) if qb + 1 < bias_ref.shape[0] else 0
        for j in range(N_KV_HEADS):
            kb = kbuf[c0:c0 + 2 * WINDOW, j * HEAD_DIM:(j + 1) * HEAD_DIM].astype(BF16)
            qc = jnp.concatenate(
                [qt[(j * gq + g) * HEAD_DIM:(j * gq + g + 1) * HEAD_DIM, c0:c0 + WINDOW] for g in range(gq)],
                axis=1).astype(BF16)
            sc[j, qb] = _dot(kb, qc) + bias_ref[var, j]
    dens = []
    for j in range(N_KV_HEADS):
        s = sc[j]
        sink = sink_ref[0, j][None]
        m = jnp.maximum(jnp.max(s, axis=1, keepdims=True), sink)
        p = jnp.exp(s - m)
        dens.append(jnp.sum(p, axis=1, keepdims=True) + jnp.exp(sink - m))
        pb[j] = p.astype(BF16)
    for qb in range(nqb):
        c0 = qb * WINDOW
        pieces = []
        for j in range(N_KV_HEADS):
            vb = vtbuf[j * HEAD_DIM:(j + 1) * HEAD_DIM, c0:c0 + 2 * WINDOW].astype(BF16)
            o = _dot(vb, pb[j, qb]) / dens[j][qb]
            pieces += [o[:, g * WINDOW:(g + 1) * WINDOW] for g in range(gq)]
        ot = jnp.concatenate(pieces, axis=0)
        yb = ot.T
        if qb == 0:
            rid = lax.broadcasted_iota(jnp.int32, (WINDOW, 1), 0)
            yb = jnp.where((rid >= pad_rows) | (t > 0), yb, 0.0)
        mix[c0:c0 + WINDOW, d_ssm:] = yb.astype(BF16)

    out_b = x + _dot(mix[:, d_ssm:], wout_ref[0, d_ssm:, :])

    ya = _s5_mix(u, uext, yint, kst_ref, bst_ref, mst_ref, d_ref[0], wglu_ref, bglu_ref[0], advance)
    sre_ref[0] = hst[0:1, 0:n_state]
    sim_ref[0] = hst[0:1, n_state:]

    kbuf[0:WINDOW, :] = kbuf[tm:tm + WINDOW, :]
    vtbuf[:, 0:WINDOW] = vtbuf[:, tm:tm + WINDOW]
    ko_ref[0] = kbuf[0:WINDOW, :]
    vo_ref[0] = v[tm - WINDOW:, :]
    o_ref[0] = out_b + _dot(ya.astype(BF16), wout_ref[0, 0:d_ssm, :])


def _mixer_prompt(x, head, idx, layer, wts, tm, pad_rows):
    nb, _, dm = x.shape
    tp = x.shape[1] + (0 if head is None else head.shape[0])
    nt = tp // tm
    n_state = wts["tre"].shape[-1]
    d_ssm = wts["ssm_d"].shape[-1]
    d_kv = N_KV_HEADS * HEAD_DIM
    w = n_state // wts["kst"].shape[1]
    cw = wts["kst"].shape[-1]
    layer_spec = lambda a, i: pl.BlockSpec((1,) + a.shape[1:], lambda b, t: (i,) + (0,) * (a.ndim - 1),
                                           pipeline_mode=pl.Buffered(1))
    if head is None:
        n_pieces = 0
        ins = [(x, pl.BlockSpec((1, tm, dm), lambda b, t: (b, t, 0)))]
    else:
        assert head.shape[0] == WINDOW
        n_pieces = tm // WINDOW
        piece = lambda i: pl.BlockSpec((1, WINDOW, dm), lambda b, t: (b, jnp.maximum(t * n_pieces + i - 1, 0), 0))
        ins = [(head, _const_spec(head.shape))] + [(x, piece(i)) for i in range(n_pieces)]
    ins += [
        (wts["g_mix"], layer_spec(wts["g_mix"], layer)),
        (wts["w_ukv"], layer_spec(wts["w_ukv"], idx)),
        (wts["w_qt"], layer_spec(wts["w_qt"], idx)),
        (wts["w_vt"], layer_spec(wts["w_vt"], idx)),
        (wts["kst"], layer_spec(wts["kst"], idx)),
        (wts["bst"], layer_spec(wts["bst"], idx)),
        (wts["mst"], layer_spec(wts["mst"], idx)),
        (wts["tre"], layer_spec(wts["tre"], idx)),
        (wts["tim"], layer_spec(wts["tim"], idx)),
        (wts["ssm_d"], layer_spec(wts["ssm_d"], idx)),
        (wts["w_glu"], layer_spec(wts["w_glu"], idx)),
        (wts["b_glu"], layer_spec(wts["b_glu"], idx)),
        (wts["bias_p"], _const_spec(wts["bias_p"].shape)),
        (wts["sink_p"], layer_spec(wts["sink_p"], idx)),
        (wts["w_out"], layer_spec(wts["w_out"], idx)),
    ]
    out_shape = [
        jax.ShapeDtypeStruct((nb, tp, dm), F32),
        jax.ShapeDtypeStruct((nb, 1, n_state), F32),
        jax.ShapeDtypeStruct((nb, 1, n_state), F32),
        jax.ShapeDtypeStruct((nb, WINDOW, d_kv), F32),
        jax.ShapeDtypeStruct((nb, WINDOW, d_kv), F32),
    ]
    out_specs = [
        pl.BlockSpec((1, tm, dm), lambda b, t: (b, t, 0)),
        pl.BlockSpec((1, 1, n_state), lambda b, t: (b, 0, 0)),
        pl.BlockSpec((1, 1, n_state), lambda b, t: (b, 0, 0)),
        pl.BlockSpec((1, WINDOW, d_kv), lambda b, t: (b, 0, 0)),
        pl.BlockSpec((1, WINDOW, d_kv), lambda b, t: (b, 0, 0)),
    ]
    scratch = [
        pltpu.VMEM((SUBLANES, tm, cw), F32),
        pltpu.VMEM((tm, cw), F32),
        pltpu.VMEM((wts["kst"].shape[1], tm // SUBLANES, 2 * w), F32),
        pltpu.VMEM((SUBLANES, 2 * n_state), F32),
        pltpu.VMEM((tm + WINDOW, d_kv), F32),
        pltpu.VMEM((d_kv, tm + WINDOW), F32),
        pltpu.VMEM((tm, dm), BF16),
        pltpu.VMEM((N_KV_HEADS, tm // WINDOW) + wts["bias_p"].shape[2:], F32),
        pltpu.VMEM((N_KV_HEADS, tm // WINDOW) + wts["bias_p"].shape[2:], BF16),
    ]
    assert tm // WINDOW >= wts["bias_p"].shape[0] - 1
    return pl.pallas_call(
        functools.partial(_mixer_prompt_kernel, n_pieces=n_pieces, pad_rows=pad_rows),
        grid=(nb, nt),
        in_specs=[s for _, s in ins],
        out_specs=out_specs,
        out_shape=out_shape,
        scratch_shapes=scratch,
        compiler_params=_cparams(("arbitrary", "arbitrary")),
        name=f"mixer_prompt_l{layer}",
    )(*[a for a, _ in ins])


def _mixer_sample_kernel(x_ref, h0r_ref, h0i_ref, kc_ref, vc_ref, g_ref, win_ref, kst_ref, bst_ref, mst_ref,
                         tre_ref, tim_ref, d_ref, wglu_ref, bglu_ref, bias_ref, sink_ref, wout_ref,
                         o_ref, sre_ref, sim_ref, ko_ref, vo_ref, uext, yint):
    rows = x_ref.shape[0]
    ns = kc_ref.shape[0]
    tq = rows // ns
    assert tq == SUBLANES
    w_rows = kc_ref.shape[1]
    d_ssm = d_ref.shape[-1]
    gq = bias_ref.shape[1] // tq
    d_q = N_KV_HEADS * gq * HEAD_DIM
    d_kv = N_KV_HEADS * HEAD_DIM

    x = x_ref[...]
    h = _rmsnorm(x, g_ref[0]).astype(BF16)
    z = _dot(h, win_ref[0])
    u = z[:, :d_ssm]
    q = (z[:, d_ssm:d_ssm + d_q] * (HEAD_DIM ** -0.5)).reshape(ns, tq, d_q)
    kn = z[:, d_ssm + d_q:d_ssm + d_q + d_kv].reshape(ns, tq, d_kv)
    vn = z[:, d_ssm + d_q + d_kv:].reshape(ns, tq, d_kv)

    w = h0r_ref.shape[-1] // kst_ref.shape[1]

    def advance(s_list):
        entering = []
        for blk, s in enumerate(s_list):
            lanes = slice(blk * w, (blk + 1) * w)
            hr = h0r_ref[:, lanes]
            hi = h0i_ref[:, lanes]
            ar = tre_ref[0, 0, SUBLANES - 1:SUBLANES, lanes]
            ai = tim_ref[0, 0, SUBLANES - 1:SUBLANES, lanes]
            sre_ref[:, lanes] = ar * hr - ai * hi + s[:, 0:w]
            sim_ref[:, lanes] = ar * hi + ai * hr + s[:, w:2 * w]
            entering.append(jnp.concatenate([hr, hi], axis=1))
        return entering

    ya = _s5_mix(u, uext, yint, kst_ref, bst_ref, mst_ref, d_ref[0], wglu_ref, bglu_ref[0], advance)

    kc = jnp.concatenate([kc_ref[...], kn], axis=1)
    vc = jnp.concatenate([vc_ref[...], vn], axis=1)
    ko_ref[...] = kc[:, tq:, :]
    vo_ref[...] = vc[:, tq:, :]
    pieces = []
    for j in range(N_KV_HEADS):
        kb = kc[:, :, j * HEAD_DIM:(j + 1) * HEAD_DIM].astype(BF16)
        vb = vc[:, :, j * HEAD_DIM:(j + 1) * HEAD_DIM].astype(BF16)
        qs = jnp.concatenate(
            [q[:, :, (j * gq + g) * HEAD_DIM:(j * gq + g + 1) * HEAD_DIM] for g in range(gq)],
            axis=1).astype(BF16)
        s = jnp.einsum("nqd,nkd->nqk", qs, kb, preferred_element_type=F32) + bias_ref[j]
        sink = sink_ref[0, j]
        m = jnp.maximum(jnp.max(s, axis=-1, keepdims=True), sink)
        p = jnp.exp(s - m)
        l = jnp.sum(p, axis=-1, keepdims=True) + jnp.exp(sink - m)
        o = jnp.einsum("nqk,nkd->nqd", p.astype(BF16), vb, preferred_element_type=F32) / l
        pieces += [o[:, g * tq:(g + 1) * tq, :] for g in range(gq)]
    yb = jnp.concatenate(pieces, axis=2).reshape(rows, d_q)
    mixed = jnp.concatenate([ya, yb], axis=1).astype(BF16)
    o_ref[...] = x + _dot(mixed, wout_ref[0])


def _mixer_sample(x2, h0r, h0i, kc, vc, idx, layer, wts, ns):
    rows_all, dm = x2.shape
    _, n_seq, w_rows, d_kv = kc.shape
    tq = rows_all // n_seq
    n_state = h0r.shape[-1]
    cw = wts["kst"].shape[-1]
    layer_spec = lambda a, i: pl.BlockSpec((1,) + a.shape[1:], lambda s: (i,) + (0,) * (a.ndim - 1),
                                           pipeline_mode=pl.Buffered(1))
    row_spec = pl.BlockSpec((ns * tq, dm), lambda s: (s, 0))
    st_spec = pl.BlockSpec((ns, n_state), lambda s: (s, 0))
    kv_spec = pl.BlockSpec((ns, w_rows, d_kv), lambda s: (s, 0, 0))
    st_in = pl.BlockSpec((None, ns, n_state), lambda s: (idx, s, 0))
    kv_in = pl.BlockSpec((None, ns, w_rows, d_kv), lambda s: (idx, s, 0, 0))
    ins = [
        (x2, row_spec), (h0r, st_in), (h0i, st_in), (kc, kv_in), (vc, kv_in),
        (wts["g_mix"], layer_spec(wts["g_mix"], layer)),
        (wts["w_in"], layer_spec(wts["w_in"], idx)),
        (wts["kst"], layer_spec(wts["kst"], idx)),
        (wts["bst"], layer_spec(wts["bst"], idx)),
        (wts["mst"], layer_spec(wts["mst"], idx)),
        (wts["tre"], layer_spec(wts["tre"], idx)),
        (wts["tim"], layer_spec(wts["tim"], idx)),
        (wts["ssm_d"], layer_spec(wts["ssm_d"], idx)),
        (wts["w_glu"], layer_spec(wts["w_glu"], idx)),
        (wts["b_glu"], layer_spec(wts["b_glu"], idx)),
        (wts["bias_s"], _const_spec(wts["bias_s"].shape)),
        (wts["sink_s"], layer_spec(wts["sink_s"], idx)),
        (wts["w_out"], layer_spec(wts["w_out"], idx)),
    ]
    return pl.pallas_call(
        _mixer_sample_kernel,
        grid=(n_seq // ns,),
        in_specs=[s for _, s in ins],
        out_specs=[row_spec, st_spec, st_spec, kv_spec, kv_spec],
        out_shape=[jax.ShapeDtypeStruct(x2.shape, F32), jax.ShapeDtypeStruct(h0r.shape[1:], F32),
                   jax.ShapeDtypeStruct(h0i.shape[1:], F32), jax.ShapeDtypeStruct(kc.shape[1:], F32),
                   jax.ShapeDtypeStruct(vc.shape[1:], F32)],
        scratch_shapes=[pltpu.VMEM((SUBLANES, ns * tq, cw), F32), pltpu.VMEM((ns * tq, cw), F32)],
        compiler_params=_cparams(("arbitrary",)),
        name=f"mixer_sample_l{layer}",
    )(*[a for a, _ in ins])


def _conformer_head(x, g_ref, w1_ref, c):
    h = _rmsnorm(x, g_ref[0]).astype(BF16)
    z = _dot(h, w1_ref[0])
    return z[:, :c] * jax.nn.sigmoid(z[:, c:])


def _conformer_tail(y, bdw_ref, lng_ref, lnb_ref, w2_ref):
    y = y + bdw_ref[0]
    mu = jnp.mean(y, axis=-1, keepdims=True)
    yc = y - mu
    var = jnp.mean(yc * yc, axis=-1, keepdims=True)
    y = yc * lax.rsqrt(var + EPS) * lng_ref[0] + lnb_ref[0]
    return _dot(jax.nn.silu(y).astype(BF16), w2_ref[0])


def _conformer_prompt_kernel(x_ref, g_ref, w1_ref, wdw_ref, bdw_ref, lng_ref, lnb_ref, w2_ref, o_ref, st_ref,
                             ext, ybuf, *, pad_rows, width):
    t = pl.program_id(1)
    x = x_ref[0]
    nslab, le, lanes = ext.shape
    l = le - CONV_HALO
    c = nslab * lanes
    off = CONV_HALO - (width - 1)
    mrows = CONV_PHASE_VREGS * SUBLANES
    span = CONV_PHASES * mrows
    assert l % CONF_BLOCK == 0 and CONF_BLOCK % span == 0

    @pl.when(t == 0)
    def _():
        ext[:, 0:CONV_HALO, :] = jnp.zeros((nslab, CONV_HALO, lanes), F32)

    gl = _conformer_head(x, g_ref, w1_ref, c)
    for q in range(nslab):
        ext[q, CONV_HALO:, :] = gl[:, q * lanes:(q + 1) * lanes]
    st_ref[0] = jnp.concatenate([ext[q, l + off:, :] for q in range(nslab)], axis=1)

    def block(cb, _):
        r0 = pl.multiple_of(cb * CONF_BLOCK, CONF_BLOCK)
        for grp in range(CONF_BLOCK // span):
            g0 = r0 + grp * span
            for q0 in range(0, nslab, CONV_SLABS):
                slabs = range(q0, q0 + CONV_SLABS)
                acc = {(r, q): jnp.zeros((CONV_PHASE_VREGS, SUBLANES, lanes), F32)
                       for r in range(CONV_PHASES) for q in slabs}
                for s in range(width + CONV_PHASES - 1):
                    for q in slabs:
                        xs = ext[q, pl.ds(g0 + (off + s), mrows, stride=CONV_PHASES), :]
                        xs = xs.reshape(CONV_PHASE_VREGS, SUBLANES, lanes)
                        for r in range(CONV_PHASES):
                            k = s - r
                            if 0 <= k < width:
                                wk = wdw_ref[0, k, :, q * lanes:(q + 1) * lanes]
                                acc[r, q] = acc[r, q] + wk[None] * xs
                for r in range(CONV_PHASES):
                    for q in slabs:
                        ybuf[q, pl.ds(g0 + r, mrows, stride=CONV_PHASES), :] = acc[r, q].reshape(mrows, lanes)
        y = jnp.concatenate([ybuf[q, pl.ds(r0, CONF_BLOCK), :] for q in range(nslab)], axis=1)
        out = _conformer_tail(y, bdw_ref, lng_ref, lnb_ref, w2_ref)
        rid = t * l + r0 + lax.broadcasted_iota(jnp.int32, (CONF_BLOCK, 1), 0)
        o_ref[0, pl.ds(r0, CONF_BLOCK), :] = x_ref[0, pl.ds(r0, CONF_BLOCK), :] + jnp.where(rid >= pad_rows, out, 0.0)
        return 0

    lax.fori_loop(0, l // CONF_BLOCK, block, 0)
    ext[:, 0:CONV_HALO, :] = ext[:, l:l + CONV_HALO, :]


def _conformer_sample_kernel(x_ref, past_ref, g_ref, w1_ref, wdw_ref, bdw_ref, lng_ref, lnb_ref, w2_ref,
                             o_ref, st_ref, ext, ybuf, stage, *, width):
    x = x_ref[...]
    ns, le, c = ext.shape
    l = le - CONV_HALO
    off = CONV_HALO - (width - 1)
    ext[:, 0:off, :] = jnp.zeros((ns, off, c), F32)
    ext[:, off:CONV_HALO, :] = past_ref[...]
    ext[:, CONV_HALO:, :] = _conformer_head(x, g_ref, w1_ref, c).reshape(ns, l, c)
    st_ref[...] = ext[:, l + off:, :]

    def conv_unit(n, stg):
        win = ext[n]
        for s in range(1, SUBLANES):
            stg[s - 1] = win[s:s + l + CONV_HALO - SUBLANES]
        acc = jnp.zeros((l // SUBLANES, SUBLANES, c), F32)
        for k in range(width):
            a, s = divmod(off + k, SUBLANES)
            if s == 0:
                xk = ext[n, a * SUBLANES:a * SUBLANES + l, :]
            else:
                xk = stg[s - 1, a * SUBLANES:a * SUBLANES + l, :]
            acc = acc + wdw_ref[0, k][None] * xk.reshape(l // SUBLANES, SUBLANES, c)
        return acc.reshape(l, c)

    def seq(i, _):
        for half in range(2):
            n = 2 * i + half
            ybuf[pl.ds(pl.multiple_of(n * l, l), l), :] = conv_unit(n, stage.at[half])
        return 0
    assert ns % 2 == 0 and l % SUBLANES == 0
    lax.fori_loop(0, ns // 2, seq, 0)
    o_ref[...] = x + _conformer_tail(ybuf[...], bdw_ref, lng_ref, lnb_ref, w2_ref)


def _conformer(x, past, idx, layer, wts, tile, pad_rows):
    carry = past is None
    width = wts["w_dw"].shape[1]
    c = wts["w_dw"].shape[3]
    if carry:
        nb, tp, dm = x.shape
        grid = (nb, tp // tile)
        lspec = lambda a, i: pl.BlockSpec((1,) + a.shape[1:], lambda b, t: (i,) + (0,) * (a.ndim - 1),
                                          pipeline_mode=pl.Buffered(1))
        xspec = pl.BlockSpec((1, tile, dm), lambda b, t: (b, t, 0))
        ins = [(x, xspec)]
        st_shape = (nb, width - 1, c)
        st_spec = pl.BlockSpec((1, width - 1, c), lambda b, t: (b, 0, 0))
        ns, l = 1, tile
        sem = ("arbitrary", "arbitrary")
    else:
        rows_all, dm = x.shape
        n_seq = past.shape[1]
        l = rows_all // n_seq
        ns = tile
        grid = (n_seq // ns,)
        lspec = lambda a, i: pl.BlockSpec((1,) + a.shape[1:], lambda s: (i,) + (0,) * (a.ndim - 1),
                                          pipeline_mode=pl.Buffered(1))
        xspec = pl.BlockSpec((ns * l, dm), lambda s: (s, 0))
        ins = [(x, xspec), (past, pl.BlockSpec((None, ns, width - 1, c), lambda s: (idx, s, 0, 0)))]
        st_shape = (n_seq, width - 1, c)
        st_spec = pl.BlockSpec((ns, width - 1, c), lambda s: (s, 0, 0))
        sem = ("arbitrary",)
    ins += [
        (wts["g_mix"], lspec(wts["g_mix"], layer)),
        (wts["w_pw1"], lspec(wts["w_pw1"], idx)),
        (wts["w_dw"], lspec(wts["w_dw"], idx)),
        (wts["b_dw"], lspec(wts["b_dw"], idx)),
        (wts["ln_g"], lspec(wts["ln_g"], idx)),
        (wts["ln_b"], lspec(wts["ln_b"], idx)),
        (wts["w_pw2"], lspec(wts["w_pw2"], idx)),
    ]
    if carry:
        body = functools.partial(_conformer_prompt_kernel, pad_rows=pad_rows, width=width)
        scratch = [pltpu.VMEM((c // LANES, l + CONV_HALO, LANES), F32), pltpu.VMEM((c // LANES, l, LANES), F32)]
    else:
        body = functools.partial(_conformer_sample_kernel, width=width)
        scratch = [pltpu.VMEM((ns, l + CONV_HALO, c), F32), pltpu.VMEM((ns * l, c), F32),
                   pltpu.VMEM((2, SUBLANES - 1, l + CONV_HALO - SUBLANES, c), F32)]
    return pl.pallas_call(
        body,
        grid=grid,
        in_specs=[s for _, s in ins],
        out_specs=[xspec, st_spec],
        out_shape=[jax.ShapeDtypeStruct(x.shape, F32), jax.ShapeDtypeStruct(st_shape, F32)],
        scratch_shapes=scratch,
        compiler_params=_cparams(sem),
        name=f"conformer_{'prompt' if carry else 'sample'}_l{layer}",
    )(*[a for a, _ in ins])


def _ffn_kernel(*refs, carry, final_norm, n_pieces):
    if n_pieces:
        halo_ref, piece_refs, refs = refs[0], refs[1:1 + n_pieces], refs[1 + n_pieces:]
        x = jnp.concatenate([r[0] for r in piece_refs], axis=0)
    elif carry:
        x_ref, refs = refs[0], refs[1:]
        x = x_ref[0]
    else:
        x_ref, past_ref, refs = refs[0], refs[1], refs[2:]
        x = x_ref[...]
    (g_ref, wup_ref, wcv_ref, bcv_ref, wdn_ref, gf_ref, o_ref, st_ref, ext) = refs
    ns, le, dff = ext.shape
    l = le - FFN_HALO
    rows = ns * l
    kw = wcv_ref.shape[1]

    if carry:
        @pl.when(pl.program_id(1) == 0)
        def _():
            if n_pieces:
                hh = _rmsnorm(halo_ref[0, WINDOW - FFN_HALO:, :], g_ref[0]).astype(BF16)
                ext[:, 0:FFN_HALO, :] = _dot(hh, wup_ref[0, :, 0:dff]).reshape(ns, FFN_HALO, dff)
            else:
                ext[:, 0:FFN_HALO, :] = jnp.zeros((ns, FFN_HALO, dff), F32)
    else:
        ext[:, FFN_HALO - (kw - 1):FFN_HALO, :] = past_ref[...]

    h = _rmsnorm(x, g_ref[0]).astype(BF16)
    gate = _dot(h, wup_ref[0, :, 0:dff])
    up = _dot(h, wup_ref[0, :, dff:])
    ext[:, FFN_HALO:, :] = gate.reshape(ns, l, dff)
    gc = bcv_ref[0]
    for k in range(kw):
        o = FFN_HALO - (kw - 1) + k
        gc = gc + wcv_ref[0, k:k + 1, :] * ext[:, o:o + l, :]
    y = (jax.nn.gelu(gc).reshape(rows, dff) * up).astype(BF16)
    acc = x + _dot(y, wdn_ref[0])
    st_ref[...] = ext[:, le - (kw - 1):, :].reshape(st_ref.shape)
    if final_norm:
        acc = _rmsnorm(acc, gf_ref[...])
    if carry:
        o_ref[0] = acc
        ext[:, 0:FFN_HALO, :] = ext[:, l:l + FFN_HALO, :]
    else:
        o_ref[...] = acc


def _ffn(x, past, layer, wts, tile, final_norm, skip_rows=0):
    carry = past is None
    dff = wts["w_dn"].shape[1]
    kw = wts["w_cv"].shape[1]
    n_pieces = 0
    if carry:
        nb, tp, dm = x.shape
        out_rows = tp - skip_rows
        grid = (nb, out_rows // tile)
        lspec = lambda a, i: pl.BlockSpec((1,) + a.shape[1:], lambda b, t: (i,) + (0,) * (a.ndim - 1),
                                          pipeline_mode=pl.Buffered(1))
        xspec = pl.BlockSpec((1, tile, dm), lambda b, t: (b, t, 0))
        if skip_rows:
            assert skip_rows % WINDOW == 0 and tile % WINDOW == 0 and out_rows % tile == 0
            n_pieces, skip = tile // WINDOW, skip_rows // WINDOW
            piece = lambda i: pl.BlockSpec((1, WINDOW, dm), lambda b, t: (b, skip + t * n_pieces + i, 0))
            ins = [(x, piece(-1))] + [(x, piece(i)) for i in range(n_pieces)]
        else:
            ins = [(x, xspec)]
        out_struct = jax.ShapeDtypeStruct((nb, out_rows, dm), F32)
        st_shape = (nb, kw - 1, dff)
        st_spec = pl.BlockSpec((1, kw - 1, dff), lambda b, t: (b, 0, 0))
        ns, l = 1, tile
        sem = ("arbitrary", "arbitrary")
    else:
        out_struct = jax.ShapeDtypeStruct(x.shape, F32)
        rows_all, dm = x.shape
        n_seq = past.shape[1]
        l = rows_all // n_seq
        ns = tile
        grid = (n_seq // ns,)
        lspec = lambda a, i: pl.BlockSpec((1,) + a.shape[1:], lambda s: (i,) + (0,) * (a.ndim - 1),
                                          pipeline_mode=pl.Buffered(1))
        xspec = pl.BlockSpec((ns * l, dm), lambda s: (s, 0))
        ins = [(x, xspec), (past, pl.BlockSpec((None, ns, kw - 1, dff), lambda s: (layer, s, 0, 0)))]
        st_shape = (n_seq, kw - 1, dff)
        st_spec = pl.BlockSpec((ns, kw - 1, dff), lambda s: (s, 0, 0))
        sem = ("arbitrary",)
    ins += [
        (wts["g_ffn"], lspec(wts["g_ffn"], layer)),
        (wts["w_up"], lspec(wts["w_up"], layer)),
        (wts["w_cv"], lspec(wts["w_cv"], layer)),
        (wts["b_cv"], lspec(wts["b_cv"], layer)),
        (wts["w_dn"], lspec(wts["w_dn"], layer)),
        (wts["g_final"], _const_spec(wts["g_final"].shape)),
    ]
    return pl.pallas_call(
        functools.partial(_ffn_kernel, carry=carry, final_norm=final_norm, n_pieces=n_pieces),
        grid=grid,
        in_specs=[s for _, s in ins],
        out_specs=[xspec, st_spec],
        out_shape=[out_struct, jax.ShapeDtypeStruct(st_shape, F32)],
        scratch_shapes=[pltpu.VMEM((ns, l + FFN_HALO, dff), F32)],
        compiler_params=_cparams(sem),
        name=f"ffn_{'prompt' if carry else 'sample'}_l{layer}",
    )(*[a for a, _ in ins])


def _pick_tile(total, target):
    best = WINDOW
    for m in range(1, total // WINDOW + 1):
        if total % (m * WINDOW) == 0 and m * WINDOW <= target:
            best = m * WINDOW
    return best


def kernel(x_prompt, x_sample, state_ssm_re, state_ssm_im, cache_swa_k, cache_swa_v, state_conv, state_ffn,
           meta_tokens, g_mix, g_ffn, g_final, w_in_mix, ssm_lambda_re, ssm_lambda_im, ssm_log_step,
           ssm_b_re, ssm_b_im, ssm_c_re, ssm_c_im, ssm_d, ssm_w_glu, ssm_b_glu, rel_bias, attn_sinks,
           w_out_mix, conv_w_pw1, conv_w_dw, conv_b_dw, conv_ln_g, conv_ln_b, conv_w_pw2,
           ffn_w_up, ffn_w_conv, ffn_b_conv, ffn_w_down):
    nb, seq, dm = x_prompt.shape
    n_seq, tq, _ = x_sample.shape
    depth = g_mix.shape[0]
    n_meta = meta_tokens.shape[0]
    n_even, g_ssm, p_ssm = ssm_lambda_re.shape
    n_heads = rel_bias.shape[1]
    gq = n_heads // N_KV_HEADS
    w_rows = cache_swa_k.shape[2]
    d_kv = N_KV_HEADS * HEAD_DIM
    conv_w = conv_w_dw.shape[1]
    ffn_w = ffn_w_conv.shape[1]
    assert tq == SUBLANES and w_rows == WINDOW and seq % WINDOW == 0 and n_meta <= WINDOW
    assert conv_w - 1 <= CONV_HALO and ffn_w - 1 <= FFN_HALO

    tre, tim, kst, bst, mst = _ssm_prep(ssm_lambda_re, ssm_lambda_im, ssm_log_step,
                                        ssm_b_re, ssm_b_im, ssm_c_re, ssm_c_im)
    bias = _bias_table(rel_bias)
    row3 = lambda a: a.reshape(a.shape[0], 1, a.shape[-1]).astype(F32)
    sinks = attn_sinks.astype(F32).reshape(n_even, N_KV_HEADS, gq, 1)
    pad_rows = WINDOW - n_meta
    bias_t = jnp.transpose(bias.reshape(N_KV_HEADS, gq, WINDOW, 2 * WINDOW), (0, 3, 1, 2)
                           ).reshape(N_KV_HEADS, 2 * WINDOW, gq * WINDOW)
    key_i = jnp.arange(2 * WINDOW)[None, :, None]
    n_var = 1 + -(-(pad_rows + WINDOW) // WINDOW)
    bias_p = jnp.stack([bias_t] + [jnp.where(key_i >= pad_rows + WINDOW - qb * WINDOW, bias_t, NEG_INF)
                                   for qb in range(n_var - 1)])
    d_ssm = ssm_d.shape[-1]
    d_q = n_heads * HEAD_DIM
    w_in_bf = w_in_mix.astype(BF16)
    wts = {
        "g_mix": row3(g_mix), "g_ffn": row3(g_ffn), "g_final": g_final.reshape(1, dm).astype(F32),
        "w_in": w_in_bf, "kst": kst, "bst": bst, "mst": mst, "tre": tre, "tim": tim,
        "w_ukv": jnp.concatenate([w_in_bf[:, :, :d_ssm], w_in_bf[:, :, d_ssm + d_q:]], axis=-1),
        "w_qt": jnp.transpose(w_in_bf[:, :, d_ssm:d_ssm + d_q], (0, 2, 1)) * (HEAD_DIM ** -0.5),
        "w_vt": jnp.transpose(w_in_bf[:, :, d_ssm + d_q + d_kv:], (0, 2, 1)),
        "ssm_d": row3(ssm_d), "w_glu": ssm_w_glu.astype(BF16), "b_glu": row3(ssm_b_glu),
        "bias_p": bias_p,
        "bias_s": bias[:, :tq, :w_rows + tq].reshape(N_KV_HEADS, gq * tq, w_rows + tq),
        "sink_p": jnp.broadcast_to(sinks[:, :, :, None, :], (n_even, N_KV_HEADS, gq, WINDOW, 1)
                                   ).reshape(n_even, N_KV_HEADS, 1, gq * WINDOW),
        "sink_s": jnp.broadcast_to(sinks[:, :, :, None, :], (n_even, N_KV_HEADS, gq, tq, 1)
                                   ).reshape(n_even, N_KV_HEADS, gq * tq, 1),
        "w_out": w_out_mix.astype(BF16),
        "w_pw1": conv_w_pw1.astype(BF16), "w_dw": jnp.broadcast_to(conv_w_dw.astype(F32)[:, :, None, :],
                                 conv_w_dw.shape[:2] + (SUBLANES, conv_w_dw.shape[2])), "b_dw": row3(conv_b_dw),
        "ln_g": row3(conv_ln_g), "ln_b": row3(conv_ln_b), "w_pw2": conv_w_pw2.astype(BF16),
        "w_up": ffn_w_up.astype(BF16), "w_cv": ffn_w_conv.astype(F32), "b_cv": row3(ffn_b_conv),
        "w_dn": ffn_w_down.astype(BF16),
    }

    tp = pad_rows + n_meta + seq
    tm = _pick_tile(tp, PROMPT_TILE_TARGET)
    xp = x_prompt.astype(F32)
    head = jnp.concatenate([jnp.zeros((pad_rows, dm), F32), meta_tokens.astype(F32)], axis=0)
    xs = x_sample.astype(F32).reshape(n_seq * tq, dm)
    ns = min(SAMPLE_SEQS, n_seq)
    assert n_seq % ns == 0

    h0r_all = state_ssm_re.astype(F32).reshape(n_even, n_seq, g_ssm * p_ssm)
    h0i_all = state_ssm_im.astype(F32).reshape(n_even, n_seq, g_ssm * p_ssm)
    kc_all = cache_swa_k.astype(F32).reshape(n_even, n_seq, w_rows, d_kv)
    vc_all = cache_swa_v.astype(F32).reshape(n_even, n_seq, w_rows, d_kv)
    conv_all = state_conv.astype(F32)
    ffn_all = state_ffn.astype(F32)

    sr_p, si_p, k_p, v_p, c_p, f_p = [], [], [], [], [], []
    sr_s, si_s, k_s, v_s, c_s, f_s = [], [], [], [], [], []
    for layer in range(depth):
        idx = layer // 2
        if layer % 2 == 0:
            xp, sre, sim, ko, vo = _mixer_prompt(xp, head if layer == 0 else None, idx, layer, wts, tm, pad_rows)
            sr_p.append(sre.reshape(nb, g_ssm, p_ssm))
            si_p.append(sim.reshape(nb, g_ssm, p_ssm))
            k_p.append(ko.reshape(nb, WINDOW, N_KV_HEADS, HEAD_DIM))
            v_p.append(vo.reshape(nb, WINDOW, N_KV_HEADS, HEAD_DIM))
            xs, sre, sim, ko, vo = _mixer_sample(xs, h0r_all, h0i_all, kc_all, vc_all, idx, layer, wts, ns)
            sr_s.append(sre.reshape(n_seq, g_ssm, p_ssm))
            si_s.append(sim.reshape(n_seq, g_ssm, p_ssm))
            k_s.append(ko.reshape(n_seq, w_rows, N_KV_HEADS, HEAD_DIM))
            v_s.append(vo.reshape(n_seq, w_rows, N_KV_HEADS, HEAD_DIM))
        else:
            xp, st = _conformer(xp, None, idx, layer, wts, tm, pad_rows)
            c_p.append(st)
            xs, st = _conformer(xs, conv_all, idx, layer, wts, ns, 0)
            c_s.append(st)
        last = layer == depth - 1
        if last:
            xp, st = _ffn(xp, None, layer, wts, _pick_tile(seq, PROMPT_TILE_TARGET), True,
                          skip_rows=pad_rows + n_meta)
        else:
            xp, st = _ffn(xp, None, layer, wts, tm, False)
        f_p.append(st)
        xs, st = _ffn(xs, ffn_all, layer, wts, ns, last)
        f_s.append(st)

    yp = xp
    ys = xs.reshape(n_seq, tq, dm)
    st = jnp.stack
    return (yp, ys, st(sr_p), st(si_p), st(k_p), st(v_p), st(c_p), st(f_p),
            st(sr_s), st(si_s), st(k_s), st(v_s), st(c_s), st(f_s))
```
